```python
import jax, jax.numpy as jnp
from jax import lax
import numpy as np

D_MODEL = 1024
BATCH = 8
SEQ = 4096
DEPTH = 4

CTX_LEN = 256
GRID_W = 64
N_MIXERS = 3
NORM_EPS = 1e-6
ROPE_THETA = 10000.0
Q_BLOCK = 128
FOURIER_GROUPS = 8
FOURIER_GROUP_DIM = D_MODEL // FOURIER_GROUPS
MLA_HEADS = 16
MLA_Q_LORA = 384
MLA_KV_LORA = 256
MLA_NOPE = 64
MLA_ROPE = 32
MLA_V = 64
GQA_HEADS = 16
GQA_KV_HEADS = 4
GQA_HEAD_DIM = 64
N_EXPERTS = 32
TOP_K = 4
D_EXPERT = 1024
SWIGLU_LIMIT = 7.0
SWIGLU_ALPHA = 1.702
EXPERT_BLOCK = 256
N_MOD = 6

kernel_name = "hybrid_fourier_mla_gqa_moe_dit"


def rms_norm(x, g):
    xf = x.astype(jnp.float32)
    y = xf * lax.rsqrt(jnp.mean(xf * xf, axis=-1, keepdims=True) + NORM_EPS)
    return (y * g.astype(jnp.float32)).astype(x.dtype)


def modulate(h, shift, scale):
    return h * (1 + scale) + shift


def axial_rope_angles(n_tok, rot_dim):
    rows = n_tok // GRID_W
    row = jnp.repeat(jnp.arange(rows, dtype=jnp.float32), GRID_W)
    col = jnp.tile(jnp.arange(GRID_W, dtype=jnp.float32), rows)
    n_freq = rot_dim // 4
    inv_freq = ROPE_THETA ** (-jnp.arange(n_freq, dtype=jnp.float32) / n_freq)
    ang = jnp.stack([row[:, None] * inv_freq, col[:, None] * inv_freq], axis=1)
    return jnp.cos(ang), jnp.sin(ang)


def apply_axial_rope(x, cos, sin):
    B, L, H, R = x.shape
    xf = x.astype(jnp.float32).reshape(B, L, H, 2, 2, R // 4)
    x1, x2 = xf[..., 0, :], xf[..., 1, :]
    cs, sn = cos[None, :, None], sin[None, :, None]
    out = jnp.stack([x1 * cs - x2 * sn, x2 * cs + x1 * sn], axis=-2)
    return out.reshape(B, L, H, R).astype(x.dtype)


def block_attention(q, k, v, scale):
    B, Lq, H, dk = q.shape
    Hkv, dv = k.shape[2], v.shape[3]
    grp = H // Hkv
    nb = Lq // Q_BLOCK
    kf = k.astype(jnp.float32)
    vf = v.astype(jnp.float32)
    qb = jnp.moveaxis(q.reshape(B, nb, Q_BLOCK, Hkv, grp, dk), 1, 0)

    def one_block(qblk):
        s = jnp.einsum("bqhgd,bkhd->bhgqk", qblk.astype(jnp.float32), kf) * scale
        p = jax.nn.softmax(s, axis=-1)
        return jnp.einsum("bhgqk,bkhd->bqhgd", p, vf)

    o = lax.map(one_block, qb)
    return jnp.moveaxis(o, 0, 1).reshape(B, Lq, H, dv).astype(q.dtype)


def fourier_mixer(h, w_in, w_out):
    B, L, D = h.shape
    u = (h @ w_in).astype(jnp.float32).reshape(B, L, FOURIER_GROUPS, FOURIER_GROUP_DIM)
    f = jnp.fft.fft2(u, axes=(1, 3), norm="ortho").real
    return f.reshape(B, L, D).astype(h.dtype) @ w_out


def mla_q(a, g_qa, w_qb, rope):
    B, L = a.shape[:2]
    q = (rms_norm(a[..., :MLA_Q_LORA], g_qa) @ w_qb).reshape(B, L, MLA_HEADS, MLA_NOPE + MLA_ROPE)
    q_nope, q_pe = q[..., :MLA_NOPE], q[..., MLA_NOPE:]
    if rope is not None:
        q_pe = apply_axial_rope(q_pe, rope[0], rope[1])
    return jnp.concatenate([q_nope, q_pe], axis=-1)


def mla_kv(a, g_kva, w_kvb, rope):
    B, L = a.shape[:2]
    c_kv = rms_norm(a[..., MLA_Q_LORA:MLA_Q_LORA + MLA_KV_LORA], g_kva)
    kv = (c_kv @ w_kvb).reshape(B, L, MLA_HEADS, MLA_NOPE + MLA_V)
    k_pe = a[..., MLA_Q_LORA + MLA_KV_LORA:][:, :, None, :]
    if rope is not None:
        k_pe = apply_axial_rope(k_pe, rope[0], rope[1])
    k = jnp.concatenate([kv[..., :MLA_NOPE], jnp.broadcast_to(k_pe, (B, L, MLA_HEADS, MLA_ROPE))], axis=-1)
    return k, kv[..., MLA_NOPE:]


def mla_mixer(h_ctx, h_lat, need_ctx_out, w_in, g_qa, w_qb, g_kva, w_kvb, w_o):
    B, L, _ = h_lat.shape
    rope = axial_rope_angles(L, MLA_ROPE)
    a_c = h_ctx @ w_in
    a_l = h_lat @ w_in
    k_c, v_c = mla_kv(a_c, g_kva, w_kvb, None)
    k_l, v_l = mla_kv(a_l, g_kva, w_kvb, rope)
    q_l = mla_q(a_l, g_qa, w_qb, rope)
    scale = (MLA_NOPE + MLA_ROPE) ** -0.5
    o_l = block_attention(q_l, jnp.concatenate([k_c, k_l], axis=1), jnp.concatenate([v_c, v_l], axis=1), scale)
    out_l = o_l.reshape(B, L, MLA_HEADS * MLA_V) @ w_o
    out_c = None
    if need_ctx_out:
        o_c = block_attention(mla_q(a_c, g_qa, w_qb, None), k_c, v_c, scale)
        out_c = o_c.reshape(B, h_ctx.shape[1], MLA_HEADS * MLA_V) @ w_o
    return out_c, out_l


GQA_NQ = GQA_HEADS * GQA_HEAD_DIM
GQA_NKV = GQA_KV_HEADS * GQA_HEAD_DIM


def gqa_q(a, g_q):
    B, L = a.shape[:2]
    return rms_norm(a[..., :GQA_NQ].reshape(B, L, GQA_HEADS, GQA_HEAD_DIM), g_q)


def gqa_kv(a, g_k):
    B, L = a.shape[:2]
    k = rms_norm(a[..., GQA_NQ:GQA_NQ + GQA_NKV].reshape(B, L, GQA_KV_HEADS, GQA_HEAD_DIM), g_k)
    v = a[..., GQA_NQ + GQA_NKV:].reshape(B, L, GQA_KV_HEADS, GQA_HEAD_DIM)
    return k, v


def gqa_mixer(h_ctx, h_lat, need_ctx_out, w_qkv, g_q, g_k, w_o):
    B, L, _ = h_lat.shape
    cos, sin = axial_rope_angles(L, GQA_HEAD_DIM)
    a_c = h_ctx @ w_qkv
    a_l = h_lat @ w_qkv
    k_c, v_c = gqa_kv(a_c, g_k)
    k_l, v_l = gqa_kv(a_l, g_k)
    k_l = apply_axial_rope(k_l, cos, sin)
    q_l = apply_axial_rope(gqa_q(a_l, g_q), cos, sin)
    scale = GQA_HEAD_DIM ** -0.5
    o_l = block_attention(q_l, jnp.concatenate([k_c, k_l], axis=1), jnp.concatenate([v_c, v_l], axis=1), scale)
    out_l = o_l.reshape(B, L, GQA_NQ) @ w_o
    out_c = None
    if need_ctx_out:
        o_c = block_attention(gqa_q(a_c, g_q), k_c, v_c, scale)
        out_c = o_c.reshape(B, h_ctx.shape[1], GQA_NQ) @ w_o
    return out_c, out_l


def moe_ffn(h, w_router, b_router, w_gu, b_gu, w_down, b_down):
    N, D = h.shape
    logits = h.astype(jnp.float32) @ w_router.astype(jnp.float32) + b_router.astype(jnp.float32)
    top_logit, top_idx = lax.top_k(logits, TOP_K)
    top_w = jax.nn.softmax(top_logit, axis=-1)
    n_assign = N * TOP_K
    flat_e = top_idx.reshape(-1)
    flat_tok = jnp.arange(n_assign, dtype=jnp.int32) // TOP_K
    flat_w = top_w.reshape(-1)
    order = jnp.argsort(flat_e)
    e_sorted = flat_e[order]
    counts = jnp.bincount(flat_e, length=N_EXPERTS)
    padded = (counts + EXPERT_BLOCK - 1) // EXPERT_BLOCK * EXPERT_BLOCK
    pad_end = jnp.cumsum(padded)
    pad_start = pad_end - padded
    grp_start = jnp.cumsum(counts) - counts
    rank = jnp.arange(n_assign, dtype=jnp.int32) - grp_start[e_sorted]
    dest = pad_start[e_sorted] + rank
    n_blocks = (n_assign + N_EXPERTS * (EXPERT_BLOCK - 1) + EXPERT_BLOCK - 1) // EXPERT_BLOCK
    n_slots = n_blocks * EXPERT_BLOCK
    slot_tok = jnp.full((n_slots,), N, jnp.int32).at[dest].set(flat_tok[order])
    slot_w = jnp.zeros((n_slots,), jnp.float32).at[dest].set(flat_w[order])
    block_start = jnp.arange(n_blocks, dtype=pad_end.dtype) * EXPERT_BLOCK
    block_expert = jnp.minimum(jnp.searchsorted(pad_end, block_start, side="right"), N_EXPERTS - 1)
    h_pad = jnp.concatenate([h, jnp.zeros((1, D), h.dtype)], axis=0)

    def expert_block(args):
        tok, wt, e = args
        gu = h_pad[tok] @ w_gu[e] + b_gu[e]
        gate = jnp.minimum(gu[..., ::2], SWIGLU_LIMIT)
        up = jnp.clip(gu[..., 1::2], -SWIGLU_LIMIT, SWIGLU_LIMIT)
        glu = gate * jax.nn.sigmoid(SWIGLU_ALPHA * gate)
        y = ((up + 1) * glu) @ w_down[e] + b_down[e]
        return y.astype(jnp.float32) * wt[:, None]

    y = lax.map(expert_block, (slot_tok.reshape(n_blocks, EXPERT_BLOCK),
                               slot_w.reshape(n_blocks, EXPERT_BLOCK), block_expert))
    out = jax.ops.segment_sum(y.reshape(n_slots, D), slot_tok, num_segments=N + 1)[:N]
    return out.astype(h.dtype)


def setup_inputs(seed: int = 0) -> dict:
    key = jax.random.key(seed)
    keys = jax.random.split(key, 32)
    ctr = iter(range(32))
    D = D_MODEL
    n_f = len(range(0, DEPTH, N_MIXERS))
    n_m = len(range(1, DEPTH, N_MIXERS))
    n_g = len(range(2, DEPTH, N_MIXERS))

    def nrm(shape, s):
        return jax.random.normal(keys[next(ctr)], shape, jnp.float32) * s

    def gain(shape):
        return 1.0 + nrm(shape, 0.02)

    return {
        "x": nrm((BATCH, SEQ, D), 1.0),
        "c": nrm((BATCH, D), 1.0),
        "ctx": nrm((BATCH, CTX_LEN, D), 1.0),
        "c_ctx": nrm((D,), 1.0),
        "w_mod": nrm((DEPTH, D, N_MOD * D), 0.5 * D ** -0.5),
        "b_mod": nrm((DEPTH, N_MOD * D), 0.01),
        "g_mix": gain((DEPTH, D)),
        "g_ffn": gain((DEPTH, D)),
        "g_final": gain((D,)),
        "f_w_in": nrm((n_f, D, D), D ** -0.5),
        "f_w_out": nrm((n_f, D, D), D ** -0.5),
        "mla_w_in": nrm((n_m, D, MLA_Q_LORA + MLA_KV_LORA + MLA_ROPE), D ** -0.5),
        "mla_g_qa": gain((n_m, MLA_Q_LORA)),
        "mla_w_qb": nrm((n_m, MLA_Q_LORA, MLA_HEADS * (MLA_NOPE + MLA_ROPE)), MLA_Q_LORA ** -0.5),
        "mla_g_kva": gain((n_m, MLA_KV_LORA)),
        "mla_w_kvb": nrm((n_m, MLA_KV_LORA, MLA_HEADS * (MLA_NOPE + MLA_V)), MLA_KV_LORA ** -0.5),
        "mla_w_o": nrm((n_m, MLA_HEADS * MLA_V, D), (MLA_HEADS * MLA_V) ** -0.5),
        "gqa_w_qkv": nrm((n_g, D, GQA_NQ + 2 * GQA_NKV), D ** -0.5),
        "gqa_g_q": gain((n_g, GQA_HEAD_DIM)),
        "gqa_g_k": gain((n_g, GQA_HEAD_DIM)),
        "gqa_w_o": nrm((n_g, GQA_NQ, D), GQA_NQ ** -0.5),
        "moe_w_router": nrm((DEPTH, D, N_EXPERTS), D ** -0.5),
        "moe_b_router": nrm((DEPTH, N_EXPERTS), 0.01),
        "moe_w_gu": nrm((DEPTH, N_EXPERTS, D, 2 * D_EXPERT), D ** -0.5),
        "moe_b_gu": nrm((DEPTH, N_EXPERTS, 2 * D_EXPERT), 0.01),
        "moe_w_down": nrm((DEPTH, N_EXPERTS, D_EXPERT, D), D_EXPERT ** -0.5),
        "moe_b_down": nrm((DEPTH, N_EXPERTS, D), 0.01),
    }


def reference(x, c, ctx, c_ctx, w_mod, b_mod, g_mix, g_ffn, g_final, f_w_in, f_w_out,
              mla_w_in, mla_g_qa, mla_w_qb, mla_g_kva, mla_w_kvb, mla_w_o,
              gqa_w_qkv, gqa_g_q, gqa_g_k, gqa_w_o,
              moe_w_router, moe_b_router, moe_w_gu, moe_b_gu, moe_w_down, moe_b_down):
    D = x.shape[-1]
    lat, cx = x, ctx
    for i in range(DEPTH):
        kind, j = i % N_MIXERS, i // N_MIXERS
        last = i == DEPTH - 1
        B, L, _ = lat.shape
        n_c = cx.shape[1]
        m_l = (jax.nn.silu(c) @ w_mod[i] + b_mod[i]).reshape(B, N_MOD, 1, D)
        h_l = modulate(rms_norm(lat, g_mix[i]), m_l[:, 0], m_l[:, 1])
        ctx_used = (kind != 0) or (not last)
        if ctx_used:
            m_c = (jax.nn.silu(c_ctx) @ w_mod[i] + b_mod[i]).reshape(N_MOD, D)
            h_c = modulate(rms_norm(cx, g_mix[i]), m_c[0], m_c[1])
        if kind == 0:
            o_l = fourier_mixer(h_l, f_w_in[j], f_w_out[j])
            o_c = None if last else fourier_mixer(h_c, f_w_in[j], f_w_out[j])
        elif kind == 1:
            o_c, o_l = mla_mixer(h_c, h_l, not last, mla_w_in[j], mla_g_qa[j], mla_w_qb[j],
                                 mla_g_kva[j], mla_w_kvb[j], mla_w_o[j])
        else:
            o_c, o_l = gqa_mixer(h_c, h_l, not last, gqa_w_qkv[j], gqa_g_q[j], gqa_g_k[j], gqa_w_o[j])
        lat = lat + m_l[:, 2] * o_l
        h2_l = modulate(rms_norm(lat, g_ffn[i]), m_l[:, 3], m_l[:, 4]).reshape(B * L, D)
        moe_p = (moe_w_router[i], moe_b_router[i], moe_w_gu[i], moe_b_gu[i], moe_w_down[i], moe_b_down[i])
        if last:
            f_l = moe_ffn(h2_l, *moe_p)
        else:
            cx = cx + m_c[2] * o_c
            h2_c = modulate(rms_norm(cx, g_ffn[i]), m_c[3], m_c[4]).reshape(B * n_c, D)
            f_all = moe_ffn(jnp.concatenate([h2_c, h2_l], axis=0), *moe_p)
            cx = cx + m_c[5] * f_all[:B * n_c].reshape(B, n_c, D)
            f_l = f_all[B * n_c:]
        lat = lat + m_l[:, 5] * f_l.reshape(B, L, D)
    return rms_norm(lat, g_final)
```

```python
import functools

import jax
import jax.numpy as jnp
from jax import lax
from jax.experimental import pallas as pl
from jax.experimental.pallas import tpu as pltpu

F32 = jnp.float32
BF16 = jnp.bfloat16
I32 = jnp.int32
HI = lax.Precision.HIGHEST

GRID_W = 64
N_MIXERS = 3
NORM_EPS = 1e-6
ROPE_THETA = 10000.0
FOURIER_GROUPS = 8
MLA_HEADS = 16
MLA_Q_LORA = 384
MLA_KV_LORA = 256
MLA_NOPE = 64
MLA_ROPE = 32
MLA_V = 64
GQA_HEADS = 16
GQA_KV_HEADS = 4
GQA_HEAD_DIM = 64
TOP_K = 4
SWIGLU_LIMIT = 7.0
SWIGLU_ALPHA = 1.702
N_MOD = 6

LANES = 128
SUBLANES = 8
TT = 256
EXPERT_BLOCK = 256
BIG_CHUNK = 32
MLA_SLOT = 128
VMEM_LIMIT = 56 * 1024 * 1024


def _cparams(sem):
    return pltpu.CompilerParams(dimension_semantics=sem, vmem_limit_bytes=VMEM_LIMIT)


def _round_up(x, m):
    return (x + m - 1) // m * m


def _norm_mod(x, g, shift, scale):
    y = x * lax.rsqrt(jnp.mean(x * x, axis=-1, keepdims=True) + NORM_EPS) * g
    return y * (1.0 + scale) + shift


def _rms(x):
    return x * lax.rsqrt(jnp.mean(x * x, axis=-1, keepdims=True) + NORM_EPS)


def _mod_kernel(cc_ref, w_ref, b_ref, o_ref):
    cc = cc_ref[...]
    a = cc / (1.0 + jnp.exp(-cc))
    o_ref[...] = jnp.dot(a, w_ref[...], precision=HI, preferred_element_type=F32) + b_ref[...]


def _mods(cc, w_mod, b_mod):
    depth, d, n6 = w_mod.shape
    g = cc.shape[0]
    tn = 1536 if n6 % 1536 == 0 else n6
    return pl.pallas_call(
        _mod_kernel,
        grid=(depth, n6 // tn),
        in_specs=[
            pl.BlockSpec((g, d), lambda i, n: (0, 0)),
            pl.BlockSpec((None, d, tn), lambda i, n: (i, 0, n)),
            pl.BlockSpec((None, 1, tn), lambda i, n: (i, 0, n)),
        ],
        out_specs=pl.BlockSpec((None, g, tn), lambda i, n: (i, 0, n)),
        out_shape=jax.ShapeDtypeStruct((depth, g, n6), F32),
        compiler_params=_cparams(("parallel", "parallel")),
        name="mods",
    )(cc, w_mod, b_mod.reshape(depth, 1, n6))


def _fold_kernel(a_ref, b_ref, o_ref):
    o_ref[...] = jnp.dot(a_ref[...], b_ref[...], precision=HI, preferred_element_type=F32).astype(o_ref.dtype)


def _fold(a, b, out_dtype):
    m, k = a.shape
    n = b.shape[1]
    tn = 512
    return pl.pallas_call(
        _fold_kernel,
        grid=(n // tn,),
        in_specs=[pl.BlockSpec((m, k), lambda j: (0, 0)), pl.BlockSpec((k, tn), lambda j: (0, j))],
        out_specs=pl.BlockSpec((m, tn), lambda j: (0, j)),
        out_shape=jax.ShapeDtypeStruct((m, n), out_dtype),
        compiler_params=_cparams(("parallel",)),
        name="fold",
    )(a, b)


class _Geom:
    def __init__(self, batch, seq, n_ctx):
        self.batch, self.seq, self.n_ctx = batch, seq, n_ctx
        assert seq % TT == 0 and n_ctx % TT == 0 and seq % GRID_W == 0
        self.lat_per_b = seq // TT
        self.ctx_per_b = n_ctx // TT
        self.n_lat_tiles = batch * self.lat_per_b
        self.n_ctx_tiles = batch * self.ctx_per_b
        self.n_tiles = self.n_lat_tiles + self.n_ctx_tiles
        self.n_lat = batch * seq
        self.n_tok = self.n_lat + batch * n_ctx

    def group(self, j):
        return jnp.where(j < self.n_lat_tiles, j // self.lat_per_b, self.batch)

    def pos_tile(self, j):
        return jnp.where(j < self.n_lat_tiles, j % self.lat_per_b, self.lat_per_b)


def _mod_spec(geom, layer, d):
    return pl.BlockSpec((None, None, N_MOD, d), lambda j: (layer, geom.group(j), 0, 0))


def _pre_fourier_kernel(s_ref, mod_ref, g_ref, w_ref, u_ref):
    m = mod_ref[...]
    h = _norm_mod(s_ref[...], g_ref[...], m[0:1], m[1:2])
    u_ref[...] = jnp.dot(h.astype(BF16), w_ref[...], preferred_element_type=F32).astype(BF16)


def _pre_fourier(stream, mods4, g_mix, wcs, geom, layer, n_tiles):
    d = stream.shape[1]
    return pl.pallas_call(
        _pre_fourier_kernel,
        grid=(n_tiles,),
        in_specs=[
            pl.BlockSpec((TT, d), lambda j: (j, 0)),
            _mod_spec(geom, layer, d),
            pl.BlockSpec((1, d), lambda j: (0, 0)),
            pl.BlockSpec((d, 2 * d), lambda j: (0, 0)),
        ],
        out_specs=pl.BlockSpec((TT, 2 * d), lambda j: (j, 0)),
        out_shape=jax.ShapeDtypeStruct((n_tiles * TT, 2 * d), BF16),
        compiler_params=_cparams(("parallel",)),
        name="pre_fourier",
    )(stream, mods4, g_mix, wcs)


def _dft_kernel(c_ref, s_ref, u_ref, o_ref, acc_ref, *, d):
    k = pl.program_id(2)

    @pl.when(k == 0)
    def _():
        acc_ref[...] = jnp.zeros_like(acc_ref)

    acc_ref[...] += (jnp.dot(c_ref[...], u_ref[:, :d], preferred_element_type=F32)
                     - jnp.dot(s_ref[...], u_ref[:, d:], preferred_element_type=F32))

    @pl.when(k == pl.num_programs(2) - 1)
    def _():
        o_ref[...] = acc_ref[...].astype(o_ref.dtype)


def _dft_tables(n):
    k = lax.broadcasted_iota(I32, (n, n), 0)
    m = lax.broadcasted_iota(I32, (n, n), 1)
    ang = ((k * m) % n).astype(F32) * (2.0 * jnp.pi / n)
    scale = n ** -0.5
    return (jnp.cos(ang) * scale), (jnp.sin(ang) * scale)


def _dft(u, row_off, seq, batch, d):
    cos, sin = _dft_tables(seq)
    cos, sin = cos.astype(BF16), sin.astype(BF16)
    tm = min(seq, 1024)
    tk = min(seq, 512)
    assert row_off % tk == 0
    kb = seq // tk
    return pl.pallas_call(
        functools.partial(_dft_kernel, d=d),
        grid=(batch, seq // tm, kb),
        in_specs=[
            pl.BlockSpec((tm, tk), lambda b, m, k: (m, k)),
            pl.BlockSpec((tm, tk), lambda b, m, k: (m, k)),
            pl.BlockSpec((tk, 2 * d), lambda b, m, k: (row_off // tk + b * kb + k, 0)),
        ],
        out_specs=pl.BlockSpec((tm, d), lambda b, m, k: (b * (seq // tm) + m, 0)),
        out_shape=jax.ShapeDtypeStruct((batch * seq, d), BF16),
        scratch_shapes=[pltpu.VMEM((tm, d), F32)],
        compiler_params=_cparams(("parallel", "parallel", "arbitrary")),
        name="dft",
    )(cos, sin, u)


def _pre_mla_kernel(s_ref, mod_ref, g_ref, win_ref, gqa_ref, gkva_ref, wqb_ref, wqbs_ref, wkvk_ref, wkvv_ref,
                    cos_ref, sin_ref, q_ref, k_ref, v_ref, *, scale):
    m = mod_ref[...]
    h = _norm_mod(s_ref[...], g_ref[...], m[0:1], m[1:2])
    a = jnp.dot(h.astype(BF16), win_ref[...], preferred_element_type=F32)
    aq = (_rms(a[:, :MLA_Q_LORA]) * gqa_ref[...]).astype(BF16)
    ckv = (_rms(a[:, MLA_Q_LORA:MLA_Q_LORA + MLA_KV_LORA]) * gkva_ref[...]).astype(BF16)
    o = MLA_Q_LORA + MLA_KV_LORA
    cos = cos_ref[...]
    sin = sin_ref[...]
    kpe = a[:, o:o + MLA_SLOT] * cos + a[:, o + MLA_SLOT:o + 2 * MLA_SLOT] * sin
    cos_h = jnp.tile(cos, (1, MLA_HEADS))
    sin_h = jnp.tile(sin, (1, MLA_HEADS))
    q = jnp.dot(aq, wqb_ref[...], preferred_element_type=F32)
    qs = jnp.dot(aq, wqbs_ref[...], preferred_element_type=F32)
    q_ref[...] = ((q * cos_h + qs * sin_h) * scale).astype(BF16)
    kk = jnp.dot(ckv, wkvk_ref[...], preferred_element_type=F32)
    k_ref[...] = (kk + jnp.tile(kpe, (1, MLA_HEADS))).astype(BF16)
    v_ref[...] = jnp.dot(ckv, wkvv_ref[...], preferred_element_type=F32).astype(BF16)


def _rope_perm(rot):
    sec = rot // 2
    half = sec // 2
    d = jnp.arange(rot)
    first = (d % sec) < half
    partner = jnp.where(first, d + half, d - half)
    sign = jnp.where(first, -1.0, 1.0).astype(F32)
    return partner, sign


def _rope_tables(seq, rot, lat_per_b):
    rows = seq // GRID_W
    row = jnp.repeat(jnp.arange(rows, dtype=F32), GRID_W)
    col = jnp.tile(jnp.arange(GRID_W, dtype=F32), rows)
    n_freq = rot // 4
    inv_freq = ROPE_THETA ** (-jnp.arange(n_freq, dtype=F32) / n_freq)
    ang = jnp.stack([row[:, None] * inv_freq, col[:, None] * inv_freq], axis=1)
    cos = jnp.cos(ang)
    sin = jnp.sin(ang)
    cos_full = jnp.concatenate([cos, cos], axis=-1).reshape(seq, rot)
    sin_full = jnp.concatenate([sin, sin], axis=-1).reshape(seq, rot)
    _, sign = _rope_perm(rot)
    sin_full = sin_full * sign
    ident_c = jnp.ones((TT, rot), F32)
    ident_s = jnp.zeros((TT, rot), F32)
    assert seq == lat_per_b * TT
    return jnp.concatenate([cos_full, ident_c], 0), jnp.concatenate([sin_full, ident_s], 0)


def _pre_mla(stream, mods4, g_mix, w_in, g_qa, w_qb, g_kva, w_kvb, geom, layer):
    d = stream.shape[1]
    hd = MLA_NOPE + MLA_ROPE
    partner, _ = _rope_perm(MLA_ROPE)
    o = MLA_Q_LORA + MLA_KV_LORA
    kpe_w = w_in[:, o:]
    z_lo = jnp.zeros((d, MLA_NOPE), F32)
    z_hi = jnp.zeros((d, MLA_SLOT - hd), F32)
    w_in_ext = jnp.concatenate([w_in[:, :o], z_lo, kpe_w, z_hi, z_lo, kpe_w[:, partner], z_hi], axis=1).astype(BF16)
    wq = w_qb.reshape(MLA_Q_LORA, MLA_HEADS, hd)
    zq = jnp.zeros((MLA_Q_LORA, MLA_HEADS, MLA_SLOT - hd), F32)
    wq_p = jnp.concatenate([wq, zq], axis=2).reshape(MLA_Q_LORA, MLA_HEADS * MLA_SLOT).astype(BF16)
    wq_s = jnp.concatenate([jnp.zeros_like(wq[:, :, :MLA_NOPE]), wq[:, :, MLA_NOPE:][:, :, partner], zq], axis=2)
    wq_s = wq_s.reshape(MLA_Q_LORA, MLA_HEADS * MLA_SLOT).astype(BF16)
    wkv = w_kvb.reshape(MLA_KV_LORA, MLA_HEADS, MLA_NOPE + MLA_V)
    zk = jnp.zeros((MLA_KV_LORA, MLA_HEADS, MLA_SLOT - MLA_NOPE), F32)
    wkv_k = jnp.concatenate([wkv[:, :, :MLA_NOPE], zk], axis=2).reshape(MLA_KV_LORA, MLA_HEADS * MLA_SLOT).astype(BF16)
    wkv_v = wkv[:, :, MLA_NOPE:].reshape(MLA_KV_LORA, MLA_HEADS * MLA_V).astype(BF16)
    cos, sin = _rope_tables(geom.seq, MLA_ROPE, geom.lat_per_b)
    rows = cos.shape[0]
    cos_slot = jnp.concatenate([jnp.ones((rows, MLA_NOPE), F32), cos, jnp.ones((rows, MLA_SLOT - hd), F32)], axis=1)
    sin_slot = jnp.concatenate([jnp.zeros((rows, MLA_NOPE), F32), sin, jnp.zeros((rows, MLA_SLOT - hd), F32)], axis=1)
    n_tiles = geom.n_tiles
    wq_w = MLA_HEADS * MLA_SLOT
    wv_w = MLA_HEADS * MLA_V
    full = lambda a: pl.BlockSpec(a.shape, lambda j: (0,) * a.ndim)
    g_qa2, g_kva2 = g_qa.reshape(1, -1), g_kva.reshape(1, -1)
    return pl.pallas_call(
        functools.partial(_pre_mla_kernel, scale=float(hd) ** -0.5),
        grid=(n_tiles,),
        in_specs=[
            pl.BlockSpec((TT, d), lambda j: (j, 0)),
            _mod_spec(geom, layer, d),
            pl.BlockSpec((1, d), lambda j: (0, 0)),
            full(w_in_ext), full(g_qa2), full(g_kva2), full(wq_p), full(wq_s), full(wkv_k), full(wkv_v),
            pl.BlockSpec((TT, MLA_SLOT), lambda j: (geom.pos_tile(j), 0)),
            pl.BlockSpec((TT, MLA_SLOT), lambda j: (geom.pos_tile(j), 0)),
        ],
        out_specs=[
            pl.BlockSpec((TT, wq_w), lambda j: (j, 0)),
            pl.BlockSpec((TT, wq_w), lambda j: (j, 0)),
            pl.BlockSpec((TT, wv_w), lambda j: (j, 0)),
        ],
        out_shape=[
            jax.ShapeDtypeStruct((n_tiles * TT, wq_w), BF16),
            jax.ShapeDtypeStruct((n_tiles * TT, wq_w), BF16),
            jax.ShapeDtypeStruct((n_tiles * TT, wv_w), BF16),
        ],
        compiler_params=_cparams(("parallel",)),
        name="pre_mla",
    )(stream, mods4, g_mix, w_in_ext, g_qa2, g_kva2, wq_p, wq_s, wkv_k, wkv_v, cos_slot, sin_slot)


def _split_hi_lo(x):
    hi = x.astype(BF16)
    lo = (x - hi.astype(F32)).astype(BF16)
    return hi, lo


def _pre_gqa_kernel(s_ref, mod_ref, g_ref, w_ref, bd_ref, cq_ref, sq_ref, ck_ref, sk_ref, q_ref, k_ref, v_ref, *, scale):
    m = mod_ref[...]
    h = _norm_mod(s_ref[...], g_ref[...], m[0:1], m[1:2])
    a = jnp.dot(h.astype(BF16), w_ref[...], preferred_element_type=F32)
    nq = GQA_HEADS * GQA_HEAD_DIM
    nkv = GQA_KV_HEADS * GQA_HEAD_DIM
    aq, ak, av = a[:, :nq], a[:, nq:nq + nkv], a[:, nq + nkv:nq + 2 * nkv]
    aqs, aks = a[:, nq + 2 * nkv:2 * nq + 2 * nkv], a[:, 2 * nq + 2 * nkv:]
    bd = bd_ref[...]

    def head_rs(x, width):
        hi, lo = _split_hi_lo(x * x)
        b = bd[:width, :width]
        ssq = jnp.dot(hi, b, preferred_element_type=F32) + jnp.dot(lo, b, preferred_element_type=F32)
        return lax.rsqrt(ssq * (1.0 / GQA_HEAD_DIM) + NORM_EPS)

    reps_q = nq // cq_ref.shape[1]
    reps_k = nkv // ck_ref.shape[1]
    q = head_rs(aq, nq) * (aq * jnp.tile(cq_ref[...], (1, reps_q)) + aqs * jnp.tile(sq_ref[...], (1, reps_q)))
    q_ref[...] = (q * scale).astype(BF16)
    k = head_rs(ak, nkv) * (ak * jnp.tile(ck_ref[...], (1, reps_k)) + aks * jnp.tile(sk_ref[...], (1, reps_k)))
    k_ref[...] = k.astype(BF16)
    v_ref[...] = av.astype(BF16)


def _pre_gqa(stream, mods4, g_mix, w_qkv, g_q, g_k, geom, layer):
    d = stream.shape[1]
    nq = GQA_HEADS * GQA_HEAD_DIM
    nkv = GQA_KV_HEADS * GQA_HEAD_DIM
    partner, _ = _rope_perm(GQA_HEAD_DIM)
    wq = w_qkv[:, :nq].reshape(d, GQA_HEADS, GQA_HEAD_DIM)
    wk = w_qkv[:, nq:nq + nkv].reshape(d, GQA_KV_HEADS, GQA_HEAD_DIM)
    w_ext = jnp.concatenate([w_qkv, wq[:, :, partner].reshape(d, nq), wk[:, :, partner].reshape(d, nkv)], axis=1).astype(BF16)
    cos, sin = _rope_tables(geom.seq, GQA_HEAD_DIM, geom.lat_per_b)
    per = LANES // GQA_HEAD_DIM
    cq = jnp.tile(cos * g_q[None, :], (1, per))
    sq = jnp.tile(sin * g_q[partner][None, :], (1, per))
    ck = jnp.tile(cos * g_k[None, :], (1, per))
    sk = jnp.tile(sin * g_k[partner][None, :], (1, per))
    hid = jnp.arange(nq) // GQA_HEAD_DIM
    bd = (hid[:, None] == hid[None, :]).astype(BF16)
    n_tiles = geom.n_tiles
    full = lambda a: pl.BlockSpec(a.shape, lambda j: (0,) * a.ndim)
    tab = pl.BlockSpec((TT, LANES), lambda j: (geom.pos_tile(j), 0))
    return pl.pallas_call(
        functools.partial(_pre_gqa_kernel, scale=float(GQA_HEAD_DIM) ** -0.5),
        grid=(n_tiles,),
        in_specs=[
            pl.BlockSpec((TT, d), lambda j: (j, 0)),
            _mod_spec(geom, layer, d),
            pl.BlockSpec((1, d), lambda j: (0, 0)),
            full(w_ext), full(bd), tab, tab, tab, tab,
        ],
        out_specs=[
            pl.BlockSpec((TT, nq), lambda j: (j, 0)),
            pl.BlockSpec((TT, nkv), lambda j: (j, 0)),
            pl.BlockSpec((TT, nkv), lambda j: (j, 0)),
        ],
        out_shape=[
            jax.ShapeDtypeStruct((n_tiles * TT, nq), BF16),
            jax.ShapeDtypeStruct((n_tiles * TT, nkv), BF16),
            jax.ShapeDtypeStruct((n_tiles * TT, nkv), BF16),
        ],
        compiler_params=_cparams(("parallel",)),
        name="pre_gqa",
    )(stream, mods4, g_mix, w_ext, bd, cq, sq, ck, sk)


def _attn_kernel(*refs, n_heads, kv_heads, dk, dv, has_lat):
    if has_lat:
        q_ref, kl_ref, vl_ref, kc_ref, vc_ref, o_ref = refs
    else:
        q_ref, kc_ref, vc_ref, o_ref = refs
    nt = (((1,), (1,)), ((), ()))
    for g in range(n_heads):
        gk = g * kv_heads // n_heads
        q = q_ref[:, g * dk:(g + 1) * dk]
        kc = kc_ref[:, gk * dk:(gk + 1) * dk]
        vc = vc_ref[:, gk * dv:(gk + 1) * dv]
        sc = lax.dot_general(q, kc, nt, preferred_element_type=F32)
        mx = jnp.max(sc, axis=-1, keepdims=True)
        if has_lat:
            kl = kl_ref[:, gk * dk:(gk + 1) * dk]
            vl = vl_ref[:, gk * dv:(gk + 1) * dv]
            sl = lax.dot_general(q, kl, nt, preferred_element_type=F32)
            mx = jnp.maximum(mx, jnp.max(sl, axis=-1, keepdims=True))
        pc = jnp.exp(sc - mx)
        den = jnp.sum(pc, axis=-1, keepdims=True)
        o = jnp.dot(pc.astype(BF16), vc, preferred_element_type=F32)
        if has_lat:
            pl_ = jnp.exp(sl - mx)
            den = den + jnp.sum(pl_, axis=-1, keepdims=True)
            o = o + jnp.dot(pl_.astype(BF16), vl, preferred_element_type=F32)
        o_ref[:, g * dv:(g + 1) * dv] = (o / den).astype(o_ref.dtype)


def _attention(q, k, v, geom, *, groups, n_heads, kv_heads, dk, dv, latent_queries):
    wq, wk, wv = n_heads * dk, kv_heads * dk, kv_heads * dv
    wo = n_heads * dv
    batch = geom.batch
    nc = geom.n_ctx
    ctx_blk0 = geom.n_lat // nc
    assert geom.n_lat % nc == 0
    kern = functools.partial(_attn_kernel, n_heads=n_heads, kv_heads=kv_heads, dk=dk, dv=dv, has_lat=latent_queries)
    if latent_queries:
        tq = TT
        qt = geom.seq // tq
        return pl.pallas_call(
            kern,
            grid=(batch, groups, qt),
            in_specs=[
                pl.BlockSpec((tq, wq), lambda b, h, t: (b * qt + t, h)),
                pl.BlockSpec((geom.seq, wk), lambda b, h, t: (b, h)),
                pl.BlockSpec((geom.seq, wv), lambda b, h, t: (b, h)),
                pl.BlockSpec((nc, wk), lambda b, h, t: (ctx_blk0 + b, h)),
                pl.BlockSpec((nc, wv), lambda b, h, t: (ctx_blk0 + b, h)),
            ],
            out_specs=pl.BlockSpec((tq, wo), lambda b, h, t: (b * qt + t, h)),
            out_shape=jax.ShapeDtypeStruct((geom.n_lat, groups * wo), BF16),
            compiler_params=_cparams(("parallel", "parallel", "arbitrary")),
            name="attn_lat",
        )(q, k, v, k, v)
    return pl.pallas_call(
        kern,
        grid=(batch, groups),
        in_specs=[
            pl.BlockSpec((nc, wq), lambda b, h: (ctx_blk0 + b, h)),
            pl.BlockSpec((nc, wk), lambda b, h: (ctx_blk0 + b, h)),
            pl.BlockSpec((nc, wv), lambda b, h: (ctx_blk0 + b, h)),
        ],
        out_specs=pl.BlockSpec((nc, wo), lambda b, h: (b, h)),
        out_shape=jax.ShapeDtypeStruct((batch * nc, groups * wo), BF16),
        compiler_params=_cparams(("parallel", "parallel")),
        name="attn_ctx",
    )(q, k, v)


def _post_kernel(o_ref, wo_ref, s_ref, mod_ref, g_ref, wr_ref, br_ref, s_out, h2_out, idx_out, w_out, *, n_experts):
    m = mod_ref[...]
    s = s_ref[...] + m[2:3] * jnp.dot(o_ref[...], wo_ref[...], preferred_element_type=F32)
    s_out[...] = s
    h2 = _norm_mod(s, g_ref[...], m[3:4], m[4:5])
    h2_out[...] = h2.astype(BF16)
    logits = jnp.dot(h2, wr_ref[...], precision=HI, preferred_element_type=F32) + br_ref[...]
    lane = lax.broadcasted_iota(I32, logits.shape, 1).astype(F32)
    work = jnp.where(lane < n_experts, logits, -jnp.inf)
    idx_slab = jnp.zeros(logits.shape, F32)
    vals = []
    for k in range(TOP_K):
        mx = jnp.max(work, axis=-1, keepdims=True)
        sel = jnp.min(jnp.where(work == mx, lane, float(LANES)), axis=-1, keepdims=True)
        idx_slab = jnp.where(lane == k, sel, idx_slab)
        vals.append(mx)
        work = jnp.where(lane == sel, -jnp.inf, work)
    es = [jnp.exp(v - vals[0]) for v in vals]
    den = es[0] + es[1] + es[2] + es[3]
    w_slab = jnp.zeros(logits.shape, F32)
    for k in range(TOP_K):
        w_slab = jnp.where(lane == k, es[k] / den, w_slab)
    idx_out[...] = idx_slab.astype(I32)
    w_out[...] = w_slab


def _post(o, w_o, stream, mods4, g_ffn, w_router, b_router, geom, layer, n_tiles):
    d = stream.shape[1]
    wo_w = o.shape[1]
    n_experts = w_router.shape[1]
    wr = jnp.concatenate([w_router, jnp.zeros((d, LANES - n_experts), F32)], axis=1)
    br = jnp.concatenate([b_router, jnp.zeros((LANES - n_experts,), F32)]).reshape(1, LANES)
    n = n_tiles * TT
    return pl.pallas_call(
        functools.partial(_post_kernel, n_experts=n_experts),
        grid=(n_tiles,),
        in_specs=[
            pl.BlockSpec((TT, wo_w), lambda j: (j, 0)),
            pl.BlockSpec((wo_w, d), lambda j: (0, 0)),
            pl.BlockSpec((TT, d), lambda j: (j, 0)),
            _mod_spec(geom, layer, d),
            pl.BlockSpec((1, d), lambda j: (0, 0)),
            pl.BlockSpec((d, LANES), lambda j: (0, 0)),
            pl.BlockSpec((1, LANES), lambda j: (0, 0)),
        ],
        out_specs=[
            pl.BlockSpec((TT, d), lambda j: (j, 0)),
            pl.BlockSpec((TT, d), lambda j: (j, 0)),
            pl.BlockSpec((TT, LANES), lambda j: (j, 0)),
            pl.BlockSpec((TT, LANES), lambda j: (j, 0)),
        ],
        out_shape=[
            jax.ShapeDtypeStruct((n, d), F32),
            jax.ShapeDtypeStruct((n, d), BF16),
            jax.ShapeDtypeStruct((n, LANES), I32),
            jax.ShapeDtypeStruct((n, LANES), F32),
        ],
        compiler_params=_cparams(("parallel",)),
        name="post",
    )(o, w_o, stream, mods4, g_ffn, wr, br)


def _local_rows(n_experts):
    return _round_up(TT * TOP_K + n_experts * (SUBLANES - 1), LANES)


def _n_blocks(n_tok, n_tiles, n_experts):
    rows = n_tok * TOP_K + n_experts * n_tiles * (SUBLANES - 1) + n_experts * (EXPERT_BLOCK - 1)
    return pl.cdiv(rows, EXPERT_BLOCK)


def _route_meta(top_idx, n_tiles, n_experts):
    n = top_idx.shape[0]
    n_blocks = _n_blocks(n, n_tiles, n_experts)
    oh = (top_idx[:, :, None] == jnp.arange(n_experts, dtype=I32)[None, None, :]).any(axis=1).astype(I32)
    oh_t = oh.reshape(n_tiles, TT, n_experts)
    cnt = oh_t.sum(axis=1)
    rank = jnp.cumsum(oh_t, axis=1) - oh_t
    cp = _round_up(cnt, SUBLANES)
    seg = jnp.cumsum(cp, axis=1) - cp
    run_rows = cp.sum(axis=0)
    reg = _round_up(run_rows, EXPERT_BLOCK)
    reg_end = jnp.cumsum(reg)
    reg_start = reg_end - reg
    off = reg_start[None, :] + jnp.cumsum(cp, axis=0) - cp
    lpos = jnp.take_along_axis((seg[:, None, :] + rank).reshape(n, n_experts), top_idx, axis=1)
    n_used = (reg_end[-1] // EXPERT_BLOCK).astype(I32)
    bstart = jnp.arange(n_blocks, dtype=I32) * EXPERT_BLOCK
    be = jnp.minimum(jnp.searchsorted(reg_end, bstart, side="right"), n_experts - 1).astype(I32)
    be = jnp.where(jnp.arange(n_blocks) < n_used, be, be[jnp.maximum(n_used - 1, 0)])
    n_big = (cp // BIG_CHUNK).sum(axis=1)
    n_small = ((cp % BIG_CHUNK) // SUBLANES).sum(axis=1)
    return dict(
        cp=cp.reshape(-1).astype(I32), seg=seg.reshape(-1).astype(I32), off=off.reshape(-1).astype(I32),
        n_big=n_big.astype(I32), n_small=n_small.astype(I32),
        gap_start=(reg_start + run_rows).astype(I32), gap=(reg - run_rows).astype(I32),
        lpos=lpos.astype(I32), block_expert=be, n_used=n_used.reshape(1), n_blocks=n_blocks,
    )


def _run_copies(tile, n_experts, cp_ref, seg_ref, off_ref, make_big, make_small):
    def per_expert(e, carry):
        cp = cp_ref[tile * n_experts + e]
        seg = seg_ref[tile * n_experts + e]
        off = off_ref[tile * n_experts + e]
        nb = cp // BIG_CHUNK

        def big(j, c):
            make_big(pl.multiple_of(seg + j * BIG_CHUNK, SUBLANES), pl.multiple_of(off + j * BIG_CHUNK, SUBLANES)).start()
            return c

        lax.fori_loop(0, nb, big, 0)
        base = nb * BIG_CHUNK

        def small(j, c):
            make_small(pl.multiple_of(seg + base + j * SUBLANES, SUBLANES),
                       pl.multiple_of(off + base + j * SUBLANES, SUBLANES)).start()
            return c

        lax.fori_loop(0, (cp - base) // SUBLANES, small, 0)
        return carry

    lax.fori_loop(0, n_experts, per_expert, 0)


def _wait_copies(tile, nbig_ref, nsmall_ref, make_big, make_small):
    def wb(j, c):
        make_big(0, 0).wait()
        return c

    def ws(j, c):
        make_small(0, 0).wait()
        return c

    lax.fori_loop(0, nbig_ref[tile], wb, 0)
    lax.fori_loop(0, nsmall_ref[tile], ws, 0)


def _dispatch_kernel(cp_ref, seg_ref, off_ref, nbig_ref, nsmall_ref, gs_ref, gap_ref, nu_ref,
                     lpt_ref, h2_ref, xs_ref, buf_ref, zero_ref, sem, zsem, *, n_experts, local_rows, n_blocks):
    i = pl.program_id(0)
    n = pl.num_programs(0)
    slot = i % 2

    def big(src, dst):
        return pltpu.make_async_copy(buf_ref.at[slot, pl.ds(src, BIG_CHUNK), :], xs_ref.at[pl.ds(dst, BIG_CHUNK), :], sem.at[slot])

    def small(src, dst):
        return pltpu.make_async_copy(buf_ref.at[slot, pl.ds(src, SUBLANES), :], xs_ref.at[pl.ds(dst, SUBLANES), :], sem.at[slot])

    @pl.when(i >= 2)
    def _():
        _wait_copies(i - 2, nbig_ref, nsmall_ref, big, small)

    lpt = lpt_ref[...]
    rows = lax.broadcasted_iota(I32, (local_rows, TT), 0)
    hit = rows == lpt[0:1, :]
    for k in range(1, TOP_K):
        hit = hit | (rows == lpt[k:k + 1, :])
    p = jnp.where(hit, 1.0, 0.0).astype(BF16)
    buf_ref[slot] = jnp.dot(p, h2_ref[...], preferred_element_type=F32)
    _run_copies(i, n_experts, cp_ref, seg_ref, off_ref, big, small)

    @pl.when(i == n - 1)
    def _():
        zero_ref[...] = jnp.zeros_like(zero_ref)
        sizes = [s for s in (128, 64, 32, 16, 8) if s < EXPERT_BLOCK]

        def zcopy(dst, size):
            return pltpu.make_async_copy(zero_ref.at[pl.ds(0, size), :], xs_ref.at[pl.ds(dst, size), :], zsem)

        def per_expert(e, carry):
            gap = gap_ref[e]
            pos = gs_ref[e]
            for size in sizes:
                take = (gap & size) != 0

                @pl.when(take)
                def _():
                    zcopy(pl.multiple_of(pos, SUBLANES), size).start()

                pos = pos + jnp.where(take, size, 0)
            return carry

        lax.fori_loop(0, n_experts, per_expert, 0)

        zrows = zero_ref.shape[0]
        per_block = EXPERT_BLOCK // zrows
        tail_copies = (n_blocks - nu_ref[0]) * per_block

        def tail(t, carry):
            zcopy(pl.multiple_of(nu_ref[0] * EXPERT_BLOCK + t * zrows, SUBLANES), zrows).start()
            return carry

        lax.fori_loop(0, tail_copies, tail, 0)

        def per_expert_wait(e, carry):
            gap = gap_ref[e]
            for size in sizes:
                @pl.when((gap & size) != 0)
                def _():
                    zcopy(0, size).wait()
            return carry

        lax.fori_loop(0, n_experts, per_expert_wait, 0)

        def tail_wait(t, carry):
            zcopy(0, zrows).wait()
            return carry

        lax.fori_loop(0, tail_copies, tail_wait, 0)

        @pl.when(i >= 1)
        def _():
            def big_o(src, dst):
                return pltpu.make_async_copy(buf_ref.at[1 - slot, pl.ds(src, BIG_CHUNK), :], xs_ref.at[pl.ds(dst, BIG_CHUNK), :], sem.at[1 - slot])

            def small_o(src, dst):
                return pltpu.make_async_copy(buf_ref.at[1 - slot, pl.ds(src, SUBLANES), :], xs_ref.at[pl.ds(dst, SUBLANES), :], sem.at[1 - slot])

            _wait_copies(i - 1, nbig_ref, nsmall_ref, big_o, small_o)

        _wait_copies(i, nbig_ref, nsmall_ref, big, small)


def _dispatch(h2, meta, n_tiles, n_experts):
    n, d = h2.shape
    local_rows = _local_rows(n_experts)
    n_rows = meta["n_blocks"] * EXPERT_BLOCK
    lpt = meta["lpos"].reshape(n_tiles, TT, TOP_K).transpose(0, 2, 1)
    grid_spec = pltpu.PrefetchScalarGridSpec(
        num_scalar_prefetch=8,
        grid=(n_tiles,),
        in_specs=[
            pl.BlockSpec((None, TOP_K, TT), lambda j, *_: (j, 0, 0)),
            pl.BlockSpec((TT, d), lambda j, *_: (j, 0)),
        ],
        out_specs=pl.BlockSpec(memory_space=pl.ANY),
        scratch_shapes=[
            pltpu.VMEM((2, local_rows, d), F32),
            pltpu.VMEM((128, d), F32),
            pltpu.SemaphoreType.DMA((2,)),
            pltpu.SemaphoreType.DMA(()),
        ],
    )
    return pl.pallas_call(
        functools.partial(_dispatch_kernel, n_experts=n_experts, local_rows=local_rows, n_blocks=meta["n_blocks"]),
        grid_spec=grid_spec,
        out_shape=jax.ShapeDtypeStruct((n_rows, d), F32),
        compiler_params=_cparams(("arbitrary",)),
        name="moe_dispatch",
    )(meta["cp"], meta["seg"], meta["off"], meta["n_big"], meta["n_small"], meta["gap_start"], meta["gap"], meta["n_used"],
      lpt, h2)


def _ffn_kernel(be_ref, nu_ref, x_ref, wg_ref, wu_ref, wd_ref, bg_ref, bu_ref, bd_ref, y_ref):
    b = pl.program_id(0)

    @pl.when(b < nu_ref[0])
    def _():
        x = x_ref[...].astype(BF16)
        gate = jnp.dot(x, wg_ref[...], preferred_element_type=F32) + bg_ref[...]
        up = jnp.dot(x, wu_ref[...], preferred_element_type=F32) + bu_ref[...]
        gate = jnp.minimum(gate, SWIGLU_LIMIT)
        up = jnp.clip(up, -SWIGLU_LIMIT, SWIGLU_LIMIT)
        glu = gate / (1.0 + jnp.exp(-SWIGLU_ALPHA * gate))
        act = ((up + 1.0) * glu).astype(BF16)
        y_ref[...] = jnp.dot(act, wd_ref[...], preferred_element_type=F32) + bd_ref[...]

    @pl.when(b >= nu_ref[0])
    def _():
        y_ref[...] = jnp.zeros_like(y_ref)


def _ffn(xs, meta, w_gate, w_up, w_down, b_gate, b_up, b_down):
    n_rows, d = xs.shape
    n_blocks = meta["n_blocks"]
    f = w_gate.shape[2]
    xmap = lambda b, be, nu: (jnp.minimum(b, jnp.maximum(nu[0] - 1, 0)), 0)
    wmap = lambda b, be, nu: (be[b], 0, 0)
    grid_spec = pltpu.PrefetchScalarGridSpec(
        num_scalar_prefetch=2,
        grid=(n_blocks,),
        in_specs=[
            pl.BlockSpec((EXPERT_BLOCK, d), xmap),
            pl.BlockSpec((None, d, f), wmap),
            pl.BlockSpec((None, d, f), wmap),
            pl.BlockSpec((None, f, d), wmap),
            pl.BlockSpec((None, 1, f), wmap),
            pl.BlockSpec((None, 1, f), wmap),
            pl.BlockSpec((None, 1, d), wmap),
        ],
        out_specs=pl.BlockSpec((EXPERT_BLOCK, d), lambda b, be, nu: (b, 0)),
    )
    return pl.pallas_call(
        _ffn_kernel,
        grid_spec=grid_spec,
        out_shape=jax.ShapeDtypeStruct((n_rows, d), F32),
        compiler_params=_cparams(("arbitrary",)),
        name="moe_ffn",
    )(meta["block_expert"], meta["n_used"], xs, w_gate, w_up, w_down, b_gate, b_up, b_down)


def _combine_kernel(cp_ref, seg_ref, off_ref, nbig_ref, nsmall_ref,
                    lp_ref, w_ref, s_ref, mod_ref, gf_ref, ys_ref, o_ref, buf_ref, sem, *, n_experts, local_rows, final):
    i = pl.program_id(0)
    n = pl.num_programs(0)
    slot = i % 2

    def copies(sl):
        def big(dst, src):
            return pltpu.make_async_copy(ys_ref.at[pl.ds(src, BIG_CHUNK), :], buf_ref.at[sl, pl.ds(dst, BIG_CHUNK), :], sem.at[sl])

        def small(dst, src):
            return pltpu.make_async_copy(ys_ref.at[pl.ds(src, SUBLANES), :], buf_ref.at[sl, pl.ds(dst, SUBLANES), :], sem.at[sl])

        return big, small

    @pl.when(i == 0)
    def _():
        buf_ref[...] = jnp.zeros_like(buf_ref)
        _run_copies(0, n_experts, cp_ref, seg_ref, off_ref, *copies(0))

    @pl.when(i + 1 < n)
    def _():
        _run_copies(i + 1, n_experts, cp_ref, seg_ref, off_ref, *copies(1 - slot))

    _wait_copies(i, nbig_ref, nsmall_ref, *copies(slot))

    lp = lp_ref[...]
    w = w_ref[...]
    lanes = lax.broadcasted_iota(I32, (TT, local_rows), 1)
    pw = jnp.zeros((TT, local_rows), F32)
    for k in range(TOP_K):
        pw = pw + jnp.where(lanes == lp[:, k:k + 1], w[:, k:k + 1], 0.0)
    y = buf_ref[slot].astype(BF16)
    f = jnp.dot(pw.astype(BF16), y, preferred_element_type=F32)
    m = mod_ref[...]
    s = s_ref[...] + m[5:6] * f
    if final:
        s = _rms(s) * gf_ref[...]
    o_ref[...] = s


def _combine(ys, meta, top_w, stream, mods4, g_final, geom, layer, n_tiles, n_experts, final):
    d = stream.shape[1]
    local_rows = _local_rows(n_experts)
    grid_spec = pltpu.PrefetchScalarGridSpec(
        num_scalar_prefetch=5,
        grid=(n_tiles,),
        in_specs=[
            pl.BlockSpec((TT, TOP_K), lambda j, *_: (j, 0)),
            pl.BlockSpec((TT, TOP_K), lambda j, *_: (j, 0)),
            pl.BlockSpec((TT, d), lambda j, *_: (j, 0)),
            pl.BlockSpec((None, None, N_MOD, d), lambda j, *_: (layer, geom.group(j), 0, 0)),
            pl.BlockSpec((1, d), lambda j, *_: (0, 0)),
            pl.BlockSpec(memory_space=pl.ANY),
        ],
        out_specs=pl.BlockSpec((TT, d), lambda j, *_: (j, 0)),
        scratch_shapes=[
            pltpu.VMEM((2, local_rows, d), F32),
            pltpu.SemaphoreType.DMA((2,)),
        ],
    )
    return pl.pallas_call(
        functools.partial(_combine_kernel, n_experts=n_experts, local_rows=local_rows, final=final),
        grid_spec=grid_spec,
        out_shape=jax.ShapeDtypeStruct((n_tiles * TT, d), F32),
        compiler_params=_cparams(("arbitrary",)),
        name="moe_combine",
    )(meta["cp"], meta["seg"], meta["off"], meta["n_big"], meta["n_small"],
      meta["lpos"], top_w, stream, mods4, g_final, ys)


def kernel(x, c, ctx, c_ctx, w_mod, b_mod, g_mix, g_ffn, g_final, f_w_in, f_w_out, mla_w_in, mla_g_qa, mla_w_qb,
           mla_g_kva, mla_w_kvb, mla_w_o, gqa_w_qkv, gqa_g_q, gqa_g_k, gqa_w_o, moe_w_router, moe_b_router,
           moe_w_gu, moe_b_gu, moe_w_down, moe_b_down):
    batch, seq, d = x.shape
    n_ctx = ctx.shape[1]
    depth = w_mod.shape[0]
    n_experts = moe_w_router.shape[2]
    geom = _Geom(batch, seq, n_ctx)

    n_groups = _round_up(batch + 1, SUBLANES)
    cc = jnp.concatenate([c, c_ctx[None, :], jnp.zeros((n_groups - batch - 1, d), F32)], axis=0)
    mods4 = _mods(cc, w_mod, b_mod).reshape(depth, n_groups, N_MOD, d)

    stream = jnp.concatenate([x.reshape(batch * seq, d), ctx.reshape(batch * n_ctx, d)], axis=0)

    gd = d // FOURIER_GROUPS
    gcos, gsin = _dft_tables(gd)
    eye = jnp.eye(FOURIER_GROUPS, dtype=F32)
    bd_cs = jnp.concatenate([jnp.kron(eye, gcos), jnp.kron(eye, gsin)], axis=1)

    for i in range(depth):
        kind, j = i % N_MIXERS, i // N_MIXERS
        last = i == depth - 1
        ctx_used = (kind != 0) or (not last)
        n_tiles = geom.n_tiles if ctx_used else geom.n_lat_tiles
        gm = g_mix[i].reshape(1, d)
        gf = g_ffn[i].reshape(1, d)

        if kind == 0:
            wcs = _fold(f_w_in[j], bd_cs, BF16)
            u = _pre_fourier(stream, mods4, gm, wcs, geom, i, n_tiles)
            o = _dft(u, 0, seq, batch, d)
            if ctx_used:
                o = jnp.concatenate([o, _dft(u, geom.n_lat, n_ctx, batch, d)], axis=0)
            w_o = f_w_out[j].astype(BF16)
        elif kind == 1:
            q, k, v = _pre_mla(stream, mods4, gm, mla_w_in[j], mla_g_qa[j], mla_w_qb[j], mla_g_kva[j], mla_w_kvb[j], geom, i)
            att = functools.partial(_attention, q, k, v, geom, groups=4, n_heads=MLA_HEADS // 4, kv_heads=MLA_HEADS // 4,
                                    dk=MLA_SLOT, dv=MLA_V)
            o = att(latent_queries=True)
            if not last:
                o = jnp.concatenate([o, att(latent_queries=False)], axis=0)
            w_o = mla_w_o[j].astype(BF16)
        else:
            q, k, v = _pre_gqa(stream, mods4, gm, gqa_w_qkv[j], gqa_g_q[j], gqa_g_k[j], geom, i)
            att = functools.partial(_attention, q, k, v, geom, groups=1, n_heads=GQA_HEADS, kv_heads=GQA_KV_HEADS,
                                    dk=GQA_HEAD_DIM, dv=GQA_HEAD_DIM)
            o = att(latent_queries=True)
            if not last:
                o = jnp.concatenate([o, att(latent_queries=False)], axis=0)
            w_o = gqa_w_o[j].astype(BF16)

        n_moe_tiles = geom.n_lat_tiles if last else geom.n_tiles
        stream, h2, idx_slab, w_slab = _post(o, w_o, stream, mods4, gf, moe_w_router[i], moe_b_router[i], geom, i, n_moe_tiles)
        top_idx = idx_slab[:, :TOP_K]
        top_w = w_slab[:, :TOP_K]
        meta = _route_meta(top_idx, n_moe_tiles, n_experts)
        xs = _dispatch(h2, meta, n_moe_tiles, n_experts)
        f = moe_w_down.shape[2]
        w_gate = moe_w_gu[i][:, :, 0::2].astype(BF16)
        w_up = moe_w_gu[i][:, :, 1::2].astype(BF16)
        b_gate = moe_b_gu[i][:, 0::2].reshape(n_experts, 1, f)
        b_up = moe_b_gu[i][:, 1::2].reshape(n_experts, 1, f)
        ys = _ffn(xs, meta, w_gate, w_up, moe_w_down[i].astype(BF16), b_gate, b_up, moe_b_down[i].reshape(n_experts, 1, d))
        stream = _combine(ys, meta, top_w, stream, mods4, g_final.reshape(1, d), geom, i, n_moe_tiles, n_experts, last)

    return stream[:batch * seq].reshape(batch, seq, d)
```

```python
import functools

import jax
import jax.numpy as jnp
from jax import lax
from jax.experimental import pallas as pl
from jax.experimental.pallas import tpu as pltpu

F32 = jnp.float32
BF16 = jnp.bfloat16
I32 = jnp.int32
HI = lax.Precision.HIGHEST

GRID_W = 64
N_MIXERS = 3
NORM_EPS = 1e-6
ROPE_THETA = 10000.0
FOURIER_GROUPS = 8
MLA_HEADS = 16
MLA_Q_LORA = 384
MLA_KV_LORA = 256
MLA_NOPE = 64
MLA_ROPE = 32
MLA_V = 64
GQA_HEADS = 16
GQA_KV_HEADS = 4
GQA_HEAD_DIM = 64
TOP_K = 4
SWIGLU_LIMIT = 7.0
SWIGLU_ALPHA = 1.702
N_MOD = 6

LANES = 128
SUBLANES = 8
TT = 256
EXPERT_BLOCK = 256
BIG_CHUNK = 32
MLA_SLOT = 128
VMEM_LIMIT = 56 * 1024 * 1024


def _cparams(sem):
    return pltpu.CompilerParams(dimension_semantics=sem, vmem_limit_bytes=VMEM_LIMIT)


def _round_up(x, m):
    return (x + m - 1) // m * m


def _norm_mod(x, g, shift, scale):
    y = x * lax.rsqrt(jnp.mean(x * x, axis=-1, keepdims=True) + NORM_EPS) * g
    return y * (1.0 + scale) + shift


def _rms(x):
    return x * lax.rsqrt(jnp.mean(x * x, axis=-1, keepdims=True) + NORM_EPS)


def _mod_kernel(cc_ref, w_ref, b_ref, o_ref):
    cc = cc_ref[...]
    a = cc / (1.0 + jnp.exp(-cc))
    o_ref[...] = jnp.dot(a, w_ref[...], precision=HI, preferred_element_type=F32) + b_ref[...]


def _mods(cc, w_mod, b_mod):
    depth, d, n6 = w_mod.shape
    g = cc.shape[0]
    tn = 1536 if n6 % 1536 == 0 else n6
    return pl.pallas_call(
        _mod_kernel,
        grid=(depth, n6 // tn),
        in_specs=[
            pl.BlockSpec((g, d), lambda i, n: (0, 0)),
            pl.BlockSpec((None, d, tn), lambda i, n: (i, 0, n)),
            pl.BlockSpec((None, 1, tn), lambda i, n: (i, 0, n)),
        ],
        out_specs=pl.BlockSpec((None, g, tn), lambda i, n: (i, 0, n)),
        out_shape=jax.ShapeDtypeStruct((depth, g, n6), F32),
        compiler_params=_cparams(("parallel", "parallel")),
        name="mods",
    )(cc, w_mod, b_mod.reshape(depth, 1, n6))


def _fold_kernel(a_ref, b_ref, o_ref):
    o_ref[...] = jnp.dot(a_ref[...], b_ref[...], precision=HI, preferred_element_type=F32).astype(o_ref.dtype)


def _fold(a, b, out_dtype):
    m, k = a.shape
    n = b.shape[1]
    tn = 512
    return pl.pallas_call(
        _fold_kernel,
        grid=(n // tn,),
        in_specs=[pl.BlockSpec((m, k), lambda j: (0, 0)), pl.BlockSpec((k, tn), lambda j: (0, j))],
        out_specs=pl.BlockSpec((m, tn), lambda j: (0, j)),
        out_shape=jax.ShapeDtypeStruct((m, n), out_dtype),
        compiler_params=_cparams(("parallel",)),
        name="fold",
    )(a, b)


class _Geom:
    def __init__(self, batch, seq, n_ctx):
        self.batch, self.seq, self.n_ctx = batch, seq, n_ctx
        assert seq % TT == 0 and n_ctx % TT == 0 and seq % GRID_W == 0
        self.lat_per_b = seq // TT
        self.ctx_per_b = n_ctx // TT
        self.n_lat_tiles = batch * self.lat_per_b
        self.n_ctx_tiles = batch * self.ctx_per_b
        self.n_tiles = self.n_lat_tiles + self.n_ctx_tiles
        self.n_lat = batch * seq
        self.n_tok = self.n_lat + batch * n_ctx

    def group(self, j):
        return jnp.where(j < self.n_lat_tiles, j // self.lat_per_b, self.batch)

    def pos_tile(self, j):
        return jnp.where(j < self.n_lat_tiles, j % self.lat_per_b, self.lat_per_b)


def _mod_spec(geom, layer, d):
    return pl.BlockSpec((None, None, N_MOD, d), lambda j: (layer, geom.group(j), 0, 0))


def _pre_fourier_kernel(s_ref, mod_ref, g_ref, w_ref, u_ref):
    m = mod_ref[...]
    h = _norm_mod(s_ref[...], g_ref[...], m[0:1], m[1:2])
    u_ref[...] = jnp.dot(h.astype(BF16), w_ref[...], preferred_element_type=F32).astype(BF16)


def _pre_fourier(stream, mods4, g_mix, wcs, geom, layer, n_tiles):
    d = stream.shape[1]
    return pl.pallas_call(
        _pre_fourier_kernel,
        grid=(n_tiles,),
        in_specs=[
            pl.BlockSpec((TT, d), lambda j: (j, 0)),
            _mod_spec(geom, layer, d),
            pl.BlockSpec((1, d), lambda j: (0, 0)),
            pl.BlockSpec((d, 2 * d), lambda j: (0, 0)),
        ],
        out_specs=pl.BlockSpec((TT, 2 * d), lambda j: (j, 0)),
        out_shape=jax.ShapeDtypeStruct((n_tiles * TT, 2 * d), BF16),
        compiler_params=_cparams(("parallel",)),
        name="pre_fourier",
    )(stream, mods4, g_mix, wcs)


def _dft_kernel(c_ref, s_ref, u_ref, o_ref, acc_ref, *, d):
    k = pl.program_id(2)

    @pl.when(k == 0)
    def _():
        acc_ref[...] = jnp.zeros_like(acc_ref)

    acc_ref[...] += (jnp.dot(c_ref[...], u_ref[:, :d], preferred_element_type=F32)
                     - jnp.dot(s_ref[...], u_ref[:, d:], preferred_element_type=F32))

    @pl.when(k == pl.num_programs(2) - 1)
    def _():
        o_ref[...] = acc_ref[...].astype(o_ref.dtype)


def _dft_tables(n):
    k = lax.broadcasted_iota(I32, (n, n), 0)
    m = lax.broadcasted_iota(I32, (n, n), 1)
    ang = ((k * m) % n).astype(F32) * (2.0 * jnp.pi / n)
    scale = n ** -0.5
    return (jnp.cos(ang) * scale), (jnp.sin(ang) * scale)


def _dft(u, row_off, seq, batch, d):
    cos, sin = _dft_tables(seq)
    cos, sin = cos.astype(BF16), sin.astype(BF16)
    tm = min(seq, 1024)
    tk = min(seq, 512)
    assert row_off % tk == 0
    kb = seq // tk
    return pl.pallas_call(
        functools.partial(_dft_kernel, d=d),
        grid=(batch, seq // tm, kb),
        in_specs=[
            pl.BlockSpec((tm, tk), lambda b, m, k: (m, k)),
            pl.BlockSpec((tm, tk), lambda b, m, k: (m, k)),
            pl.BlockSpec((tk, 2 * d), lambda b, m, k: (row_off // tk + b * kb + k, 0)),
        ],
        out_specs=pl.BlockSpec((tm, d), lambda b, m, k: (b * (seq // tm) + m, 0)),
        out_shape=jax.ShapeDtypeStruct((batch * seq, d), BF16),
        scratch_shapes=[pltpu.VMEM((tm, d), F32)],
        compiler_params=_cparams(("parallel", "parallel", "arbitrary")),
        name="dft",
    )(cos, sin, u)


def _pre_mla_kernel(s_ref, mod_ref, g_ref, win_ref, gqa_ref, gkva_ref, wqb_ref, wqbs_ref, wkvk_ref, wkvv_ref,
                    cos_ref, sin_ref, q_ref, k_ref, v_ref, *, scale):
    m = mod_ref[...]
    h = _norm_mod(s_ref[...], g_ref[...], m[0:1], m[1:2])
    a = jnp.dot(h.astype(BF16), win_ref[...], preferred_element_type=F32)
    aq = (_rms(a[:, :MLA_Q_LORA]) * gqa_ref[...]).astype(BF16)
    ckv = (_rms(a[:, MLA_Q_LORA:MLA_Q_LORA + MLA_KV_LORA]) * gkva_ref[...]).astype(BF16)
    o = MLA_Q_LORA + MLA_KV_LORA
    cos = cos_ref[...]
    sin = sin_ref[...]
    kpe = a[:, o:o + MLA_SLOT] * cos + a[:, o + MLA_SLOT:o + 2 * MLA_SLOT] * sin
    cos_h = jnp.tile(cos, (1, MLA_HEADS))
    sin_h = jnp.tile(sin, (1, MLA_HEADS))
    q = jnp.dot(aq, wqb_ref[...], preferred_element_type=F32)
    qs = jnp.dot(aq, wqbs_ref[...], preferred_element_type=F32)
    q_ref[...] = ((q * cos_h + qs * sin_h) * scale).astype(BF16)
    kk = jnp.dot(ckv, wkvk_ref[...], preferred_element_type=F32)
    k_ref[...] = (kk + jnp.tile(kpe, (1, MLA_HEADS))).astype(BF16)
    v_ref[...] = jnp.dot(ckv, wkvv_ref[...], preferred_element_type=F32).astype(BF16)


def _rope_perm(rot):
    sec = rot // 2
    half = sec // 2
    d = jnp.arange(rot)
    first = (d % sec) < half
    partner = jnp.where(first, d + half, d - half)
    sign = jnp.where(first, -1.0, 1.0).astype(F32)
    return partner, sign


def _rope_tables(seq, rot, lat_per_b):
    rows = seq // GRID_W
    row = jnp.repeat(jnp.arange(rows, dtype=F32), GRID_W)
    col = jnp.tile(jnp.arange(GRID_W, dtype=F32), rows)
    n_freq = rot // 4
    inv_freq = ROPE_THETA ** (-jnp.arange(n_freq, dtype=F32) / n_freq)
    ang = jnp.stack([row[:, None] * inv_freq, col[:, None] * inv_freq], axis=1)
    cos = jnp.cos(ang)
    sin = jnp.sin(ang)
    cos_full = jnp.concatenate([cos, cos], axis=-1).reshape(seq, rot)
    sin_full = jnp.concatenate([sin, sin], axis=-1).reshape(seq, rot)
    _, sign = _rope_perm(rot)
    sin_full = sin_full * sign
    ident_c = jnp.ones((TT, rot), F32)
    ident_s = jnp.zeros((TT, rot), F32)
    assert seq == lat_per_b * TT
    return jnp.concatenate([cos_full, ident_c], 0), jnp.concatenate([sin_full, ident_s], 0)


def _pre_mla(stream, mods4, g_mix, w_in, g_qa, w_qb, g_kva, w_kvb, geom, layer):
    d = stream.shape[1]
    hd = MLA_NOPE + MLA_ROPE
    partner, _ = _rope_perm(MLA_ROPE)
    o = MLA_Q_LORA + MLA_KV_LORA
    kpe_w = w_in[:, o:]
    z_lo = jnp.zeros((d, MLA_NOPE), F32)
    z_hi = jnp.zeros((d, MLA_SLOT - hd), F32)
    w_in_ext = jnp.concatenate([w_in[:, :o], z_lo, kpe_w, z_hi, z_lo, kpe_w[:, partner], z_hi], axis=1).astype(BF16)
    wq = w_qb.reshape(MLA_Q_LORA, MLA_HEADS, hd)
    zq = jnp.zeros((MLA_Q_LORA, MLA_HEADS, MLA_SLOT - hd), F32)
    wq_p = jnp.concatenate([wq, zq], axis=2).reshape(MLA_Q_LORA, MLA_HEADS * MLA_SLOT).astype(BF16)
    wq_s = jnp.concatenate([jnp.zeros_like(wq[:, :, :MLA_NOPE]), wq[:, :, MLA_NOPE:][:, :, partner], zq], axis=2)
    wq_s = wq_s.reshape(MLA_Q_LORA, MLA_HEADS * MLA_SLOT).astype(BF16)
    wkv = w_kvb.reshape(MLA_KV_LORA, MLA_HEADS, MLA_NOPE + MLA_V)
    zk = jnp.zeros((MLA_KV_LORA, MLA_HEADS, MLA_SLOT - MLA_NOPE), F32)
    wkv_k = jnp.concatenate([wkv[:, :, :MLA_NOPE], zk], axis=2).reshape(MLA_KV_LORA, MLA_HEADS * MLA_SLOT).astype(BF16)
    wkv_v = wkv[:, :, MLA_NOPE:].reshape(MLA_KV_LORA, MLA_HEADS * MLA_V).astype(BF16)
    cos, sin = _rope_tables(geom.seq, MLA_ROPE, geom.lat_per_b)
    rows = cos.shape[0]
    cos_slot = jnp.concatenate([jnp.ones((rows, MLA_NOPE), F32), cos, jnp.ones((rows, MLA_SLOT - hd), F32)], axis=1)
    sin_slot = jnp.concatenate([jnp.zeros((rows, MLA_NOPE), F32), sin, jnp.zeros((rows, MLA_SLOT - hd), F32)], axis=1)
    n_tiles = geom.n_tiles
    wq_w = MLA_HEADS * MLA_SLOT
    wv_w = MLA_HEADS * MLA_V
    full = lambda a: pl.BlockSpec(a.shape, lambda j: (0,) * a.ndim)
    g_qa2, g_kva2 = g_qa.reshape(1, -1), g_kva.reshape(1, -1)
    return pl.pallas_call(
        functools.partial(_pre_mla_kernel, scale=float(hd) ** -0.5),
        grid=(n_tiles,),
        in_specs=[
            pl.BlockSpec((TT, d), lambda j: (j, 0)),
            _mod_spec(geom, layer, d),
            pl.BlockSpec((1, d), lambda j: (0, 0)),
            full(w_in_ext), full(g_qa2), full(g_kva2), full(wq_p), full(wq_s), full(wkv_k), full(wkv_v),
            pl.BlockSpec((TT, MLA_SLOT), lambda j: (geom.pos_tile(j), 0)),
            pl.BlockSpec((TT, MLA_SLOT), lambda j: (geom.pos_tile(j), 0)),
        ],
        out_specs=[
            pl.BlockSpec((TT, wq_w), lambda j: (j, 0)),
            pl.BlockSpec((TT, wq_w), lambda j: (j, 0)),
            pl.BlockSpec((TT, wv_w), lambda j: (j, 0)),
        ],
        out_shape=[
            jax.ShapeDtypeStruct((n_tiles * TT, wq_w), BF16),
            jax.ShapeDtypeStruct((n_tiles * TT, wq_w), BF16),
            jax.ShapeDtypeStruct((n_tiles * TT, wv_w), BF16),
        ],
        compiler_params=_cparams(("parallel",)),
        name="pre_mla",
    )(stream, mods4, g_mix, w_in_ext, g_qa2, g_kva2, wq_p, wq_s, wkv_k, wkv_v, cos_slot, sin_slot)


def _split_hi_lo(x):
    hi = x.astype(BF16)
    lo = (x - hi.astype(F32)).astype(BF16)
    return hi, lo


def _pre_gqa_kernel(s_ref, mod_ref, g_ref, w_ref, bd_ref, cq_ref, sq_ref, ck_ref, sk_ref, q_ref, k_ref, v_ref, *, scale):
    m = mod_ref[...]
    h = _norm_mod(s_ref[...], g_ref[...], m[0:1], m[1:2])
    a = jnp.dot(h.astype(BF16), w_ref[...], preferred_element_type=F32)
    nq = GQA_HEADS * GQA_HEAD_DIM
    nkv = GQA_KV_HEADS * GQA_HEAD_DIM
    aq, ak, av = a[:, :nq], a[:, nq:nq + nkv], a[:, nq + nkv:nq + 2 * nkv]
    aqs, aks = a[:, nq + 2 * nkv:2 * nq + 2 * nkv], a[:, 2 * nq + 2 * nkv:]
    bd = bd_ref[...]

    def head_rs(x, width):
        hi, lo = _split_hi_lo(x * x)
        b = bd[:width, :width]
        ssq = jnp.dot(hi, b, preferred_element_type=F32) + jnp.dot(lo, b, preferred_element_type=F32)
        return lax.rsqrt(ssq * (1.0 / GQA_HEAD_DIM) + NORM_EPS)

    reps_q = nq // cq_ref.shape[1]
    reps_k = nkv // ck_ref.shape[1]
    q = head_rs(aq, nq) * (aq * jnp.tile(cq_ref[...], (1, reps_q)) + aqs * jnp.tile(sq_ref[...], (1, reps_q)))
    q_ref[...] = (q * scale).astype(BF16)
    k = head_rs(ak, nkv) * (ak * jnp.tile(ck_ref[...], (1, reps_k)) + aks * jnp.tile(sk_ref[...], (1, reps_k)))
    k_ref[...] = k.astype(BF16)
    v_ref[...] = av.astype(BF16)


def _pre_gqa(stream, mods4, g_mix, w_qkv, g_q, g_k, geom, layer):
    d = stream.shape[1]
    nq = GQA_HEADS * GQA_HEAD_DIM
    nkv = GQA_KV_HEADS * GQA_HEAD_DIM
    partner, _ = _rope_perm(GQA_HEAD_DIM)
    wq = w_qkv[:, :nq].reshape(d, GQA_HEADS, GQA_HEAD_DIM)
    wk = w_qkv[:, nq:nq + nkv].reshape(d, GQA_KV_HEADS, GQA_HEAD_DIM)
    w_ext = jnp.concatenate([w_qkv, wq[:, :, partner].reshape(d, nq), wk[:, :, partner].reshape(d, nkv)], axis=1).astype(BF16)
    cos, sin = _rope_tables(geom.seq, GQA_HEAD_DIM, geom.lat_per_b)
    per = LANES // GQA_HEAD_DIM
    cq = jnp.tile(cos * g_q[None, :], (1, per))
    sq = jnp.tile(sin * g_q[partner][None, :], (1, per))
    ck = jnp.tile(cos * g_k[None, :], (1, per))
    sk = jnp.tile(sin * g_k[partner][None, :], (1, per))
    hid = jnp.arange(nq) // GQA_HEAD_DIM
    bd = (hid[:, None] == hid[None, :]).astype(BF16)
    n_tiles = geom.n_tiles
    full = lambda a: pl.BlockSpec(a.shape, lambda j: (0,) * a.ndim)
    tab = pl.BlockSpec((TT, LANES), lambda j: (geom.pos_tile(j), 0))
    return pl.pallas_call(
        functools.partial(_pre_gqa_kernel, scale=float(GQA_HEAD_DIM) ** -0.5),
        grid=(n_tiles,),
        in_specs=[
            pl.BlockSpec((TT, d), lambda j: (j, 0)),
            _mod_spec(geom, layer, d),
            pl.BlockSpec((1, d), lambda j: (0, 0)),
            full(w_ext), full(bd), tab, tab, tab, tab,
        ],
        out_specs=[
            pl.BlockSpec((TT, nq), lambda j: (j, 0)),
            pl.BlockSpec((TT, nkv), lambda j: (j, 0)),
            pl.BlockSpec((TT, nkv), lambda j: (j, 0)),
        ],
        out_shape=[
            jax.ShapeDtypeStruct((n_tiles * TT, nq), BF16),
            jax.ShapeDtypeStruct((n_tiles * TT, nkv), BF16),
            jax.ShapeDtypeStruct((n_tiles * TT, nkv), BF16),
        ],
        compiler_params=_cparams(("parallel",)),
        name="pre_gqa",
    )(stream, mods4, g_mix, w_ext, bd, cq, sq, ck, sk)


def _attn_kernel(*refs, n_heads, kv_heads, dk, dv, has_lat):
    if has_lat:
        q_ref, kl_ref, vl_ref, kc_ref, vc_ref, o_ref = refs
    else:
        q_ref, kc_ref, vc_ref, o_ref = refs
    nt = (((1,), (1,)), ((), ()))
    for g in range(n_heads):
        gk = g * kv_heads // n_heads
        q = q_ref[:, g * dk:(g + 1) * dk]
        kc = kc_ref[:, gk * dk:(gk + 1) * dk]
        vc = vc_ref[:, gk * dv:(gk + 1) * dv]
        sc = lax.dot_general(q, kc, nt, preferred_element_type=F32)
        mx = jnp.max(sc, axis=-1, keepdims=True)
        if has_lat:
            kl = kl_ref[:, gk * dk:(gk + 1) * dk]
            vl = vl_ref[:, gk * dv:(gk + 1) * dv]
            sl = lax.dot_general(q, kl, nt, preferred_element_type=F32)
            mx = jnp.maximum(mx, jnp.max(sl, axis=-1, keepdims=True))
        pc = jnp.exp(sc - mx)
        den = jnp.sum(pc, axis=-1, keepdims=True)
        o = jnp.dot(pc.astype(BF16), vc, preferred_element_type=F32)
        if has_lat:
            pl_ = jnp.exp(sl - mx)
            den = den + jnp.sum(pl_, axis=-1, keepdims=True)
            o = o + jnp.dot(pl_.astype(BF16), vl, preferred_element_type=F32)
        o_ref[:, g * dv:(g + 1) * dv] = (o / den).astype(o_ref.dtype)


def _attention(q, k, v, geom, *, groups, n_heads, kv_heads, dk, dv, latent_queries):
    wq, wk, wv = n_heads * dk, kv_heads * dk, kv_heads * dv
    wo = n_heads * dv
    batch = geom.batch
    nc = geom.n_ctx
    ctx_blk0 = geom.n_lat // nc
    assert geom.n_lat % nc == 0
    kern = functools.partial(_attn_kernel, n_heads=n_heads, kv_heads=kv_heads, dk=dk, dv=dv, has_lat=latent_queries)
    if latent_queries:
        tq = TT
        qt = geom.seq // tq
        return pl.pallas_call(
            kern,
            grid=(batch, groups, qt),
            in_specs=[
                pl.BlockSpec((tq, wq), lambda b, h, t: (b * qt + t, h)),
                pl.BlockSpec((geom.seq, wk), lambda b, h, t: (b, h)),
                pl.BlockSpec((geom.seq, wv), lambda b, h, t: (b, h)),
                pl.BlockSpec((nc, wk), lambda b, h, t: (ctx_blk0 + b, h)),
                pl.BlockSpec((nc, wv), lambda b, h, t: (ctx_blk0 + b, h)),
            ],
            out_specs=pl.BlockSpec((tq, wo), lambda b, h, t: (b * qt + t, h)),
            out_shape=jax.ShapeDtypeStruct((geom.n_lat, groups * wo), BF16),
            compiler_params=_cparams(("parallel", "parallel", "arbitrary")),
            name="attn_lat",
        )(q, k, v, k, v)
    return pl.pallas_call(
        kern,
        grid=(batch, groups),
        in_specs=[
            pl.BlockSpec((nc, wq), lambda b, h: (ctx_blk0 + b, h)),
            pl.BlockSpec((nc, wk), lambda b, h: (ctx_blk0 + b, h)),
            pl.BlockSpec((nc, wv), lambda b, h: (ctx_blk0 + b, h)),
        ],
        out_specs=pl.BlockSpec((nc, wo), lambda b, h: (b, h)),
        out_shape=jax.ShapeDtypeStruct((batch * nc, groups * wo), BF16),
        compiler_params=_cparams(("parallel", "parallel")),
        name="attn_ctx",
    )(q, k, v)


def _post_kernel(o_ref, wo_ref, s_ref, mod_ref, g_ref, wr_ref, br_ref, s_out, h2_out, idx_out, w_out, cnt_out, *, n_experts):
    m = mod_ref[...]
    s = s_ref[...] + m[2:3] * jnp.dot(o_ref[...], wo_ref[...], preferred_element_type=F32)
    s_out[...] = s
    h2 = _norm_mod(s, g_ref[...], m[3:4], m[4:5])
    h2_out[...] = h2.astype(BF16)
    logits = jnp.dot(h2, wr_ref[...], precision=HI, preferred_element_type=F32) + br_ref[...]
    lane = lax.broadcasted_iota(I32, logits.shape, 1).astype(F32)
    work = jnp.where(lane < n_experts, logits, -jnp.inf)
    vals, hits = [], []
    for k in range(TOP_K):
        mx = jnp.max(work, axis=-1, keepdims=True)
        sel = jnp.min(jnp.where(work == mx, lane, float(LANES)), axis=-1, keepdims=True)
        hit = lane == sel
        vals.append(mx)
        hits.append(hit)
        work = jnp.where(hit, -jnp.inf, work)
    es = [jnp.exp(v - vals[0]) for v in vals]
    den = es[0] + es[1] + es[2] + es[3]
    w_slab = jnp.zeros(logits.shape, F32)
    for k in range(TOP_K):
        w_slab = jnp.where(lane == k, es[k] / den, w_slab)
    w_out[...] = w_slab
    onehot = jnp.where(hits[0] | hits[1] | hits[2] | hits[3], 1.0, 0.0)
    tr = lax.broadcasted_iota(I32, (TT, TT), 0)
    tc = lax.broadcasted_iota(I32, (TT, TT), 1)
    rank = jnp.dot(jnp.where(tc < tr, 1.0, 0.0).astype(BF16), onehot.astype(BF16), preferred_element_type=F32)
    cnt = jnp.sum(onehot, axis=0, keepdims=True)
    cp = jnp.floor((cnt + (SUBLANES - 1)) * (1.0 / SUBLANES)) * SUBLANES
    er = lax.broadcasted_iota(I32, (LANES, LANES), 0)
    ec = lax.broadcasted_iota(I32, (LANES, LANES), 1)
    seg = jnp.dot(jnp.broadcast_to(cp, (SUBLANES, LANES)).astype(BF16), jnp.where(er < ec, 1.0, 0.0).astype(BF16),
                  preferred_element_type=F32)[0:1]
    base = seg + rank
    lp_slab = jnp.zeros(logits.shape, F32)
    for k in range(TOP_K):
        lp_slab = jnp.where(lane == k, jnp.sum(jnp.where(hits[k], base, 0.0), axis=-1, keepdims=True), lp_slab)
    idx_out[...] = lp_slab.astype(I32)
    cnt_out[...] = jnp.broadcast_to(cnt, (SUBLANES, LANES)).astype(I32)


def _post(o, w_o, stream, mods4, g_ffn, w_router, b_router, geom, layer, n_tiles):
    d = stream.shape[1]
    wo_w = o.shape[1]
    n_experts = w_router.shape[1]
    wr = jnp.concatenate([w_router, jnp.zeros((d, LANES - n_experts), F32)], axis=1)
    br = jnp.concatenate([b_router, jnp.zeros((LANES - n_experts,), F32)]).reshape(1, LANES)
    n = n_tiles * TT
    return pl.pallas_call(
        functools.partial(_post_kernel, n_experts=n_experts),
        grid=(n_tiles,),
        in_specs=[
            pl.BlockSpec((TT, wo_w), lambda j: (j, 0)),
            pl.BlockSpec((wo_w, d), lambda j: (0, 0)),
            pl.BlockSpec((TT, d), lambda j: (j, 0)),
            _mod_spec(geom, layer, d),
            pl.BlockSpec((1, d), lambda j: (0, 0)),
            pl.BlockSpec((d, LANES), lambda j: (0, 0)),
            pl.BlockSpec((1, LANES), lambda j: (0, 0)),
        ],
        out_specs=[
            pl.BlockSpec((TT, d), lambda j: (j, 0)),
            pl.BlockSpec((TT, d), lambda j: (j, 0)),
            pl.BlockSpec((TT, LANES), lambda j: (j, 0)),
            pl.BlockSpec((TT, LANES), lambda j: (j, 0)),
            pl.BlockSpec((None, SUBLANES, LANES), lambda j: (j, 0, 0)),
        ],
        out_shape=[
            jax.ShapeDtypeStruct((n, d), F32),
            jax.ShapeDtypeStruct((n, d), BF16),
            jax.ShapeDtypeStruct((n, LANES), I32),
            jax.ShapeDtypeStruct((n, LANES), F32),
            jax.ShapeDtypeStruct((n_tiles, SUBLANES, LANES), I32),
        ],
        compiler_params=_cparams(("parallel",)),
        name="post",
    )(o, w_o, stream, mods4, g_ffn, wr, br)


def _local_rows(n_experts):
    return _round_up(TT * TOP_K + n_experts * (SUBLANES - 1), LANES)


def _n_blocks(n_tok, n_tiles, n_experts):
    rows = n_tok * TOP_K + n_experts * n_tiles * (SUBLANES - 1) + n_experts * (EXPERT_BLOCK - 1)
    return pl.cdiv(rows, EXPERT_BLOCK)


def _route_meta(cnt, lpos, n_experts):
    n_tiles = cnt.shape[0]
    n_blocks = _n_blocks(n_tiles * TT, n_tiles, n_experts)
    cp = _round_up(cnt, SUBLANES)
    seg = jnp.cumsum(cp, axis=1) - cp
    run_rows = cp.sum(axis=0)
    reg = _round_up(run_rows, EXPERT_BLOCK)
    reg_end = jnp.cumsum(reg)
    reg_start = reg_end - reg
    off = reg_start[None, :] + jnp.cumsum(cp, axis=0) - cp
    n_used = (reg_end[-1] // EXPERT_BLOCK).astype(I32)
    bstart = jnp.arange(n_blocks, dtype=I32) * EXPERT_BLOCK
    last_start = jnp.maximum(n_used - 1, 0) * EXPERT_BLOCK
    be = (jnp.minimum(bstart, last_start)[:, None] >= reg_end[None, :]).sum(axis=1).astype(I32)
    be = jnp.minimum(be, n_experts - 1)
    n_big = (cp // BIG_CHUNK).sum(axis=1)
    n_small = ((cp % BIG_CHUNK) // SUBLANES).sum(axis=1)
    return dict(
        cp=cp.reshape(-1).astype(I32), seg=seg.reshape(-1).astype(I32), off=off.reshape(-1).astype(I32),
        n_big=n_big.astype(I32), n_small=n_small.astype(I32),
        gap_start=(reg_start + run_rows).astype(I32), gap=(reg - run_rows).astype(I32),
        lpos=lpos.astype(I32), block_expert=be, n_used=n_used.reshape(1), n_blocks=n_blocks,
    )


def _run_copies(tile, n_experts, cp_ref, seg_ref, off_ref, make_big, make_small):
    def per_expert(e, carry):
        cp = cp_ref[tile * n_experts + e]
        seg = seg_ref[tile * n_experts + e]
        off = off_ref[tile * n_experts + e]
        nb = cp // BIG_CHUNK

        def big(j, c):
            make_big(pl.multiple_of(seg + j * BIG_CHUNK, SUBLANES), pl.multiple_of(off + j * BIG_CHUNK, SUBLANES)).start()
            return c

        lax.fori_loop(0, nb, big, 0)
        base = nb * BIG_CHUNK

        def small(j, c):
            make_small(pl.multiple_of(seg + base + j * SUBLANES, SUBLANES),
                       pl.multiple_of(off + base + j * SUBLANES, SUBLANES)).start()
            return c

        lax.fori_loop(0, (cp - base) // SUBLANES, small, 0)
        return carry

    lax.fori_loop(0, n_experts, per_expert, 0)


def _wait_copies(tile, nbig_ref, nsmall_ref, make_big, make_small):
    def wb(j, c):
        make_big(0, 0).wait()
        return c

    def ws(j, c):
        make_small(0, 0).wait()
        return c

    lax.fori_loop(0, nbig_ref[tile], wb, 0)
    lax.fori_loop(0, nsmall_ref[tile], ws, 0)


def _dispatch_kernel(cp_ref, seg_ref, off_ref, nbig_ref, nsmall_ref, gs_ref, gap_ref, nu_ref,
                     lpt_ref, h2_ref, xs_ref, buf_ref, zero_ref, sem, zsem, *, n_experts, local_rows, n_blocks):
    i = pl.program_id(0)
    n = pl.num_programs(0)
    slot = i % 2

    def big(src, dst):
        return pltpu.make_async_copy(buf_ref.at[slot, pl.ds(src, BIG_CHUNK), :], xs_ref.at[pl.ds(dst, BIG_CHUNK), :], sem.at[slot])

    def small(src, dst):
        return pltpu.make_async_copy(buf_ref.at[slot, pl.ds(src, SUBLANES), :], xs_ref.at[pl.ds(dst, SUBLANES), :], sem.at[slot])

    @pl.when(i >= 2)
    def _():
        _wait_copies(i - 2, nbig_ref, nsmall_ref, big, small)

    lpt = lpt_ref[...]
    rows = lax.broadcasted_iota(I32, (local_rows, TT), 0)
    hit = rows == lpt[0:1, :]
    for k in range(1, TOP_K):
        hit = hit | (rows == lpt[k:k + 1, :])
    p = jnp.where(hit, 1.0, 0.0).astype(BF16)
    buf_ref[slot] = jnp.dot(p, h2_ref[...], preferred_element_type=F32)
    _run_copies(i, n_experts, cp_ref, seg_ref, off_ref, big, small)

    @pl.when(i == n - 1)
    def _():
        zero_ref[...] = jnp.zeros_like(zero_ref)
        sizes = [s for s in (128, 64, 32, 16, 8) if s < EXPERT_BLOCK]

        def zcopy(dst, size):
            return pltpu.make_async_copy(zero_ref.at[pl.ds(0, size), :], xs_ref.at[pl.ds(dst, size), :], zsem)

        def per_expert(e, carry):
            gap = gap_ref[e]
            pos = gs_ref[e]
            for size in sizes:
                take = (gap & size) != 0

                @pl.when(take)
                def _():
                    zcopy(pl.multiple_of(pos, SUBLANES), size).start()

                pos = pos + jnp.where(take, size, 0)
            return carry

        lax.fori_loop(0, n_experts, per_expert, 0)

        zrows = zero_ref.shape[0]
        per_block = EXPERT_BLOCK // zrows
        tail_copies = (n_blocks - nu_ref[0]) * per_block

        def tail(t, carry):
            zcopy(pl.multiple_of(nu_ref[0] * EXPERT_BLOCK + t * zrows, SUBLANES), zrows).start()
            return carry

        lax.fori_loop(0, tail_copies, tail, 0)

        def per_expert_wait(e, carry):
            gap = gap_ref[e]
            for size in sizes:
                @pl.when((gap & size) != 0)
                def _():
                    zcopy(0, size).wait()
            return carry

        lax.fori_loop(0, n_experts, per_expert_wait, 0)

        def tail_wait(t, carry):
            zcopy(0, zrows).wait()
            return carry

        lax.fori_loop(0, tail_copies, tail_wait, 0)

        @pl.when(i >= 1)
        def _():
            def big_o(src, dst):
                return pltpu.make_async_copy(buf_ref.at[1 - slot, pl.ds(src, BIG_CHUNK), :], xs_ref.at[pl.ds(dst, BIG_CHUNK), :], sem.at[1 - slot])

            def small_o(src, dst):
                return pltpu.make_async_copy(buf_ref.at[1 - slot, pl.ds(src, SUBLANES), :], xs_ref.at[pl.ds(dst, SUBLANES), :], sem.at[1 - slot])

            _wait_copies(i - 1, nbig_ref, nsmall_ref, big_o, small_o)

        _wait_copies(i, nbig_ref, nsmall_ref, big, small)


def _dispatch(h2, meta, n_tiles, n_experts):
    n, d = h2.shape
    local_rows = _local_rows(n_experts)
    n_rows = meta["n_blocks"] * EXPERT_BLOCK
    lpt = meta["lpos"].reshape(n_tiles, TT, TOP_K).transpose(0, 2, 1)
    grid_spec = pltpu.PrefetchScalarGridSpec(
        num_scalar_prefetch=8,
        grid=(n_tiles,),
        in_specs=[
            pl.BlockSpec((None, TOP_K, TT), lambda j, *_: (j, 0, 0)),
            pl.BlockSpec((TT, d), lambda j, *_: (j, 0)),
        ],
        out_specs=pl.BlockSpec(memory_space=pl.ANY),
        scratch_shapes=[
            pltpu.VMEM((2, local_rows, d), F32),
            pltpu.VMEM((128, d), F32),
            pltpu.SemaphoreType.DMA((2,)),
            pltpu.SemaphoreType.DMA(()),
        ],
    )
    return pl.pallas_call(
        functools.partial(_dispatch_kernel, n_experts=n_experts, local_rows=local_rows, n_blocks=meta["n_blocks"]),
        grid_spec=grid_spec,
        out_shape=jax.ShapeDtypeStruct((n_rows, d), F32),
        compiler_params=_cparams(("arbitrary",)),
        name="moe_dispatch",
    )(meta["cp"], meta["seg"], meta["off"], meta["n_big"], meta["n_small"], meta["gap_start"], meta["gap"], meta["n_used"],
      lpt, h2)


def _moe_prep_kernel(wgu_ref, wd_ref, sel_ref, wg_out, wu_out, wd_out):
    sel = sel_ref[...]
    pair = 2 * LANES
    for c in range(wgu_ref.shape[1] // pair):
        chunk = wgu_ref[:, c * pair:(c + 1) * pair].astype(BF16)
        de = jnp.dot(chunk, sel, preferred_element_type=F32)
        wg_out[:, c * LANES:(c + 1) * LANES] = de[:, :LANES].astype(BF16)
        wu_out[:, c * LANES:(c + 1) * LANES] = de[:, LANES:].astype(BF16)
    wd_out[...] = wd_ref[...].astype(BF16)


def _moe_prep(w_gu, w_down):
    depth, n_experts, d, f2 = w_gu.shape
    f = f2 // 2
    r = jnp.arange(2 * LANES)
    src = jnp.where(r < LANES, 2 * r, 2 * (r - LANES) + 1)
    sel = (r[:, None] == src[None, :]).astype(BF16)
    wspec = lambda rows, cols: pl.BlockSpec((None, None, rows, cols), lambda i, e: (i, e, 0, 0))
    return pl.pallas_call(
        _moe_prep_kernel,
        grid=(depth, n_experts),
        in_specs=[wspec(d, f2), wspec(f, d), pl.BlockSpec((2 * LANES, 2 * LANES), lambda i, e: (0, 0))],
        out_specs=[wspec(d, f), wspec(d, f), wspec(f, d)],
        out_shape=[
            jax.ShapeDtypeStruct((depth, n_experts, d, f), BF16),
            jax.ShapeDtypeStruct((depth, n_experts, d, f), BF16),
            jax.ShapeDtypeStruct((depth, n_experts, f, d), BF16),
        ],
        compiler_params=_cparams(("parallel", "parallel")),
        name="moe_prep",
    )(w_gu, w_down, sel)


def _ffn_kernel(be_ref, nu_ref, x_ref, wg_ref, wu_ref, wd_ref, bg_ref, bu_ref, bd_ref, y_ref):
    b = pl.program_id(0)

    @pl.when(b < nu_ref[0])
    def _():
        x = x_ref[...].astype(BF16)
        gate = jnp.dot(x, wg_ref[...], preferred_element_type=F32) + bg_ref[...]
        up = jnp.dot(x, wu_ref[...], preferred_element_type=F32) + bu_ref[...]
        gate = jnp.minimum(gate, SWIGLU_LIMIT)
        up = jnp.clip(up, -SWIGLU_LIMIT, SWIGLU_LIMIT)
        glu = gate / (1.0 + jnp.exp(-SWIGLU_ALPHA * gate))
        act = ((up + 1.0) * glu).astype(BF16)
        y_ref[...] = jnp.dot(act, wd_ref[...], preferred_element_type=F32) + bd_ref[...]

    @pl.when(b >= nu_ref[0])
    def _():
        y_ref[...] = jnp.zeros_like(y_ref)


def _ffn(xs, meta, layer, w_gate, w_up, w_down, b_gate, b_up, b_down):
    n_rows, d = xs.shape
    n_blocks = meta["n_blocks"]
    f = w_gate.shape[3]
    xmap = lambda b, be, nu: (jnp.minimum(b, jnp.maximum(nu[0] - 1, 0)), 0)
    wmap = lambda b, be, nu: (layer, be[b], 0, 0)
    grid_spec = pltpu.PrefetchScalarGridSpec(
        num_scalar_prefetch=2,
        grid=(n_blocks,),
        in_specs=[
            pl.BlockSpec((EXPERT_BLOCK, d), xmap),
            pl.BlockSpec((None, None, d, f), wmap),
            pl.BlockSpec((None, None, d, f), wmap),
            pl.BlockSpec((None, None, f, d), wmap),
            pl.BlockSpec((None, None, 1, f), wmap),
            pl.BlockSpec((None, None, 1, f), wmap),
            pl.BlockSpec((None, None, 1, d), wmap),
        ],
        out_specs=pl.BlockSpec((EXPERT_BLOCK, d), lambda b, be, nu: (b, 0)),
    )
    return pl.pallas_call(
        _ffn_kernel,
        grid_spec=grid_spec,
        out_shape=jax.ShapeDtypeStruct((n_rows, d), F32),
        compiler_params=_cparams(("arbitrary",)),
        name="moe_ffn",
    )(meta["block_expert"], meta["n_used"], xs, w_gate, w_up, w_down, b_gate, b_up, b_down)


def _combine_kernel(cp_ref, seg_ref, off_ref, nbig_ref, nsmall_ref,
                    lp_ref, w_ref, s_ref, mod_ref, gf_ref, ys_ref, o_ref, buf_ref, sem, *, n_experts, local_rows, final):
    i = pl.program_id(0)
    n = pl.num_programs(0)
    slot = i % 2

    def copies(sl):
        def big(dst, src):
            return pltpu.make_async_copy(ys_ref.at[pl.ds(src, BIG_CHUNK), :], buf_ref.at[sl, pl.ds(dst, BIG_CHUNK), :], sem.at[sl])

        def small(dst, src):
            return pltpu.make_async_copy(ys_ref.at[pl.ds(src, SUBLANES), :], buf_ref.at[sl, pl.ds(dst, SUBLANES), :], sem.at[sl])

        return big, small

    @pl.when(i == 0)
    def _():
        buf_ref[...] = jnp.zeros_like(buf_ref)
        _run_copies(0, n_experts, cp_ref, seg_ref, off_ref, *copies(0))

    @pl.when(i + 1 < n)
    def _():
        _run_copies(i + 1, n_experts, cp_ref, seg_ref, off_ref, *copies(1 - slot))

    _wait_copies(i, nbig_ref, nsmall_ref, *copies(slot))

    lp = lp_ref[...]
    w = w_ref[...]
    lanes = lax.broadcasted_iota(I32, (TT, local_rows), 1)
    pw = jnp.zeros((TT, local_rows), F32)
    for k in range(TOP_K):
        pw = pw + jnp.where(lanes == lp[:, k:k + 1], w[:, k:k + 1], 0.0)
    y = buf_ref[slot].astype(BF16)
    f = jnp.dot(pw.astype(BF16), y, preferred_element_type=F32)
    m = mod_ref[...]
    s = s_ref[...] + m[5:6] * f
    if final:
        s = _rms(s) * gf_ref[...]
    o_ref[...] = s


def _combine(ys, meta, top_w, stream, mods4, g_final, geom, layer, n_tiles, n_experts, final):
    d = stream.shape[1]
    local_rows = _local_rows(n_experts)
    grid_spec = pltpu.PrefetchScalarGridSpec(
        num_scalar_prefetch=5,
        grid=(n_tiles,),
        in_specs=[
            pl.BlockSpec((TT, TOP_K), lambda j, *_: (j, 0)),
            pl.BlockSpec((TT, TOP_K), lambda j, *_: (j, 0)),
            pl.BlockSpec((TT, d), lambda j, *_: (j, 0)),
            pl.BlockSpec((None, None, N_MOD, d), lambda j, *_: (layer, geom.group(j), 0, 0)),
            pl.BlockSpec((1, d), lambda j, *_: (0, 0)),
            pl.BlockSpec(memory_space=pl.ANY),
        ],
        out_specs=pl.BlockSpec((TT, d), lambda j, *_: (j, 0)),
        scratch_shapes=[
            pltpu.VMEM((2, local_rows, d), F32),
            pltpu.SemaphoreType.DMA((2,)),
        ],
    )
    return pl.pallas_call(
        functools.partial(_combine_kernel, n_experts=n_experts, local_rows=local_rows, final=final),
        grid_spec=grid_spec,
        out_shape=jax.ShapeDtypeStruct((n_tiles * TT, d), F32),
        compiler_params=_cparams(("arbitrary",)),
        name="moe_combine",
    )(meta["cp"], meta["seg"], meta["off"], meta["n_big"], meta["n_small"],
      meta["lpos"], top_w, stream, mods4, g_final, ys)


def kernel(x, c, ctx, c_ctx, w_mod, b_mod, g_mix, g_ffn, g_final, f_w_in, f_w_out, mla_w_in, mla_g_qa, mla_w_qb,
           mla_g_kva, mla_w_kvb, mla_w_o, gqa_w_qkv, gqa_g_q, gqa_g_k, gqa_w_o, moe_w_router, moe_b_router,
           moe_w_gu, moe_b_gu, moe_w_down, moe_b_down):
    batch, seq, d = x.shape
    n_ctx = ctx.shape[1]
    depth = w_mod.shape[0]
    n_experts = moe_w_router.shape[2]
    geom = _Geom(batch, seq, n_ctx)

    n_groups = _round_up(batch + 1, SUBLANES)
    cc = jnp.concatenate([c, c_ctx[None, :], jnp.zeros((n_groups - batch - 1, d), F32)], axis=0)
    mods4 = _mods(cc, w_mod, b_mod).reshape(depth, n_groups, N_MOD, d)

    stream = jnp.concatenate([x.reshape(batch * seq, d), ctx.reshape(batch * n_ctx, d)], axis=0)

    gd = d // FOURIER_GROUPS
    gcos, gsin = _dft_tables(gd)
    eye = jnp.eye(FOURIER_GROUPS, dtype=F32)
    bd_cs = jnp.concatenate([jnp.kron(eye, gcos), jnp.kron(eye, gsin)], axis=1)

    f = moe_w_down.shape[2]
    w_gate, w_up, w_down = _moe_prep(moe_w_gu, moe_w_down)
    b_gate = moe_b_gu[:, :, 0::2].reshape(depth, n_experts, 1, f)
    b_up = moe_b_gu[:, :, 1::2].reshape(depth, n_experts, 1, f)
    b_down = moe_b_down.reshape(depth, n_experts, 1, d)

    for i in range(depth):
        kind, j = i % N_MIXERS, i // N_MIXERS
        last = i == depth - 1
        ctx_used = (kind != 0) or (not last)
        n_tiles = geom.n_tiles if ctx_used else geom.n_lat_tiles
        gm = g_mix[i].reshape(1, d)
        gf = g_ffn[i].reshape(1, d)

        if kind == 0:
            wcs = _fold(f_w_in[j], bd_cs, BF16)
            u = _pre_fourier(stream, mods4, gm, wcs, geom, i, n_tiles)
            o = _dft(u, 0, seq, batch, d)
            if ctx_used:
                o = jnp.concatenate([o, _dft(u, geom.n_lat, n_ctx, batch, d)], axis=0)
            w_o = f_w_out[j].astype(BF16)
        elif kind == 1:
            q, k, v = _pre_mla(stream, mods4, gm, mla_w_in[j], mla_g_qa[j], mla_w_qb[j], mla_g_kva[j], mla_w_kvb[j], geom, i)
            att = functools.partial(_attention, q, k, v, geom, groups=4, n_heads=MLA_HEADS // 4, kv_heads=MLA_HEADS // 4,
                                    dk=MLA_SLOT, dv=MLA_V)
            o = att(latent_queries=True)
            if not last:
                o = jnp.concatenate([o, att(latent_queries=False)], axis=0)
            w_o = mla_w_o[j].astype(BF16)
        else:
            q, k, v = _pre_gqa(stream, mods4, gm, gqa_w_qkv[j], gqa_g_q[j], gqa_g_k[j], geom, i)
            att = functools.partial(_attention, q, k, v, geom, groups=1, n_heads=GQA_HEADS, kv_heads=GQA_KV_HEADS,
                                    dk=GQA_HEAD_DIM, dv=GQA_HEAD_DIM)
            o = att(latent_queries=True)
            if not last:
                o = jnp.concatenate([o, att(latent_queries=False)], axis=0)
            w_o = gqa_w_o[j].astype(BF16)

        n_moe_tiles = geom.n_lat_tiles if last else geom.n_tiles
        stream, h2, lp_slab, w_slab, cnt_slab = _post(o, w_o, stream, mods4, gf, moe_w_router[i], moe_b_router[i], geom, i,
                                                      n_moe_tiles)
        meta = _route_meta(cnt_slab[:, 0, :n_experts], lp_slab[:, :TOP_K], n_experts)
        xs = _dispatch(h2, meta, n_moe_tiles, n_experts)
        ys = _ffn(xs, meta, i, w_gate, w_up, w_down, b_gate, b_up, b_down)
        stream = _combine(ys, meta, w_slab[:, :TOP_K], stream, mods4, g_final.reshape(1, d), geom, i, n_moe_tiles, n_experts,
                          last)

    return stream[:batch * seq].reshape(batch, seq, d)
```

```python
import functools

import jax
import jax.numpy as jnp
from jax import lax
from jax.experimental import pallas as pl
from jax.experimental.pallas import tpu as pltpu

F32 = jnp.float32
BF16 = jnp.bfloat16
I32 = jnp.int32
HI = lax.Precision.HIGHEST

GRID_W = 64
N_MIXERS = 3
NORM_EPS = 1e-6
ROPE_THETA = 10000.0
FOURIER_GROUPS = 8
MLA_HEADS = 16
MLA_Q_LORA = 384
MLA_KV_LORA = 256
MLA_NOPE = 64
MLA_ROPE = 32
MLA_V = 64
GQA_HEADS = 16
GQA_KV_HEADS = 4
GQA_HEAD_DIM = 64
TOP_K = 4
SWIGLU_LIMIT = 7.0
SWIGLU_ALPHA = 1.702
N_MOD = 6

LANES = 128
SUBLANES = 8
TT = 256
EXPERT_BLOCK = 512
BIG_CHUNK = 32
MLA_SLOT = 128
VMEM_LIMIT = 56 * 1024 * 1024


def _cparams(sem):
    return pltpu.CompilerParams(dimension_semantics=sem, vmem_limit_bytes=VMEM_LIMIT)


def _round_up(x, m):
    return (x + m - 1) // m * m


def _norm_mod(x, g, shift, scale):
    y = x * lax.rsqrt(jnp.mean(x * x, axis=-1, keepdims=True) + NORM_EPS) * g
    return y * (1.0 + scale) + shift


def _rms(x):
    return x * lax.rsqrt(jnp.mean(x * x, axis=-1, keepdims=True) + NORM_EPS)


def _mod_kernel(cc_ref, w_ref, b_ref, o_ref):
    cc = cc_ref[...]
    a = cc / (1.0 + jnp.exp(-cc))
    o_ref[...] = jnp.dot(a, w_ref[...], precision=HI, preferred_element_type=F32) + b_ref[...]


def _mods(cc, w_mod, b_mod):
    depth, d, n6 = w_mod.shape
    g = cc.shape[0]
    tn = 1536 if n6 % 1536 == 0 else n6
    return pl.pallas_call(
        _mod_kernel,
        grid=(depth, n6 // tn),
        in_specs=[
            pl.BlockSpec((g, d), lambda i, n: (0, 0)),
            pl.BlockSpec((None, d, tn), lambda i, n: (i, 0, n)),
            pl.BlockSpec((None, 1, tn), lambda i, n: (i, 0, n)),
        ],
        out_specs=pl.BlockSpec((None, g, tn), lambda i, n: (i, 0, n)),
        out_shape=jax.ShapeDtypeStruct((depth, g, n6), F32),
        compiler_params=_cparams(("parallel", "parallel")),
        name="mods",
    )(cc, w_mod, b_mod.reshape(depth, 1, n6))


def _fold_kernel(a_ref, b_ref, o_ref):
    o_ref[...] = jnp.dot(a_ref[...], b_ref[...], precision=HI, preferred_element_type=F32).astype(o_ref.dtype)


def _fold(a, b, out_dtype):
    m, k = a.shape
    n = b.shape[1]
    tn = 512
    return pl.pallas_call(
        _fold_kernel,
        grid=(n // tn,),
        in_specs=[pl.BlockSpec((m, k), lambda j: (0, 0)), pl.BlockSpec((k, tn), lambda j: (0, j))],
        out_specs=pl.BlockSpec((m, tn), lambda j: (0, j)),
        out_shape=jax.ShapeDtypeStruct((m, n), out_dtype),
        compiler_params=_cparams(("parallel",)),
        name="fold",
    )(a, b)


class _Geom:
    def __init__(self, batch, seq, n_ctx):
        self.batch, self.seq, self.n_ctx = batch, seq, n_ctx
        assert seq % TT == 0 and n_ctx % TT == 0 and seq % GRID_W == 0
        self.lat_per_b = seq // TT
        self.ctx_per_b = n_ctx // TT
        self.n_lat_tiles = batch * self.lat_per_b
        self.n_ctx_tiles = batch * self.ctx_per_b
        self.n_tiles = self.n_lat_tiles + self.n_ctx_tiles
        self.n_lat = batch * seq
        self.n_tok = self.n_lat + batch * n_ctx

    def group(self, j):
        return jnp.where(j < self.n_lat_tiles, j // self.lat_per_b, self.batch)

    def pos_tile(self, j):
        return jnp.where(j < self.n_lat_tiles, j % self.lat_per_b, self.lat_per_b)


def _mod_spec(geom, layer, d):
    return pl.BlockSpec((None, None, N_MOD, d), lambda j: (layer, geom.group(j), 0, 0))


def _pre_fourier_kernel(s_ref, mod_ref, g_ref, w_ref, u_ref):
    m = mod_ref[...]
    h = _norm_mod(s_ref[...], g_ref[...], m[0:1], m[1:2])
    u_ref[...] = jnp.dot(h.astype(BF16), w_ref[...], preferred_element_type=F32).astype(BF16)


def _pre_fourier(stream, mods4, g_mix, wcs, geom, layer, n_tiles):
    d = stream.shape[1]
    return pl.pallas_call(
        _pre_fourier_kernel,
        grid=(n_tiles,),
        in_specs=[
            pl.BlockSpec((TT, d), lambda j: (j, 0)),
            _mod_spec(geom, layer, d),
            pl.BlockSpec((1, d), lambda j: (0, 0)),
            pl.BlockSpec((d, 2 * d), lambda j: (0, 0)),
        ],
        out_specs=pl.BlockSpec((TT, 2 * d), lambda j: (j, 0)),
        out_shape=jax.ShapeDtypeStruct((n_tiles * TT, 2 * d), BF16),
        compiler_params=_cparams(("parallel",)),
        name="pre_fourier",
    )(stream, mods4, g_mix, wcs)


def _dft_kernel(c_ref, s_ref, u_ref, o_ref, acc_ref, *, d):
    k = pl.program_id(2)

    @pl.when(k == 0)
    def _():
        acc_ref[...] = jnp.zeros_like(acc_ref)

    acc_ref[...] += (jnp.dot(c_ref[...], u_ref[:, :d], preferred_element_type=F32)
                     - jnp.dot(s_ref[...], u_ref[:, d:], preferred_element_type=F32))

    @pl.when(k == pl.num_programs(2) - 1)
    def _():
        o_ref[...] = acc_ref[...].astype(o_ref.dtype)


def _dft_tables(n):
    k = lax.broadcasted_iota(I32, (n, n), 0)
    m = lax.broadcasted_iota(I32, (n, n), 1)
    ang = ((k * m) % n).astype(F32) * (2.0 * jnp.pi / n)
    scale = n ** -0.5
    return (jnp.cos(ang) * scale), (jnp.sin(ang) * scale)


def _dft(u, row_off, seq, batch, d):
    cos, sin = _dft_tables(seq)
    cos, sin = cos.astype(BF16), sin.astype(BF16)
    tm = min(seq, 1024)
    tk = min(seq, 512)
    assert row_off % tk == 0
    kb = seq // tk
    return pl.pallas_call(
        functools.partial(_dft_kernel, d=d),
        grid=(batch, seq // tm, kb),
        in_specs=[
            pl.BlockSpec((tm, tk), lambda b, m, k: (m, k)),
            pl.BlockSpec((tm, tk), lambda b, m, k: (m, k)),
            pl.BlockSpec((tk, 2 * d), lambda b, m, k: (row_off // tk + b * kb + k, 0)),
        ],
        out_specs=pl.BlockSpec((tm, d), lambda b, m, k: (b * (seq // tm) + m, 0)),
        out_shape=jax.ShapeDtypeStruct((batch * seq, d), BF16),
        scratch_shapes=[pltpu.VMEM((tm, d), F32)],
        compiler_params=_cparams(("parallel", "parallel", "arbitrary")),
        name="dft",
    )(cos, sin, u)


def _pre_mla_kernel(s_ref, mod_ref, g_ref, win_ref, gqa_ref, gkva_ref, wqb_ref, wqbs_ref, wkvk_ref, wkvv_ref,
                    cos_ref, sin_ref, q_ref, k_ref, v_ref, *, scale):
    m = mod_ref[...]
    h = _norm_mod(s_ref[...], g_ref[...], m[0:1], m[1:2])
    a = jnp.dot(h.astype(BF16), win_ref[...], preferred_element_type=F32)
    aq = (_rms(a[:, :MLA_Q_LORA]) * gqa_ref[...]).astype(BF16)
    ckv = (_rms(a[:, MLA_Q_LORA:MLA_Q_LORA + MLA_KV_LORA]) * gkva_ref[...]).astype(BF16)
    o = MLA_Q_LORA + MLA_KV_LORA
    cos = cos_ref[...]
    sin = sin_ref[...]
    kpe = a[:, o:o + MLA_SLOT] * cos + a[:, o + MLA_SLOT:o + 2 * MLA_SLOT] * sin
    cos_h = jnp.tile(cos, (1, MLA_HEADS))
    sin_h = jnp.tile(sin, (1, MLA_HEADS))
    q = jnp.dot(aq, wqb_ref[...], preferred_element_type=F32)
    qs = jnp.dot(aq, wqbs_ref[...], preferred_element_type=F32)
    q_ref[...] = ((q * cos_h + qs * sin_h) * scale).astype(BF16)
    kk = jnp.dot(ckv, wkvk_ref[...], preferred_element_type=F32)
    k_ref[...] = (kk + jnp.tile(kpe, (1, MLA_HEADS))).astype(BF16)
    v_ref[...] = lax.dot_general(wkvv_ref[...], ckv, (((1,), (1,)), ((), ())), preferred_element_type=F32).astype(BF16)


def _rope_perm(rot):
    sec = rot // 2
    half = sec // 2
    d = jnp.arange(rot)
    first = (d % sec) < half
    partner = jnp.where(first, d + half, d - half)
    sign = jnp.where(first, -1.0, 1.0).astype(F32)
    return partner, sign


def _rope_tables(seq, rot, lat_per_b):
    rows = seq // GRID_W
    row = jnp.repeat(jnp.arange(rows, dtype=F32), GRID_W)
    col = jnp.tile(jnp.arange(GRID_W, dtype=F32), rows)
    n_freq = rot // 4
    inv_freq = ROPE_THETA ** (-jnp.arange(n_freq, dtype=F32) / n_freq)
    ang = jnp.stack([row[:, None] * inv_freq, col[:, None] * inv_freq], axis=1)
    cos = jnp.cos(ang)
    sin = jnp.sin(ang)
    cos_full = jnp.concatenate([cos, cos], axis=-1).reshape(seq, rot)
    sin_full = jnp.concatenate([sin, sin], axis=-1).reshape(seq, rot)
    _, sign = _rope_perm(rot)
    sin_full = sin_full * sign
    ident_c = jnp.ones((TT, rot), F32)
    ident_s = jnp.zeros((TT, rot), F32)
    assert seq == lat_per_b * TT
    return jnp.concatenate([cos_full, ident_c], 0), jnp.concatenate([sin_full, ident_s], 0)


def _pre_mla(stream, mods4, g_mix, w_in, g_qa, w_qb, g_kva, w_kvb, geom, layer):
    d = stream.shape[1]
    hd = MLA_NOPE + MLA_ROPE
    partner, _ = _rope_perm(MLA_ROPE)
    o = MLA_Q_LORA + MLA_KV_LORA
    kpe_w = w_in[:, o:]
    z_lo = jnp.zeros((d, MLA_NOPE), F32)
    z_hi = jnp.zeros((d, MLA_SLOT - hd), F32)
    w_in_ext = jnp.concatenate([w_in[:, :o], z_lo, kpe_w, z_hi, z_lo, kpe_w[:, partner], z_hi], axis=1).astype(BF16)
    wq = w_qb.reshape(MLA_Q_LORA, MLA_HEADS, hd)
    zq = jnp.zeros((MLA_Q_LORA, MLA_HEADS, MLA_SLOT - hd), F32)
    wq_p = jnp.concatenate([wq, zq], axis=2).reshape(MLA_Q_LORA, MLA_HEADS * MLA_SLOT).astype(BF16)
    wq_s = jnp.concatenate([jnp.zeros_like(wq[:, :, :MLA_NOPE]), wq[:, :, MLA_NOPE:][:, :, partner], zq], axis=2)
    wq_s = wq_s.reshape(MLA_Q_LORA, MLA_HEADS * MLA_SLOT).astype(BF16)
    wkv = w_kvb.reshape(MLA_KV_LORA, MLA_HEADS, MLA_NOPE + MLA_V)
    zk = jnp.zeros((MLA_KV_LORA, MLA_HEADS, MLA_SLOT - MLA_NOPE), F32)
    wkv_k = jnp.concatenate([wkv[:, :, :MLA_NOPE], zk], axis=2).reshape(MLA_KV_LORA, MLA_HEADS * MLA_SLOT).astype(BF16)
    wkv_v = wkv[:, :, MLA_NOPE:].reshape(MLA_KV_LORA, MLA_HEADS * MLA_V).T.astype(BF16)
    cos, sin = _rope_tables(geom.seq, MLA_ROPE, geom.lat_per_b)
    rows = cos.shape[0]
    cos_slot = jnp.concatenate([jnp.ones((rows, MLA_NOPE), F32), cos, jnp.ones((rows, MLA_SLOT - hd), F32)], axis=1)
    sin_slot = jnp.concatenate([jnp.zeros((rows, MLA_NOPE), F32), sin, jnp.zeros((rows, MLA_SLOT - hd), F32)], axis=1)
    n_tiles = geom.n_tiles
    wq_w = MLA_HEADS * MLA_SLOT
    wv_w = MLA_HEADS * MLA_V
    full = lambda a: pl.BlockSpec(a.shape, lambda j: (0,) * a.ndim)
    g_qa2, g_kva2 = g_qa.reshape(1, -1), g_kva.reshape(1, -1)
    return pl.pallas_call(
        functools.partial(_pre_mla_kernel, scale=float(hd) ** -0.5),
        grid=(n_tiles,),
        in_specs=[
            pl.BlockSpec((TT, d), lambda j: (j, 0)),
            _mod_spec(geom, layer, d),
            pl.BlockSpec((1, d), lambda j: (0, 0)),
            full(w_in_ext), full(g_qa2), full(g_kva2), full(wq_p), full(wq_s), full(wkv_k), full(wkv_v),
            pl.BlockSpec((TT, MLA_SLOT), lambda j: (geom.pos_tile(j), 0)),
            pl.BlockSpec((TT, MLA_SLOT), lambda j: (geom.pos_tile(j), 0)),
        ],
        out_specs=[
            pl.BlockSpec((TT, wq_w), lambda j: (j, 0)),
            pl.BlockSpec((TT, wq_w), lambda j: (j, 0)),
            pl.BlockSpec((wv_w, TT), lambda j: (0, j)),
        ],
        out_shape=[
            jax.ShapeDtypeStruct((n_tiles * TT, wq_w), BF16),
            jax.ShapeDtypeStruct((n_tiles * TT, wq_w), BF16),
            jax.ShapeDtypeStruct((wv_w, n_tiles * TT), BF16),
        ],
        compiler_params=_cparams(("parallel",)),
        name="pre_mla",
    )(stream, mods4, g_mix, w_in_ext, g_qa2, g_kva2, wq_p, wq_s, wkv_k, wkv_v, cos_slot, sin_slot)


def _split_hi_lo(x):
    hi = x.astype(BF16)
    lo = (x - hi.astype(F32)).astype(BF16)
    return hi, lo


def _pre_gqa_kernel(s_ref, mod_ref, g_ref, w_ref, wvt_ref, bd_ref, cq_ref, sq_ref, ck_ref, sk_ref, q_ref, k_ref, v_ref, *,
                    scale):
    m = mod_ref[...]
    h = _norm_mod(s_ref[...], g_ref[...], m[0:1], m[1:2]).astype(BF16)
    a = jnp.dot(h, w_ref[...], preferred_element_type=F32)
    nq = GQA_HEADS * GQA_HEAD_DIM
    nkv = GQA_KV_HEADS * GQA_HEAD_DIM
    aq, ak = a[:, :nq], a[:, nq:nq + nkv]
    aqs, aks = a[:, nq + nkv:2 * nq + nkv], a[:, 2 * nq + nkv:]
    bd = bd_ref[...]

    def head_rs(x, width):
        hi, lo = _split_hi_lo(x * x)
        b = bd[:width, :width]
        ssq = jnp.dot(hi, b, preferred_element_type=F32) + jnp.dot(lo, b, preferred_element_type=F32)
        return lax.rsqrt(ssq * (1.0 / GQA_HEAD_DIM) + NORM_EPS)

    reps_q = nq // cq_ref.shape[1]
    reps_k = nkv // ck_ref.shape[1]
    q = head_rs(aq, nq) * (aq * jnp.tile(cq_ref[...], (1, reps_q)) + aqs * jnp.tile(sq_ref[...], (1, reps_q)))
    q_ref[...] = (q * scale).astype(BF16)
    k = head_rs(ak, nkv) * (ak * jnp.tile(ck_ref[...], (1, reps_k)) + aks * jnp.tile(sk_ref[...], (1, reps_k)))
    k = k.astype(BF16)
    for g in range(GQA_KV_HEADS):
        k_ref[g] = k[:, g * GQA_HEAD_DIM:(g + 1) * GQA_HEAD_DIM]
    v_ref[...] = lax.dot_general(wvt_ref[...], h, (((1,), (1,)), ((), ())), preferred_element_type=F32).astype(BF16)


def _pre_gqa(stream, mods4, g_mix, w_qkv, g_q, g_k, geom, layer):
    d = stream.shape[1]
    nq = GQA_HEADS * GQA_HEAD_DIM
    nkv = GQA_KV_HEADS * GQA_HEAD_DIM
    partner, _ = _rope_perm(GQA_HEAD_DIM)
    wq = w_qkv[:, :nq].reshape(d, GQA_HEADS, GQA_HEAD_DIM)
    wk = w_qkv[:, nq:nq + nkv].reshape(d, GQA_KV_HEADS, GQA_HEAD_DIM)
    w_ext = jnp.concatenate([w_qkv[:, :nq + nkv], wq[:, :, partner].reshape(d, nq), wk[:, :, partner].reshape(d, nkv)],
                            axis=1).astype(BF16)
    w_vt = w_qkv[:, nq + nkv:].T.astype(BF16)
    cos, sin = _rope_tables(geom.seq, GQA_HEAD_DIM, geom.lat_per_b)
    per = LANES // GQA_HEAD_DIM
    cq = jnp.tile(cos * g_q[None, :], (1, per))
    sq = jnp.tile(sin * g_q[partner][None, :], (1, per))
    ck = jnp.tile(cos * g_k[None, :], (1, per))
    sk = jnp.tile(sin * g_k[partner][None, :], (1, per))
    hid = jnp.arange(nq) // GQA_HEAD_DIM
    bd = (hid[:, None] == hid[None, :]).astype(BF16)
    n_tiles = geom.n_tiles
    full = lambda a: pl.BlockSpec(a.shape, lambda j: (0,) * a.ndim)
    tab = pl.BlockSpec((TT, LANES), lambda j: (geom.pos_tile(j), 0))
    return pl.pallas_call(
        functools.partial(_pre_gqa_kernel, scale=float(GQA_HEAD_DIM) ** -0.5),
        grid=(n_tiles,),
        in_specs=[
            pl.BlockSpec((TT, d), lambda j: (j, 0)),
            _mod_spec(geom, layer, d),
            pl.BlockSpec((1, d), lambda j: (0, 0)),
            full(w_ext), full(w_vt), full(bd), tab, tab, tab, tab,
        ],
        out_specs=[
            pl.BlockSpec((TT, nq), lambda j: (j, 0)),
            pl.BlockSpec((GQA_KV_HEADS, TT, GQA_HEAD_DIM), lambda j: (0, j, 0)),
            pl.BlockSpec((nkv, TT), lambda j: (0, j)),
        ],
        out_shape=[
            jax.ShapeDtypeStruct((n_tiles * TT, nq), BF16),
            jax.ShapeDtypeStruct((GQA_KV_HEADS, n_tiles * TT, GQA_HEAD_DIM), BF16),
            jax.ShapeDtypeStruct((nkv, n_tiles * TT), BF16),
        ],
        compiler_params=_cparams(("parallel",)),
        name="pre_gqa",
    )(stream, mods4, g_mix, w_ext, w_vt, bd, cq, sq, ck, sk)


def _attn_kernel(*refs, n_heads, kv_heads, dk, dv, has_lat):
    if has_lat:
        q_ref, kl_ref, vl_ref, kc_ref, vc_ref, o_ref = refs
    else:
        q_ref, kc_ref, vc_ref, o_ref = refs
    nt = (((1,), (1,)), ((), ()))
    outs = []
    for g in range(n_heads):
        gk = g * kv_heads // n_heads
        q = q_ref[:, g * dk:(g + 1) * dk]
        sc = lax.dot_general(kc_ref[:, gk * dk:(gk + 1) * dk], q, nt, preferred_element_type=F32)
        mx = jnp.max(sc, axis=0, keepdims=True)
        if has_lat:
            sl = lax.dot_general(kl_ref[:, gk * dk:(gk + 1) * dk], q, nt, preferred_element_type=F32)
            mx = jnp.maximum(mx, jnp.max(sl, axis=0, keepdims=True))
        pc = jnp.exp(sc - mx)
        den = jnp.sum(pc, axis=0, keepdims=True)
        o = jnp.dot(vc_ref[gk * dv:(gk + 1) * dv, :], pc.astype(BF16), preferred_element_type=F32)
        if has_lat:
            pl_ = jnp.exp(sl - mx)
            den = den + jnp.sum(pl_, axis=0, keepdims=True)
            o = o + jnp.dot(vl_ref[gk * dv:(gk + 1) * dv, :], pl_.astype(BF16), preferred_element_type=F32)
        outs.append(o / den)
    o_ref[...] = jnp.transpose(jnp.concatenate(outs, axis=0)).astype(o_ref.dtype)


def _attention(q, k, vt, geom, *, groups, n_heads, kv_heads, dk, dv, k_head_major, latent_queries):
    wq, wk, wv = n_heads * dk, kv_heads * dk, kv_heads * dv
    wo = n_heads * dv
    batch = geom.batch
    nc = geom.n_ctx
    ctx_blk0 = geom.n_lat // nc
    assert geom.n_lat % nc == 0

    def kspec(rows, tok_blk):
        if k_head_major:
            return pl.BlockSpec((None, rows, wk), lambda b, h, *_: (h, tok_blk(b), 0))
        return pl.BlockSpec((rows, wk), lambda b, h, *_: (tok_blk(b), h))

    def vspec(rows, tok_blk):
        return pl.BlockSpec((wv, rows), lambda b, h, *_: (h, tok_blk(b)))

    kern = functools.partial(_attn_kernel, n_heads=n_heads, kv_heads=kv_heads, dk=dk, dv=dv, has_lat=latent_queries)
    if latent_queries:
        tq = TT
        qt = geom.seq // tq
        return pl.pallas_call(
            kern,
            grid=(batch, groups, qt),
            in_specs=[
                pl.BlockSpec((tq, wq), lambda b, h, t: (b * qt + t, h)),
                kspec(geom.seq, lambda b: b),
                vspec(geom.seq, lambda b: b),
                kspec(nc, lambda b: ctx_blk0 + b),
                vspec(nc, lambda b: ctx_blk0 + b),
            ],
            out_specs=pl.BlockSpec((tq, wo), lambda b, h, t: (b * qt + t, h)),
            out_shape=jax.ShapeDtypeStruct((geom.n_lat, groups * wo), BF16),
            compiler_params=_cparams(("parallel", "parallel", "arbitrary")),
            name="attn_lat",
        )(q, k, vt, k, vt)
    return pl.pallas_call(
        kern,
        grid=(batch, groups),
        in_specs=[
            pl.BlockSpec((nc, wq), lambda b, h: (ctx_blk0 + b, h)),
            kspec(nc, lambda b: ctx_blk0 + b),
            vspec(nc, lambda b: ctx_blk0 + b),
        ],
        out_specs=pl.BlockSpec((nc, wo), lambda b, h: (b, h)),
        out_shape=jax.ShapeDtypeStruct((batch * nc, groups * wo), BF16),
        compiler_params=_cparams(("parallel", "parallel")),
        name="attn_ctx",
    )(q, k, vt)


def _post_kernel(o_ref, wo_ref, s_ref, mod_ref, g_ref, wr_ref, br_ref, s_out, h2_out, idx_out, w_out, cnt_out, *, n_experts):
    m = mod_ref[...]
    s = s_ref[...] + m[2:3] * jnp.dot(o_ref[...], wo_ref[...], preferred_element_type=F32)
    s_out[...] = s
    h2 = _norm_mod(s, g_ref[...], m[3:4], m[4:5])
    h2_out[...] = h2.astype(BF16)
    logits = jnp.dot(h2, wr_ref[...], precision=HI, preferred_element_type=F32) + br_ref[...]
    lane = lax.broadcasted_iota(I32, logits.shape, 1).astype(F32)
    work = jnp.where(lane < n_experts, logits, -jnp.inf)
    vals, hits = [], []
    for k in range(TOP_K):
        mx = jnp.max(work, axis=-1, keepdims=True)
        sel = jnp.min(jnp.where(work == mx, lane, float(LANES)), axis=-1, keepdims=True)
        hit = lane == sel
        vals.append(mx)
        hits.append(hit)
        work = jnp.where(hit, -jnp.inf, work)
    es = [jnp.exp(v - vals[0]) for v in vals]
    den = es[0] + es[1] + es[2] + es[3]
    w_slab = jnp.zeros(logits.shape, F32)
    for k in range(TOP_K):
        w_slab = jnp.where(lane == k, es[k] / den, w_slab)
    w_out[...] = w_slab
    onehot = jnp.where(hits[0] | hits[1] | hits[2] | hits[3], 1.0, 0.0)
    tr = lax.broadcasted_iota(I32, (TT, TT), 0)
    tc = lax.broadcasted_iota(I32, (TT, TT), 1)
    rank = jnp.dot(jnp.where(tc < tr, 1.0, 0.0).astype(BF16), onehot.astype(BF16), preferred_element_type=F32)
    cnt = jnp.sum(onehot, axis=0, keepdims=True)
    cp = jnp.floor((cnt + (SUBLANES - 1)) * (1.0 / SUBLANES)) * SUBLANES
    er = lax.broadcasted_iota(I32, (LANES, LANES), 0)
    ec = lax.broadcasted_iota(I32, (LANES, LANES), 1)
    seg = jnp.dot(jnp.broadcast_to(cp, (SUBLANES, LANES)).astype(BF16), jnp.where(er < ec, 1.0, 0.0).astype(BF16),
                  preferred_element_type=F32)[0:1]
    base = seg + rank
    lp_slab = jnp.zeros(logits.shape, F32)
    for k in range(TOP_K):
        lp_slab = jnp.where(lane == k, jnp.sum(jnp.where(hits[k], base, 0.0), axis=-1, keepdims=True), lp_slab)
    idx_out[...] = lp_slab.astype(I32)
    cnt_out[...] = jnp.broadcast_to(cnt, (SUBLANES, LANES)).astype(I32)


def _post(o, w_o, stream, mods4, g_ffn, w_router, b_router, geom, layer, n_tiles):
    d = stream.shape[1]
    wo_w = o.shape[1]
    n_experts = w_router.shape[1]
    wr = jnp.concatenate([w_router, jnp.zeros((d, LANES - n_experts), F32)], axis=1)
    br = jnp.concatenate([b_router, jnp.zeros((LANES - n_experts,), F32)]).reshape(1, LANES)
    n = n_tiles * TT
    return pl.pallas_call(
        functools.partial(_post_kernel, n_experts=n_experts),
        grid=(n_tiles,),
        in_specs=[
            pl.BlockSpec((TT, wo_w), lambda j: (j, 0)),
            pl.BlockSpec((wo_w, d), lambda j: (0, 0)),
            pl.BlockSpec((TT, d), lambda j: (j, 0)),
            _mod_spec(geom, layer, d),
            pl.BlockSpec((1, d), lambda j: (0, 0)),
            pl.BlockSpec((d, LANES), lambda j: (0, 0)),
            pl.BlockSpec((1, LANES), lambda j: (0, 0)),
        ],
        out_specs=[
            pl.BlockSpec((TT, d), lambda j: (j, 0)),
            pl.BlockSpec((TT, d), lambda j: (j, 0)),
            pl.BlockSpec((TT, LANES), lambda j: (j, 0)),
            pl.BlockSpec((TT, LANES), lambda j: (j, 0)),
            pl.BlockSpec((None, SUBLANES, LANES), lambda j: (j, 0, 0)),
        ],
        out_shape=[
            jax.ShapeDtypeStruct((n, d), F32),
            jax.ShapeDtypeStruct((n, d), BF16),
            jax.ShapeDtypeStruct((n, LANES), I32),
            jax.ShapeDtypeStruct((n, LANES), F32),
            jax.ShapeDtypeStruct((n_tiles, SUBLANES, LANES), I32),
        ],
        compiler_params=_cparams(("parallel",)),
        name="post",
    )(o, w_o, stream, mods4, g_ffn, wr, br)


def _local_rows(n_experts):
    return _round_up(TT * TOP_K + n_experts * (SUBLANES - 1), LANES)


def _n_blocks(n_tok, n_tiles, n_experts):
    rows = n_tok * TOP_K + n_experts * n_tiles * (SUBLANES - 1) + n_experts * (EXPERT_BLOCK - 1)
    return pl.cdiv(rows, EXPERT_BLOCK)


def _route_meta(cnt, lpos, n_experts):
    n_tiles = cnt.shape[0]
    n_blocks = _n_blocks(n_tiles * TT, n_tiles, n_experts)
    cp = _round_up(cnt, SUBLANES)
    seg = jnp.cumsum(cp, axis=1) - cp
    run_rows = cp.sum(axis=0)
    reg = _round_up(run_rows, EXPERT_BLOCK)
    reg_end = jnp.cumsum(reg)
    reg_start = reg_end - reg
    off = reg_start[None, :] + jnp.cumsum(cp, axis=0) - cp
    n_used = (reg_end[-1] // EXPERT_BLOCK).astype(I32)
    bstart = jnp.arange(n_blocks, dtype=I32) * EXPERT_BLOCK
    last_start = jnp.maximum(n_used - 1, 0) * EXPERT_BLOCK
    be = (jnp.minimum(bstart, last_start)[:, None] >= reg_end[None, :]).sum(axis=1).astype(I32)
    be = jnp.minimum(be, n_experts - 1)
    n_big = (cp // BIG_CHUNK).sum(axis=1)
    n_small = ((cp % BIG_CHUNK) // SUBLANES).sum(axis=1)
    return dict(
        cp=cp.reshape(-1).astype(I32), seg=seg.reshape(-1).astype(I32), off=off.reshape(-1).astype(I32),
        n_big=n_big.astype(I32), n_small=n_small.astype(I32),
        gap_start=(reg_start + run_rows).astype(I32), gap=(reg - run_rows).astype(I32),
        lpos=lpos.astype(I32), block_expert=be, n_used=n_used.reshape(1), n_blocks=n_blocks,
    )


def _run_copies(tile, n_experts, cp_ref, seg_ref, off_ref, make_big, make_small):
    def per_expert(e, carry):
        cp = cp_ref[tile * n_experts + e]
        seg = seg_ref[tile * n_experts + e]
        off = off_ref[tile * n_experts + e]
        nb = cp // BIG_CHUNK

        def big(j, c):
            make_big(pl.multiple_of(seg + j * BIG_CHUNK, SUBLANES), pl.multiple_of(off + j * BIG_CHUNK, SUBLANES)).start()
            return c

        lax.fori_loop(0, nb, big, 0)
        base = nb * BIG_CHUNK

        def small(j, c):
            make_small(pl.multiple_of(seg + base + j * SUBLANES, SUBLANES),
                       pl.multiple_of(off + base + j * SUBLANES, SUBLANES)).start()
            return c

        lax.fori_loop(0, (cp - base) // SUBLANES, small, 0)
        return carry

    lax.fori_loop(0, n_experts, per_expert, 0)


def _wait_copies(tile, nbig_ref, nsmall_ref, make_big, make_small):
    def wb(j, c):
        make_big(0, 0).wait()
        return c

    def ws(j, c):
        make_small(0, 0).wait()
        return c

    lax.fori_loop(0, nbig_ref[tile], wb, 0)
    lax.fori_loop(0, nsmall_ref[tile], ws, 0)


def _dispatch_kernel(cp_ref, seg_ref, off_ref, nbig_ref, nsmall_ref, gs_ref, gap_ref, nu_ref,
                     lpt_ref, h2_ref, xs_ref, buf_ref, zero_ref, sem, zsem, *, n_experts, local_rows, n_blocks):
    i = pl.program_id(0)
    n = pl.num_programs(0)
    slot = i % 2

    def big(src, dst):
        return pltpu.make_async_copy(buf_ref.at[slot, pl.ds(src, BIG_CHUNK), :], xs_ref.at[pl.ds(dst, BIG_CHUNK), :], sem.at[slot])

    def small(src, dst):
        return pltpu.make_async_copy(buf_ref.at[slot, pl.ds(src, SUBLANES), :], xs_ref.at[pl.ds(dst, SUBLANES), :], sem.at[slot])

    @pl.when(i >= 2)
    def _():
        _wait_copies(i - 2, nbig_ref, nsmall_ref, big, small)

    lpt = lpt_ref[...]
    rows = lax.broadcasted_iota(I32, (local_rows, TT), 0)
    hit = rows == lpt[0:1, :]
    for k in range(1, TOP_K):
        hit = hit | (rows == lpt[k:k + 1, :])
    p = jnp.where(hit, 1.0, 0.0).astype(BF16)
    buf_ref[slot] = jnp.dot(p, h2_ref[...], preferred_element_type=F32)
    _run_copies(i, n_experts, cp_ref, seg_ref, off_ref, big, small)

    @pl.when(i == n - 1)
    def _():
        zero_ref[...] = jnp.zeros_like(zero_ref)
        sizes = []
        size = EXPERT_BLOCK // 2
        while size >= SUBLANES:
            sizes.append(size)
            size //= 2

        def zcopy(dst, size):
            return pltpu.make_async_copy(zero_ref.at[pl.ds(0, size), :], xs_ref.at[pl.ds(dst, size), :], zsem)

        def per_expert(e, carry):
            gap = gap_ref[e]
            pos = gs_ref[e]
            for size in sizes:
                take = (gap & size) != 0

                @pl.when(take)
                def _():
                    zcopy(pl.multiple_of(pos, SUBLANES), size).start()

                pos = pos + jnp.where(take, size, 0)
            return carry

        lax.fori_loop(0, n_experts, per_expert, 0)

        zrows = zero_ref.shape[0]
        per_block = EXPERT_BLOCK // zrows
        tail_copies = (n_blocks - nu_ref[0]) * per_block

        def tail(t, carry):
            zcopy(pl.multiple_of(nu_ref[0] * EXPERT_BLOCK + t * zrows, SUBLANES), zrows).start()
            return carry

        lax.fori_loop(0, tail_copies, tail, 0)

        def per_expert_wait(e, carry):
            gap = gap_ref[e]
            for size in sizes:
                @pl.when((gap & size) != 0)
                def _():
                    zcopy(0, size).wait()
            return carry

        lax.fori_loop(0, n_experts, per_expert_wait, 0)

        def tail_wait(t, carry):
            zcopy(0, zrows).wait()
            return carry

        lax.fori_loop(0, tail_copies, tail_wait, 0)

        @pl.when(i >= 1)
        def _():
            def big_o(src, dst):
                return pltpu.make_async_copy(buf_ref.at[1 - slot, pl.ds(src, BIG_CHUNK), :], xs_ref.at[pl.ds(dst, BIG_CHUNK), :], sem.at[1 - slot])

            def small_o(src, dst):
                return pltpu.make_async_copy(buf_ref.at[1 - slot, pl.ds(src, SUBLANES), :], xs_ref.at[pl.ds(dst, SUBLANES), :], sem.at[1 - slot])

            _wait_copies(i - 1, nbig_ref, nsmall_ref, big_o, small_o)

        _wait_copies(i, nbig_ref, nsmall_ref, big, small)


def _dispatch(h2, meta, n_tiles, n_experts):
    n, d = h2.shape
    local_rows = _local_rows(n_experts)
    n_rows = meta["n_blocks"] * EXPERT_BLOCK
    lpt = meta["lpos"].reshape(n_tiles, TT, TOP_K).transpose(0, 2, 1)
    grid_spec = pltpu.PrefetchScalarGridSpec(
        num_scalar_prefetch=8,
        grid=(n_tiles,),
        in_specs=[
            pl.BlockSpec((None, TOP_K, TT), lambda j, *_: (j, 0, 0)),
            pl.BlockSpec((TT, d), lambda j, *_: (j, 0)),
        ],
        out_specs=pl.BlockSpec(memory_space=pl.ANY),
        scratch_shapes=[
            pltpu.VMEM((2, local_rows, d), F32),
            pltpu.VMEM((EXPERT_BLOCK // 2, d), F32),
            pltpu.SemaphoreType.DMA((2,)),
            pltpu.SemaphoreType.DMA(()),
        ],
    )
    return pl.pallas_call(
        functools.partial(_dispatch_kernel, n_experts=n_experts, local_rows=local_rows, n_blocks=meta["n_blocks"]),
        grid_spec=grid_spec,
        out_shape=jax.ShapeDtypeStruct((n_rows, d), F32),
        compiler_params=_cparams(("arbitrary",)),
        name="moe_dispatch",
    )(meta["cp"], meta["seg"], meta["off"], meta["n_big"], meta["n_small"], meta["gap_start"], meta["gap"], meta["n_used"],
      lpt, h2)


def _moe_prep_kernel(wgu_ref, wd_ref, sel_ref, wg_out, wu_out, wd_out):
    sel = sel_ref[...]
    pair = 2 * LANES
    for c in range(wgu_ref.shape[1] // pair):
        chunk = wgu_ref[:, c * pair:(c + 1) * pair].astype(BF16)
        de = jnp.dot(chunk, sel, preferred_element_type=F32)
        wg_out[:, c * LANES:(c + 1) * LANES] = de[:, :LANES].astype(BF16)
        wu_out[:, c * LANES:(c + 1) * LANES] = de[:, LANES:].astype(BF16)
    wd_out[...] = wd_ref[...].astype(BF16)


def _moe_prep(w_gu, w_down):
    depth, n_experts, d, f2 = w_gu.shape
    f = f2 // 2
    r = jnp.arange(2 * LANES)
    src = jnp.where(r < LANES, 2 * r, 2 * (r - LANES) + 1)
    sel = (r[:, None] == src[None, :]).astype(BF16)
    wspec = lambda rows, cols: pl.BlockSpec((None, None, rows, cols), lambda i, e: (i, e, 0, 0))
    return pl.pallas_call(
        _moe_prep_kernel,
        grid=(depth, n_experts),
        in_specs=[wspec(d, f2), wspec(f, d), pl.BlockSpec((2 * LANES, 2 * LANES), lambda i, e: (0, 0))],
        out_specs=[wspec(d, f), wspec(d, f), wspec(f, d)],
        out_shape=[
            jax.ShapeDtypeStruct((depth, n_experts, d, f), BF16),
            jax.ShapeDtypeStruct((depth, n_experts, d, f), BF16),
            jax.ShapeDtypeStruct((depth, n_experts, f, d), BF16),
        ],
        compiler_params=_cparams(("parallel", "parallel")),
        name="moe_prep",
    )(w_gu, w_down, sel)


def _ffn_kernel(be_ref, nu_ref, x_ref, wg_ref, wu_ref, wd_ref, bg_ref, bu_ref, bd_ref, y_ref):
    b = pl.program_id(0)

    @pl.when(b < nu_ref[0])
    def _():
        x = x_ref[...].astype(BF16)
        gate = jnp.dot(x, wg_ref[...], preferred_element_type=F32) + bg_ref[...]
        up = jnp.dot(x, wu_ref[...], preferred_element_type=F32) + bu_ref[...]
        gate = jnp.minimum(gate, SWIGLU_LIMIT)
        up = jnp.clip(up, -SWIGLU_LIMIT, SWIGLU_LIMIT)
        glu = gate / (1.0 + jnp.exp(-SWIGLU_ALPHA * gate))
        act = ((up + 1.0) * glu).astype(BF16)
        y_ref[...] = jnp.dot(act, wd_ref[...], preferred_element_type=F32) + bd_ref[...]

    @pl.when(b >= nu_ref[0])
    def _():
        y_ref[...] = jnp.zeros_like(y_ref)


def _ffn(xs, meta, layer, w_gate, w_up, w_down, b_gate, b_up, b_down):
    n_rows, d = xs.shape
    n_blocks = meta["n_blocks"]
    f = w_gate.shape[3]
    xmap = lambda b, be, nu: (jnp.minimum(b, jnp.maximum(nu[0] - 1, 0)), 0)
    wmap = lambda b, be, nu: (layer, be[b], 0, 0)
    grid_spec = pltpu.PrefetchScalarGridSpec(
        num_scalar_prefetch=2,
        grid=(n_blocks,),
        in_specs=[
            pl.BlockSpec((EXPERT_BLOCK, d), xmap),
            pl.BlockSpec((None, None, d, f), wmap),
            pl.BlockSpec((None, None, d, f), wmap),
            pl.BlockSpec((None, None, f, d), wmap),
            pl.BlockSpec((None, None, 1, f), wmap),
            pl.BlockSpec((None, None, 1, f), wmap),
            pl.BlockSpec((None, None, 1, d), wmap),
        ],
        out_specs=pl.BlockSpec((EXPERT_BLOCK, d), lambda b, be, nu: (b, 0)),
    )
    return pl.pallas_call(
        _ffn_kernel,
        grid_spec=grid_spec,
        out_shape=jax.ShapeDtypeStruct((n_rows, d), F32),
        compiler_params=_cparams(("arbitrary",)),
        name="moe_ffn",
    )(meta["block_expert"], meta["n_used"], xs, w_gate, w_up, w_down, b_gate, b_up, b_down)


def _combine_kernel(cp_ref, seg_ref, off_ref, nbig_ref, nsmall_ref,
                    lp_ref, w_ref, s_ref, mod_ref, gf_ref, ys_ref, o_ref, buf_ref, sem, *, n_experts, local_rows, final):
    i = pl.program_id(0)
    n = pl.num_programs(0)
    slot = i % 2

    def copies(sl):
        def big(dst, src):
            return pltpu.make_async_copy(ys_ref.at[pl.ds(src, BIG_CHUNK), :], buf_ref.at[sl, pl.ds(dst, BIG_CHUNK), :], sem.at[sl])

        def small(dst, src):
            return pltpu.make_async_copy(ys_ref.at[pl.ds(src, SUBLANES), :], buf_ref.at[sl, pl.ds(dst, SUBLANES), :], sem.at[sl])

        return big, small

    @pl.when(i == 0)
    def _():
        buf_ref[...] = jnp.zeros_like(buf_ref)
        _run_copies(0, n_experts, cp_ref, seg_ref, off_ref, *copies(0))

    @pl.when(i + 1 < n)
    def _():
        _run_copies(i + 1, n_experts, cp_ref, seg_ref, off_ref, *copies(1 - slot))

    _wait_copies(i, nbig_ref, nsmall_ref, *copies(slot))

    lp = lp_ref[...]
    w = w_ref[...]
    lanes = lax.broadcasted_iota(I32, (TT, local_rows), 1)
    pw = jnp.zeros((TT, local_rows), F32)
    for k in range(TOP_K):
        pw = pw + jnp.where(lanes == lp[:, k:k + 1], w[:, k:k + 1], 0.0)
    y = buf_ref[slot].astype(BF16)
    f = jnp.dot(pw.astype(BF16), y, preferred_element_type=F32)
    m = mod_ref[...]
    s = s_ref[...] + m[5:6] * f
    if final:
        s = _rms(s) * gf_ref[...]
    o_ref[...] = s


def _combine(ys, meta, top_w, stream, mods4, g_final, geom, layer, n_tiles, n_experts, final):
    d = stream.shape[1]
    local_rows = _local_rows(n_experts)
    grid_spec = pltpu.PrefetchScalarGridSpec(
        num_scalar_prefetch=5,
        grid=(n_tiles,),
        in_specs=[
            pl.BlockSpec((TT, TOP_K), lambda j, *_: (j, 0)),
            pl.BlockSpec((TT, TOP_K), lambda j, *_: (j, 0)),
            pl.BlockSpec((TT, d), lambda j, *_: (j, 0)),
            pl.BlockSpec((None, None, N_MOD, d), lambda j, *_: (layer, geom.group(j), 0, 0)),
            pl.BlockSpec((1, d), lambda j, *_: (0, 0)),
            pl.BlockSpec(memory_space=pl.ANY),
        ],
        out_specs=pl.BlockSpec((TT, d), lambda j, *_: (j, 0)),
        scratch_shapes=[
            pltpu.VMEM((2, local_rows, d), F32),
            pltpu.SemaphoreType.DMA((2,)),
        ],
    )
    return pl.pallas_call(
        functools.partial(_combine_kernel, n_experts=n_experts, local_rows=local_rows, final=final),
        grid_spec=grid_spec,
        out_shape=jax.ShapeDtypeStruct((n_tiles * TT, d), F32),
        compiler_params=_cparams(("arbitrary",)),
        name="moe_combine",
    )(meta["cp"], meta["seg"], meta["off"], meta["n_big"], meta["n_small"],
      meta["lpos"], top_w, stream, mods4, g_final, ys)


def kernel(x, c, ctx, c_ctx, w_mod, b_mod, g_mix, g_ffn, g_final, f_w_in, f_w_out, mla_w_in, mla_g_qa, mla_w_qb,
           mla_g_kva, mla_w_kvb, mla_w_o, gqa_w_qkv, gqa_g_q, gqa_g_k, gqa_w_o, moe_w_router, moe_b_router,
           moe_w_gu, moe_b_gu, moe_w_down, moe_b_down):
    batch, seq, d = x.shape
    n_ctx = ctx.shape[1]
    depth = w_mod.shape[0]
    n_experts = moe_w_router.shape[2]
    geom = _Geom(batch, seq, n_ctx)

    n_groups = _round_up(batch + 1, SUBLANES)
    cc = jnp.concatenate([c, c_ctx[None, :], jnp.zeros((n_groups - batch - 1, d), F32)], axis=0)
    mods4 = _mods(cc, w_mod, b_mod).reshape(depth, n_groups, N_MOD, d)

    stream = jnp.concatenate([x.reshape(batch * seq, d), ctx.reshape(batch * n_ctx, d)], axis=0)

    gd = d // FOURIER_GROUPS
    gcos, gsin = _dft_tables(gd)
    eye = jnp.eye(FOURIER_GROUPS, dtype=F32)
    bd_cs = jnp.concatenate([jnp.kron(eye, gcos), jnp.kron(eye, gsin)], axis=1)

    f = moe_w_down.shape[2]
    w_gate, w_up, w_down = _moe_prep(moe_w_gu, moe_w_down)
    b_gate = moe_b_gu[:, :, 0::2].reshape(depth, n_experts, 1, f)
    b_up = moe_b_gu[:, :, 1::2].reshape(depth, n_experts, 1, f)
    b_down = moe_b_down.reshape(depth, n_experts, 1, d)

    for i in range(depth):
        kind, j = i % N_MIXERS, i // N_MIXERS
        last = i == depth - 1
        ctx_used = (kind != 0) or (not last)
        n_tiles = geom.n_tiles if ctx_used else geom.n_lat_tiles
        gm = g_mix[i].reshape(1, d)
        gf = g_ffn[i].reshape(1, d)

        if kind == 0:
            wcs = _fold(f_w_in[j], bd_cs, BF16)
            u = _pre_fourier(stream, mods4, gm, wcs, geom, i, n_tiles)
            o = _dft(u, 0, seq, batch, d)
            if ctx_used:
                o = jnp.concatenate([o, _dft(u, geom.n_lat, n_ctx, batch, d)], axis=0)
            w_o = f_w_out[j].astype(BF16)
        elif kind == 1:
            q, k, v = _pre_mla(stream, mods4, gm, mla_w_in[j], mla_g_qa[j], mla_w_qb[j], mla_g_kva[j], mla_w_kvb[j], geom, i)
            att = functools.partial(_attention, q, k, v, geom, groups=4, n_heads=MLA_HEADS // 4, kv_heads=MLA_HEADS // 4,
                                    dk=MLA_SLOT, dv=MLA_V, k_head_major=False)
            o = att(latent_queries=True)
            if not last:
                o = jnp.concatenate([o, att(latent_queries=False)], axis=0)
            w_o = mla_w_o[j].astype(BF16)
        else:
            q, k, v = _pre_gqa(stream, mods4, gm, gqa_w_qkv[j], gqa_g_q[j], gqa_g_k[j], geom, i)
            att = functools.partial(_attention, q, k, v, geom, groups=GQA_KV_HEADS, n_heads=GQA_HEADS // GQA_KV_HEADS,
                                    kv_heads=1, dk=GQA_HEAD_DIM, dv=GQA_HEAD_DIM, k_head_major=True)
            o = att(latent_queries=True)
            if not last:
                o = jnp.concatenate([o, att(latent_queries=False)], axis=0)
            w_o = gqa_w_o[j].astype(BF16)

        n_moe_tiles = geom.n_lat_tiles if last else geom.n_tiles
        stream, h2, lp_slab, w_slab, cnt_slab = _post(o, w_o, stream, mods4, gf, moe_w_router[i], moe_b_router[i], geom, i,
                                                      n_moe_tiles)
        meta = _route_meta(cnt_slab[:, 0, :n_experts], lp_slab[:, :TOP_K], n_experts)
        xs = _dispatch(h2, meta, n_moe_tiles, n_experts)
        ys = _ffn(xs, meta, i, w_gate, w_up, w_down, b_gate, b_up, b_down)
        stream = _combine(ys, meta, w_slab[:, :TOP_K], stream, mods4, g_final.reshape(1, d), geom, i, n_moe_tiles, n_experts,
                          last)

    return stream[:batch * seq].reshape(batch, seq, d)
```

```python
import functools

import jax
import jax.numpy as jnp
from jax import lax
from jax.experimental import pallas as pl
from jax.experimental.pallas import tpu as pltpu

F32 = jnp.float32
BF16 = jnp.bfloat16
I32 = jnp.int32
HI = lax.Precision.HIGHEST

GRID_W = 64
N_MIXERS = 3
NORM_EPS = 1e-6
ROPE_THETA = 10000.0
FOURIER_GROUPS = 8
MLA_HEADS = 16
MLA_Q_LORA = 384
MLA_KV_LORA = 256
MLA_NOPE = 64
MLA_ROPE = 32
MLA_V = 64
GQA_HEADS = 16
GQA_KV_HEADS = 4
GQA_HEAD_DIM = 64
TOP_K = 4
SWIGLU_LIMIT = 7.0
SWIGLU_ALPHA = 1.702
N_MOD = 6

LANES = 128
SUBLANES = 8
TT = 256
KEY_CHUNK = 1024
LOG2_E = 1.4426950408889634
EXPERT_BLOCK = 512
BIG_CHUNK = 32
LIST_SHIFT = 11
LIST_RADIX = 1 << LIST_SHIFT
MLA_SLOT = 128
VMEM_LIMIT = 56 * 1024 * 1024


def _cparams(sem):
    return pltpu.CompilerParams(dimension_semantics=sem, vmem_limit_bytes=VMEM_LIMIT)


def _round_up(x, m):
    return (x + m - 1) // m * m


def _norm_mod(x, g, shift, scale):
    y = x * lax.rsqrt(jnp.mean(x * x, axis=-1, keepdims=True) + NORM_EPS) * g
    return y * (1.0 + scale) + shift


def _rms(x):
    return x * lax.rsqrt(jnp.mean(x * x, axis=-1, keepdims=True) + NORM_EPS)


def _mod_kernel(cc_ref, w_ref, b_ref, o_ref):
    cc = cc_ref[...]
    a = cc / (1.0 + jnp.exp(-cc))
    o_ref[...] = jnp.dot(a, w_ref[...], precision=HI, preferred_element_type=F32) + b_ref[...]


def _mods(cc, w_mod, b_mod):
    depth, d, n6 = w_mod.shape
    g = cc.shape[0]
    tn = 1536 if n6 % 1536 == 0 else n6
    return pl.pallas_call(
        _mod_kernel,
        grid=(depth, n6 // tn),
        in_specs=[
            pl.BlockSpec((g, d), lambda i, n: (0, 0)),
            pl.BlockSpec((None, d, tn), lambda i, n: (i, 0, n)),
            pl.BlockSpec((None, 1, tn), lambda i, n: (i, 0, n)),
        ],
        out_specs=pl.BlockSpec((None, g, tn), lambda i, n: (i, 0, n)),
        out_shape=jax.ShapeDtypeStruct((depth, g, n6), F32),
        compiler_params=_cparams(("parallel", "parallel")),
        name="mods",
    )(cc, w_mod, b_mod.reshape(depth, 1, n6))


def _fold_kernel(a_ref, b_ref, o_ref):
    o_ref[...] = jnp.dot(a_ref[...], b_ref[...], precision=HI, preferred_element_type=F32).astype(o_ref.dtype)


def _fold(a, b, out_dtype):
    m, k = a.shape
    n = b.shape[1]
    tn = 512
    return pl.pallas_call(
        _fold_kernel,
        grid=(n // tn,),
        in_specs=[pl.BlockSpec((m, k), lambda j: (0, 0)), pl.BlockSpec((k, tn), lambda j: (0, j))],
        out_specs=pl.BlockSpec((m, tn), lambda j: (0, j)),
        out_shape=jax.ShapeDtypeStruct((m, n), out_dtype),
        compiler_params=_cparams(("parallel",)),
        name="fold",
    )(a, b)


class _Geom:
    def __init__(self, batch, seq, n_ctx):
        self.batch, self.seq, self.n_ctx = batch, seq, n_ctx
        assert seq % TT == 0 and n_ctx % TT == 0 and seq % GRID_W == 0
        self.lat_per_b = seq // TT
        self.ctx_per_b = n_ctx // TT
        self.n_lat_tiles = batch * self.lat_per_b
        self.n_ctx_tiles = batch * self.ctx_per_b
        self.n_tiles = self.n_lat_tiles + self.n_ctx_tiles
        self.n_lat = batch * seq
        self.n_tok = self.n_lat + batch * n_ctx

    def group(self, j):
        return jnp.where(j < self.n_lat_tiles, j // self.lat_per_b, self.batch)

    def pos_tile(self, j):
        return jnp.where(j < self.n_lat_tiles, j % self.lat_per_b, self.lat_per_b)


def _mod_spec(geom, layer, d):
    return pl.BlockSpec((None, None, N_MOD, d), lambda j: (layer, geom.group(j), 0, 0))


def _pre_fourier_kernel(s_ref, mod_ref, g_ref, w_ref, u_ref):
    m = mod_ref[...]
    h = _norm_mod(s_ref[...], g_ref[...], m[0:1], m[1:2])
    u_ref[...] = jnp.dot(h.astype(BF16), w_ref[...], preferred_element_type=F32).astype(BF16)


def _pre_fourier(stream, mods4, g_mix, wcs, geom, layer, n_tiles):
    d = stream.shape[1]
    return pl.pallas_call(
        _pre_fourier_kernel,
        grid=(n_tiles,),
        in_specs=[
            pl.BlockSpec((TT, d), lambda j: (j, 0)),
            _mod_spec(geom, layer, d),
            pl.BlockSpec((1, d), lambda j: (0, 0)),
            pl.BlockSpec((d, 2 * d), lambda j: (0, 0)),
        ],
        out_specs=pl.BlockSpec((TT, 2 * d), lambda j: (j, 0)),
        out_shape=jax.ShapeDtypeStruct((n_tiles * TT, 2 * d), BF16),
        compiler_params=_cparams(("parallel",)),
        name="pre_fourier",
    )(stream, mods4, g_mix, wcs)


def _dft_kernel(c_ref, s_ref, u_ref, o_ref, acc_ref, *, d):
    k = pl.program_id(2)

    @pl.when(k == 0)
    def _():
        acc_ref[...] = jnp.zeros_like(acc_ref)

    acc_ref[...] += (jnp.dot(c_ref[...], u_ref[:, :d], preferred_element_type=F32)
                     - jnp.dot(s_ref[...], u_ref[:, d:], preferred_element_type=F32))

    @pl.when(k == pl.num_programs(2) - 1)
    def _():
        o_ref[...] = acc_ref[...].astype(o_ref.dtype)


def _dft_tables(n):
    k = lax.broadcasted_iota(I32, (n, n), 0)
    m = lax.broadcasted_iota(I32, (n, n), 1)
    ang = ((k * m) % n).astype(F32) * (2.0 * jnp.pi / n)
    scale = n ** -0.5
    return (jnp.cos(ang) * scale), (jnp.sin(ang) * scale)


def _dft(u, row_off, seq, batch, d):
    cos, sin = _dft_tables(seq)
    cos, sin = cos.astype(BF16), sin.astype(BF16)
    tm = min(seq, 1024)
    tk = min(seq, 512)
    assert row_off % tk == 0
    kb = seq // tk
    return pl.pallas_call(
        functools.partial(_dft_kernel, d=d),
        grid=(batch, seq // tm, kb),
        in_specs=[
            pl.BlockSpec((tm, tk), lambda b, m, k: (m, k)),
            pl.BlockSpec((tm, tk), lambda b, m, k: (m, k)),
            pl.BlockSpec((tk, 2 * d), lambda b, m, k: (row_off // tk + b * kb + k, 0)),
        ],
        out_specs=pl.BlockSpec((tm, d), lambda b, m, k: (b * (seq // tm) + m, 0)),
        out_shape=jax.ShapeDtypeStruct((batch * seq, d), BF16),
        scratch_shapes=[pltpu.VMEM((tm, d), F32)],
        compiler_params=_cparams(("parallel", "parallel", "arbitrary")),
        name="dft",
    )(cos, sin, u)


def _pre_mla_kernel(s_ref, mod_ref, g_ref, win_ref, gqa_ref, gkva_ref, wqb_ref, wqbs_ref, wkvk_ref, wkvv_ref,
                    cos_ref, sin_ref, q_ref, k_ref, v_ref, *, scale):
    m = mod_ref[...]
    h = _norm_mod(s_ref[...], g_ref[...], m[0:1], m[1:2])
    a = jnp.dot(h.astype(BF16), win_ref[...], preferred_element_type=F32)
    aq = (_rms(a[:, :MLA_Q_LORA]) * gqa_ref[...]).astype(BF16)
    ckv = (_rms(a[:, MLA_Q_LORA:MLA_Q_LORA + MLA_KV_LORA]) * gkva_ref[...]).astype(BF16)
    o = MLA_Q_LORA + MLA_KV_LORA
    cos = cos_ref[...]
    sin = sin_ref[...]
    kpe = a[:, o:o + MLA_SLOT] * cos + a[:, o + MLA_SLOT:o + 2 * MLA_SLOT] * sin
    cos_h = jnp.tile(cos, (1, MLA_HEADS))
    sin_h = jnp.tile(sin, (1, MLA_HEADS))
    q = jnp.dot(aq, wqb_ref[...], preferred_element_type=F32)
    qs = jnp.dot(aq, wqbs_ref[...], preferred_element_type=F32)
    q_ref[...] = ((q * cos_h + qs * sin_h) * scale).astype(BF16)
    kk = jnp.dot(ckv, wkvk_ref[...], preferred_element_type=F32)
    k_ref[...] = (kk + jnp.tile(kpe, (1, MLA_HEADS))).astype(BF16)
    v_ref[...] = lax.dot_general(wkvv_ref[...], ckv, (((1,), (1,)), ((), ())), preferred_element_type=F32).astype(BF16)


def _rope_perm(rot):
    sec = rot // 2
    half = sec // 2
    d = jnp.arange(rot)
    first = (d % sec) < half
    partner = jnp.where(first, d + half, d - half)
    sign = jnp.where(first, -1.0, 1.0).astype(F32)
    return partner, sign


def _rope_tables(seq, rot, lat_per_b):
    rows = seq // GRID_W
    row = jnp.repeat(jnp.arange(rows, dtype=F32), GRID_W)
    col = jnp.tile(jnp.arange(GRID_W, dtype=F32), rows)
    n_freq = rot // 4
    inv_freq = ROPE_THETA ** (-jnp.arange(n_freq, dtype=F32) / n_freq)
    ang = jnp.stack([row[:, None] * inv_freq, col[:, None] * inv_freq], axis=1)
    cos = jnp.cos(ang)
    sin = jnp.sin(ang)
    cos_full = jnp.concatenate([cos, cos], axis=-1).reshape(seq, rot)
    sin_full = jnp.concatenate([sin, sin], axis=-1).reshape(seq, rot)
    _, sign = _rope_perm(rot)
    sin_full = sin_full * sign
    ident_c = jnp.ones((TT, rot), F32)
    ident_s = jnp.zeros((TT, rot), F32)
    assert seq == lat_per_b * TT
    return jnp.concatenate([cos_full, ident_c], 0), jnp.concatenate([sin_full, ident_s], 0)


def _pre_mla(stream, mods4, g_mix, w_in, g_qa, w_qb, g_kva, w_kvb, geom, layer):
    d = stream.shape[1]
    hd = MLA_NOPE + MLA_ROPE
    partner, _ = _rope_perm(MLA_ROPE)
    o = MLA_Q_LORA + MLA_KV_LORA
    kpe_w = w_in[:, o:]
    z_lo = jnp.zeros((d, MLA_NOPE), F32)
    z_hi = jnp.zeros((d, MLA_SLOT - hd), F32)
    w_in_ext = jnp.concatenate([w_in[:, :o], z_lo, kpe_w, z_hi, z_lo, kpe_w[:, partner], z_hi], axis=1).astype(BF16)
    wq = w_qb.reshape(MLA_Q_LORA, MLA_HEADS, hd)
    zq = jnp.zeros((MLA_Q_LORA, MLA_HEADS, MLA_SLOT - hd), F32)
    wq_p = jnp.concatenate([wq, zq], axis=2).reshape(MLA_Q_LORA, MLA_HEADS * MLA_SLOT).astype(BF16)
    wq_s = jnp.concatenate([jnp.zeros_like(wq[:, :, :MLA_NOPE]), wq[:, :, MLA_NOPE:][:, :, partner], zq], axis=2)
    wq_s = wq_s.reshape(MLA_Q_LORA, MLA_HEADS * MLA_SLOT).astype(BF16)
    wkv = w_kvb.reshape(MLA_KV_LORA, MLA_HEADS, MLA_NOPE + MLA_V)
    zk = jnp.zeros((MLA_KV_LORA, MLA_HEADS, MLA_SLOT - MLA_NOPE), F32)
    wkv_k = jnp.concatenate([wkv[:, :, :MLA_NOPE], zk], axis=2).reshape(MLA_KV_LORA, MLA_HEADS * MLA_SLOT).astype(BF16)
    wkv_v = wkv[:, :, MLA_NOPE:].reshape(MLA_KV_LORA, MLA_HEADS * MLA_V).T.astype(BF16)
    cos, sin = _rope_tables(geom.seq, MLA_ROPE, geom.lat_per_b)
    rows = cos.shape[0]
    cos_slot = jnp.concatenate([jnp.ones((rows, MLA_NOPE), F32), cos, jnp.ones((rows, MLA_SLOT - hd), F32)], axis=1)
    sin_slot = jnp.concatenate([jnp.zeros((rows, MLA_NOPE), F32), sin, jnp.zeros((rows, MLA_SLOT - hd), F32)], axis=1)
    n_tiles = geom.n_tiles
    wq_w = MLA_HEADS * MLA_SLOT
    wv_w = MLA_HEADS * MLA_V
    full = lambda a: pl.BlockSpec(a.shape, lambda j: (0,) * a.ndim)
    g_qa2, g_kva2 = g_qa.reshape(1, -1), g_kva.reshape(1, -1)
    return pl.pallas_call(
        functools.partial(_pre_mla_kernel, scale=float(hd) ** -0.5 * LOG2_E),
        grid=(n_tiles,),
        in_specs=[
            pl.BlockSpec((TT, d), lambda j: (j, 0)),
            _mod_spec(geom, layer, d),
            pl.BlockSpec((1, d), lambda j: (0, 0)),
            full(w_in_ext), full(g_qa2), full(g_kva2), full(wq_p), full(wq_s), full(wkv_k), full(wkv_v),
            pl.BlockSpec((TT, MLA_SLOT), lambda j: (geom.pos_tile(j), 0)),
            pl.BlockSpec((TT, MLA_SLOT), lambda j: (geom.pos_tile(j), 0)),
        ],
        out_specs=[
            pl.BlockSpec((TT, wq_w), lambda j: (j, 0)),
            pl.BlockSpec((TT, wq_w), lambda j: (j, 0)),
            pl.BlockSpec((wv_w, TT), lambda j: (0, j)),
        ],
        out_shape=[
            jax.ShapeDtypeStruct((n_tiles * TT, wq_w), BF16),
            jax.ShapeDtypeStruct((n_tiles * TT, wq_w), BF16),
            jax.ShapeDtypeStruct((wv_w, n_tiles * TT), BF16),
        ],
        compiler_params=_cparams(("parallel",)),
        name="pre_mla",
    )(stream, mods4, g_mix, w_in_ext, g_qa2, g_kva2, wq_p, wq_s, wkv_k, wkv_v, cos_slot, sin_slot)


def _split_hi_lo(x):
    hi = x.astype(BF16)
    lo = (x - hi.astype(F32)).astype(BF16)
    return hi, lo


def _pre_gqa_kernel(s_ref, mod_ref, g_ref, w_ref, wvt_ref, bd_ref, cq_ref, sq_ref, ck_ref, sk_ref, q_ref, k_ref, v_ref, *,
                    scale):
    m = mod_ref[...]
    h = _norm_mod(s_ref[...], g_ref[...], m[0:1], m[1:2]).astype(BF16)
    a = jnp.dot(h, w_ref[...], preferred_element_type=F32)
    nq = GQA_HEADS * GQA_HEAD_DIM
    nkv = GQA_KV_HEADS * GQA_HEAD_DIM
    aq, ak = a[:, :nq], a[:, nq:nq + nkv]
    aqs, aks = a[:, nq + nkv:2 * nq + nkv], a[:, 2 * nq + nkv:]
    bd = bd_ref[...]

    def head_rs(x, width):
        hi, lo = _split_hi_lo(x * x)
        b = bd[:width, :width]
        ssq = jnp.dot(hi, b, preferred_element_type=F32) + jnp.dot(lo, b, preferred_element_type=F32)
        return lax.rsqrt(ssq * (1.0 / GQA_HEAD_DIM) + NORM_EPS)

    reps_q = nq // cq_ref.shape[1]
    reps_k = nkv // ck_ref.shape[1]
    q = head_rs(aq, nq) * (aq * jnp.tile(cq_ref[...], (1, reps_q)) + aqs * jnp.tile(sq_ref[...], (1, reps_q)))
    q_ref[...] = (q * scale).astype(BF16)
    k = head_rs(ak, nkv) * (ak * jnp.tile(ck_ref[...], (1, reps_k)) + aks * jnp.tile(sk_ref[...], (1, reps_k)))
    k = k.astype(BF16)
    for g in range(GQA_KV_HEADS):
        k_ref[g] = k[:, g * GQA_HEAD_DIM:(g + 1) * GQA_HEAD_DIM]
    v_ref[...] = lax.dot_general(wvt_ref[...], h, (((1,), (1,)), ((), ())), preferred_element_type=F32).astype(BF16)


def _pre_gqa(stream, mods4, g_mix, w_qkv, g_q, g_k, geom, layer):
    d = stream.shape[1]
    nq = GQA_HEADS * GQA_HEAD_DIM
    nkv = GQA_KV_HEADS * GQA_HEAD_DIM
    partner, _ = _rope_perm(GQA_HEAD_DIM)
    wq = w_qkv[:, :nq].reshape(d, GQA_HEADS, GQA_HEAD_DIM)
    wk = w_qkv[:, nq:nq + nkv].reshape(d, GQA_KV_HEADS, GQA_HEAD_DIM)
    w_ext = jnp.concatenate([w_qkv[:, :nq + nkv], wq[:, :, partner].reshape(d, nq), wk[:, :, partner].reshape(d, nkv)],
                            axis=1).astype(BF16)
    w_vt = w_qkv[:, nq + nkv:].T.astype(BF16)
    cos, sin = _rope_tables(geom.seq, GQA_HEAD_DIM, geom.lat_per_b)
    per = LANES // GQA_HEAD_DIM
    cq = jnp.tile(cos * g_q[None, :], (1, per))
    sq = jnp.tile(sin * g_q[partner][None, :], (1, per))
    ck = jnp.tile(cos * g_k[None, :], (1, per))
    sk = jnp.tile(sin * g_k[partner][None, :], (1, per))
    hid = jnp.arange(nq) // GQA_HEAD_DIM
    bd = (hid[:, None] == hid[None, :]).astype(BF16)
    n_tiles = geom.n_tiles
    full = lambda a: pl.BlockSpec(a.shape, lambda j: (0,) * a.ndim)
    tab = pl.BlockSpec((TT, LANES), lambda j: (geom.pos_tile(j), 0))
    return pl.pallas_call(
        functools.partial(_pre_gqa_kernel, scale=float(GQA_HEAD_DIM) ** -0.5 * LOG2_E),
        grid=(n_tiles,),
        in_specs=[
            pl.BlockSpec((TT, d), lambda j: (j, 0)),
            _mod_spec(geom, layer, d),
            pl.BlockSpec((1, d), lambda j: (0, 0)),
            full(w_ext), full(w_vt), full(bd), tab, tab, tab, tab,
        ],
        out_specs=[
            pl.BlockSpec((TT, nq), lambda j: (j, 0)),
            pl.BlockSpec((GQA_KV_HEADS, TT, GQA_HEAD_DIM), lambda j: (0, j, 0)),
            pl.BlockSpec((nkv, TT), lambda j: (0, j)),
        ],
        out_shape=[
            jax.ShapeDtypeStruct((n_tiles * TT, nq), BF16),
            jax.ShapeDtypeStruct((GQA_KV_HEADS, n_tiles * TT, GQA_HEAD_DIM), BF16),
            jax.ShapeDtypeStruct((nkv, n_tiles * TT), BF16),
        ],
        compiler_params=_cparams(("parallel",)),
        name="pre_gqa",
    )(stream, mods4, g_mix, w_ext, w_vt, bd, cq, sq, ck, sk)


def _attn_kernel(*refs, n_heads, kv_heads, dk, dv, has_lat):
    if has_lat:
        q_ref, kl_ref, vl_ref, kc_ref, vc_ref, o_ref, s_ref = refs
    else:
        q_ref, kc_ref, vc_ref, o_ref, s_ref = refs
    nt = (((1,), (1,)), ((), ()))
    tq = q_ref.shape[0]
    chunks, row = [], 0
    for k_ref, v_ref in ([(kc_ref, vc_ref), (kl_ref, vl_ref)] if has_lat else [(kc_ref, vc_ref)]):
        n_keys = k_ref.shape[-2]
        for c0 in range(0, n_keys, KEY_CHUNK):
            kc = min(KEY_CHUNK, n_keys - c0)
            chunks.append((k_ref, v_ref, c0, row, kc))
            row += kc
    mx = [None] * n_heads
    den = [None] * n_heads
    acc = [None] * n_heads
    for phase in range(n_heads + 1):
        for k_ref, v_ref, c0, row, kc in chunks:
            if phase < n_heads:
                g = phase
                gk = g * kv_heads // n_heads
                q = q_ref[:, g * dk:(g + 1) * dk]
                k = k_ref[gk, c0:c0 + kc, :] if k_ref.ndim == 3 else k_ref[c0:c0 + kc, gk * dk:(gk + 1) * dk]
                s = lax.dot_general(k, q, nt, preferred_element_type=F32)
                s_ref[g % 2, row:row + kc, :] = s
                cmax = jnp.max(s, axis=0, keepdims=True)
                mx[g] = cmax if mx[g] is None else jnp.maximum(mx[g], cmax)
            if phase > 0:
                g = phase - 1
                gk = g * kv_heads // n_heads
                p = jnp.exp2(s_ref[g % 2, row:row + kc, :] - mx[g])
                psum = jnp.sum(p, axis=0, keepdims=True)
                pv = jnp.dot(v_ref[gk * dv:(gk + 1) * dv, c0:c0 + kc], p.astype(BF16), preferred_element_type=F32)
                den[g] = psum if den[g] is None else den[g] + psum
                acc[g] = pv if acc[g] is None else acc[g] + pv
    o_all = jnp.concatenate([acc[g] / den[g] for g in range(n_heads)], axis=0)
    o_ref[...] = jnp.transpose(o_all).astype(o_ref.dtype)


def _attention(q, k, vt, geom, *, groups, n_heads, kv_heads, dk, dv, k_head_major, latent_queries):
    wq, wk, wv = n_heads * dk, kv_heads * dk, kv_heads * dv
    wo = n_heads * dv
    batch = geom.batch
    nc = geom.n_ctx
    ctx_blk0 = geom.n_lat // nc
    assert geom.n_lat % nc == 0

    def kspec(rows, tok_blk):
        if k_head_major:
            return pl.BlockSpec((kv_heads, rows, dk), lambda b, h, *_: (h, tok_blk(b), 0))
        return pl.BlockSpec((rows, wk), lambda b, h, *_: (tok_blk(b), h))

    def vspec(rows, tok_blk):
        return pl.BlockSpec((wv, rows), lambda b, h, *_: (h, tok_blk(b)))

    kern = functools.partial(_attn_kernel, n_heads=n_heads, kv_heads=kv_heads, dk=dk, dv=dv, has_lat=latent_queries)
    if latent_queries:
        tq = TT
        qt = geom.seq // tq
        return pl.pallas_call(
            kern,
            grid=(batch, groups, qt),
            in_specs=[
                pl.BlockSpec((tq, wq), lambda b, h, t: (b * qt + t, h)),
                kspec(geom.seq, lambda b: b),
                vspec(geom.seq, lambda b: b),
                kspec(nc, lambda b: ctx_blk0 + b),
                vspec(nc, lambda b: ctx_blk0 + b),
            ],
            out_specs=pl.BlockSpec((tq, wo), lambda b, h, t: (b * qt + t, h)),
            out_shape=jax.ShapeDtypeStruct((geom.n_lat, groups * wo), BF16),
            scratch_shapes=[pltpu.VMEM((2, geom.seq + nc, tq), F32)],
            compiler_params=_cparams(("parallel", "parallel", "arbitrary")),
            name="attn_lat",
        )(q, k, vt, k, vt)
    return pl.pallas_call(
        kern,
        grid=(batch, groups),
        in_specs=[
            pl.BlockSpec((nc, wq), lambda b, h: (ctx_blk0 + b, h)),
            kspec(nc, lambda b: ctx_blk0 + b),
            vspec(nc, lambda b: ctx_blk0 + b),
        ],
        out_specs=pl.BlockSpec((nc, wo), lambda b, h: (b, h)),
        out_shape=jax.ShapeDtypeStruct((batch * nc, groups * wo), BF16),
        scratch_shapes=[pltpu.VMEM((2, nc, nc), F32)],
        compiler_params=_cparams(("parallel", "parallel")),
        name="attn_ctx",
    )(q, k, vt)


def _post_kernel(o_ref, wo_ref, s_ref, mod_ref, g_ref, wrh_ref, wrl_ref, br_ref, s_out, h2_out, idx_out, w_out, cnt_out, *, n_experts):
    m = mod_ref[...]
    s = s_ref[...] + m[2:3] * jnp.dot(o_ref[...], wo_ref[...], preferred_element_type=F32)
    s_out[...] = s
    h2 = _norm_mod(s, g_ref[...], m[3:4], m[4:5])
    h2_out[...] = h2.astype(BF16)
    h2_hi, h2_lo = _split_hi_lo(h2)
    logits = (jnp.dot(h2_hi, wrh_ref[...], preferred_element_type=F32) + jnp.dot(h2_lo, wrh_ref[...], preferred_element_type=F32)
              + jnp.dot(h2_hi, wrl_ref[...], preferred_element_type=F32) + br_ref[...])
    lane = lax.broadcasted_iota(I32, logits.shape, 1).astype(F32)
    work = jnp.where(lane < n_experts, logits, -jnp.inf)
    vals, hits = [], []
    for k in range(TOP_K):
        mx = jnp.max(work, axis=-1, keepdims=True)
        sel = jnp.min(jnp.where(work == mx, lane, float(LANES)), axis=-1, keepdims=True)
        hit = lane == sel
        vals.append(mx)
        hits.append(hit)
        work = jnp.where(hit, -jnp.inf, work)
    es = [jnp.exp(v - vals[0]) for v in vals]
    den = es[0] + es[1] + es[2] + es[3]
    w_slab = jnp.zeros(logits.shape, F32)
    for k in range(TOP_K):
        w_slab = jnp.where(lane == k, es[k] / den, w_slab)
    w_out[...] = w_slab
    onehot = jnp.where(hits[0] | hits[1] | hits[2] | hits[3], 1.0, 0.0)
    tr = lax.broadcasted_iota(I32, (TT, TT), 0)
    tc = lax.broadcasted_iota(I32, (TT, TT), 1)
    rank = jnp.dot(jnp.where(tc < tr, 1.0, 0.0).astype(BF16), onehot.astype(BF16), preferred_element_type=F32)
    cnt = jnp.sum(onehot, axis=0, keepdims=True)
    cp = jnp.floor((cnt + (SUBLANES - 1)) * (1.0 / SUBLANES)) * SUBLANES
    er = lax.broadcasted_iota(I32, (LANES, LANES), 0)
    ec = lax.broadcasted_iota(I32, (LANES, LANES), 1)
    seg = jnp.dot(jnp.broadcast_to(cp, (SUBLANES, LANES)).astype(BF16), jnp.where(er < ec, 1.0, 0.0).astype(BF16),
                  preferred_element_type=F32)[0:1]
    base = seg + rank
    lp_slab = jnp.zeros(logits.shape, F32)
    for k in range(TOP_K):
        lp_slab = jnp.where(lane == k, jnp.sum(jnp.where(hits[k], base, 0.0), axis=-1, keepdims=True), lp_slab)
    idx_out[...] = lp_slab.astype(I32)
    cnt_out[...] = jnp.broadcast_to(cnt, (SUBLANES, LANES)).astype(I32)


def _post(o, w_o, stream, mods4, g_ffn, w_router, b_router, geom, layer, n_tiles):
    d = stream.shape[1]
    wo_w = o.shape[1]
    n_experts = w_router.shape[1]
    wr_hi, wr_lo = _split_hi_lo(jnp.concatenate([w_router, jnp.zeros((d, LANES - n_experts), F32)], axis=1))
    br = jnp.concatenate([b_router, jnp.zeros((LANES - n_experts,), F32)]).reshape(1, LANES)
    n = n_tiles * TT
    return pl.pallas_call(
        functools.partial(_post_kernel, n_experts=n_experts),
        grid=(n_tiles,),
        in_specs=[
            pl.BlockSpec((TT, wo_w), lambda j: (j, 0)),
            pl.BlockSpec((wo_w, d), lambda j: (0, 0)),
            pl.BlockSpec((TT, d), lambda j: (j, 0)),
            _mod_spec(geom, layer, d),
            pl.BlockSpec((1, d), lambda j: (0, 0)),
            pl.BlockSpec((d, LANES), lambda j: (0, 0)),
            pl.BlockSpec((d, LANES), lambda j: (0, 0)),
            pl.BlockSpec((1, LANES), lambda j: (0, 0)),
        ],
        out_specs=[
            pl.BlockSpec((TT, d), lambda j: (j, 0)),
            pl.BlockSpec((TT, d), lambda j: (j, 0)),
            pl.BlockSpec((TT, LANES), lambda j: (j, 0)),
            pl.BlockSpec((TT, LANES), lambda j: (j, 0)),
            pl.BlockSpec((None, SUBLANES, LANES), lambda j: (j, 0, 0)),
        ],
        out_shape=[
            jax.ShapeDtypeStruct((n, d), F32),
            jax.ShapeDtypeStruct((n, d), BF16),
            jax.ShapeDtypeStruct((n, LANES), I32),
            jax.ShapeDtypeStruct((n, LANES), F32),
            jax.ShapeDtypeStruct((n_tiles, SUBLANES, LANES), I32),
        ],
        compiler_params=_cparams(("parallel",)),
        name="post",
    )(o, w_o, stream, mods4, g_ffn, wr_hi, wr_lo, br)


def _local_rows(n_experts):
    return _round_up(TT * TOP_K + n_experts * (SUBLANES - 1), LANES)


def _n_blocks(n_tok, n_tiles, n_experts):
    rows = n_tok * TOP_K + n_experts * n_tiles * (SUBLANES - 1) + n_experts * (EXPERT_BLOCK - 1)
    return pl.cdiv(rows, EXPERT_BLOCK)


def _route_meta(cnt, lpos, n_experts):
    n_tiles = cnt.shape[0]
    n_blocks = _n_blocks(n_tiles * TT, n_tiles, n_experts)
    cp = _round_up(cnt, SUBLANES)
    seg = jnp.cumsum(cp, axis=1) - cp
    run_rows = cp.sum(axis=0)
    reg = _round_up(run_rows, EXPERT_BLOCK)
    reg_end = jnp.cumsum(reg)
    reg_start = reg_end - reg
    off = reg_start[None, :] + jnp.cumsum(cp, axis=0) - cp
    n_used = (reg_end[-1] // EXPERT_BLOCK).astype(I32)
    bstart = jnp.arange(n_blocks, dtype=I32) * EXPERT_BLOCK
    last_start = jnp.maximum(n_used - 1, 0) * EXPERT_BLOCK
    be = (jnp.minimum(bstart, last_start)[:, None] >= reg_end[None, :]).sum(axis=1).astype(I32)
    be = jnp.minimum(be, n_experts - 1)
    per_big = cp // BIG_CHUNK
    per_small = (cp % BIG_CHUNK) // SUBLANES
    tail = per_big * BIG_CHUNK
    return dict(
        big_list=_copy_list(per_big, seg, off, BIG_CHUNK, _max_big(n_experts)),
        small_list=_copy_list(per_small, seg + tail, off + tail, SUBLANES, _max_small(n_experts)),
        n_big=per_big.sum(axis=1).astype(I32), n_small=per_small.sum(axis=1).astype(I32),
        gap_start=(reg_start + run_rows).astype(I32), gap=(reg - run_rows).astype(I32),
        lpos=lpos.astype(I32), block_expert=be, n_used=n_used.reshape(1), n_blocks=n_blocks,
    )


def _max_big(n_experts):
    return _local_rows(n_experts) // BIG_CHUNK


def _max_small(n_experts):
    return n_experts * (BIG_CHUNK // SUBLANES - 1)


def _copy_list(per_run, local_row, sorted_row, chunk, max_copies):
    n_experts = per_run.shape[1]
    ends = jnp.cumsum(per_run, axis=1)
    k = jnp.arange(max_copies, dtype=I32)
    run = jnp.minimum((ends[:, None, :] <= k[None, :, None]).sum(axis=2), n_experts - 1)
    j = k[None, :] - jnp.take_along_axis(ends - per_run, run, axis=1)
    src = jnp.take_along_axis(local_row, run, axis=1) + chunk * j
    dst = jnp.take_along_axis(sorted_row, run, axis=1) + chunk * j
    return (dst * LIST_RADIX + src).reshape(-1).astype(I32)


def _run_copies(tile, nbig_ref, nsmall_ref, big_ref, small_ref, max_big, max_small, make_big, make_small):
    def start(make, packed):
        local_row = pl.multiple_of(packed & (LIST_RADIX - 1), SUBLANES)
        sorted_row = pl.multiple_of(lax.shift_right_logical(packed, LIST_SHIFT), SUBLANES)
        make(local_row, sorted_row).start()

    def big(k, c):
        start(make_big, big_ref[tile * max_big + k])
        return c

    def small(k, c):
        start(make_small, small_ref[tile * max_small + k])
        return c

    lax.fori_loop(0, nbig_ref[tile], big, 0)
    lax.fori_loop(0, nsmall_ref[tile], small, 0)


def _wait_copies(tile, nbig_ref, nsmall_ref, make_big, make_small):
    def wb(j, c):
        make_big(0, 0).wait()
        return c

    def ws(j, c):
        make_small(0, 0).wait()
        return c

    lax.fori_loop(0, nbig_ref[tile], wb, 0)
    lax.fori_loop(0, nsmall_ref[tile], ws, 0)


def _dispatch_kernel(big_ref, small_ref, nbig_ref, nsmall_ref, gs_ref, gap_ref, nu_ref,
                     lpt_ref, h2_ref, xs_ref, buf_ref, zero_ref, sem, zsem, *, n_experts, local_rows, n_blocks):
    i = pl.program_id(0)
    n = pl.num_programs(0)
    slot = i % 2

    def big(src, dst):
        return pltpu.make_async_copy(buf_ref.at[slot, pl.ds(src, BIG_CHUNK), :], xs_ref.at[pl.ds(dst, BIG_CHUNK), :], sem.at[slot])

    def small(src, dst):
        return pltpu.make_async_copy(buf_ref.at[slot, pl.ds(src, SUBLANES), :], xs_ref.at[pl.ds(dst, SUBLANES), :], sem.at[slot])

    @pl.when(i >= 2)
    def _():
        _wait_copies(i - 2, nbig_ref, nsmall_ref, big, small)

    lpt = lpt_ref[...]
    rows = lax.broadcasted_iota(I32, (local_rows, TT), 0)
    hit = rows == lpt[0:1, :]
    for k in range(1, TOP_K):
        hit = hit | (rows == lpt[k:k + 1, :])
    p = jnp.where(hit, 1.0, 0.0).astype(BF16)
    buf_ref[slot] = jnp.dot(p, h2_ref[...], preferred_element_type=F32)
    _run_copies(i, nbig_ref, nsmall_ref, big_ref, small_ref, _max_big(n_experts), _max_small(n_experts), big, small)

    @pl.when(i == n - 1)
    def _():
        zero_ref[...] = jnp.zeros_like(zero_ref)
        sizes = []
        size = EXPERT_BLOCK // 2
        while size >= SUBLANES:
            sizes.append(size)
            size //= 2

        def zcopy(dst, size):
            return pltpu.make_async_copy(zero_ref.at[pl.ds(0, size), :], xs_ref.at[pl.ds(dst, size), :], zsem)

        def per_expert(e, carry):
            gap = gap_ref[e]
            pos = gs_ref[e]
            for size in sizes:
                take = (gap & size) != 0

                @pl.when(take)
                def _():
                    zcopy(pl.multiple_of(pos, SUBLANES), size).start()

                pos = pos + jnp.where(take, size, 0)
            return carry

        lax.fori_loop(0, n_experts, per_expert, 0)

        zrows = zero_ref.shape[0]
        per_block = EXPERT_BLOCK // zrows
        tail_copies = (n_blocks - nu_ref[0]) * per_block

        def tail(t, carry):
            zcopy(pl.multiple_of(nu_ref[0] * EXPERT_BLOCK + t * zrows, SUBLANES), zrows).start()
            return carry

        lax.fori_loop(0, tail_copies, tail, 0)

        def per_expert_wait(e, carry):
            gap = gap_ref[e]
            for size in sizes:
                @pl.when((gap & size) != 0)
                def _():
                    zcopy(0, size).wait()
            return carry

        lax.fori_loop(0, n_experts, per_expert_wait, 0)

        def tail_wait(t, carry):
            zcopy(0, zrows).wait()
            return carry

        lax.fori_loop(0, tail_copies, tail_wait, 0)

        @pl.when(i >= 1)
        def _():
            def big_o(src, dst):
                return pltpu.make_async_copy(buf_ref.at[1 - slot, pl.ds(src, BIG_CHUNK), :], xs_ref.at[pl.ds(dst, BIG_CHUNK), :], sem.at[1 - slot])

            def small_o(src, dst):
                return pltpu.make_async_copy(buf_ref.at[1 - slot, pl.ds(src, SUBLANES), :], xs_ref.at[pl.ds(dst, SUBLANES), :], sem.at[1 - slot])

            _wait_copies(i - 1, nbig_ref, nsmall_ref, big_o, small_o)

        _wait_copies(i, nbig_ref, nsmall_ref, big, small)


def _dispatch(h2, meta, n_tiles, n_experts):
    n, d = h2.shape
    local_rows = _local_rows(n_experts)
    n_rows = meta["n_blocks"] * EXPERT_BLOCK
    lpt = meta["lpos"].reshape(n_tiles, TT, TOP_K).transpose(0, 2, 1)
    grid_spec = pltpu.PrefetchScalarGridSpec(
        num_scalar_prefetch=7,
        grid=(n_tiles,),
        in_specs=[
            pl.BlockSpec((None, TOP_K, TT), lambda j, *_: (j, 0, 0)),
            pl.BlockSpec((TT, d), lambda j, *_: (j, 0)),
        ],
        out_specs=pl.BlockSpec(memory_space=pl.ANY),
        scratch_shapes=[
            pltpu.VMEM((2, local_rows, d), F32),
            pltpu.VMEM((EXPERT_BLOCK // 2, d), F32),
            pltpu.SemaphoreType.DMA((2,)),
            pltpu.SemaphoreType.DMA(()),
        ],
    )
    return pl.pallas_call(
        functools.partial(_dispatch_kernel, n_experts=n_experts, local_rows=local_rows, n_blocks=meta["n_blocks"]),
        grid_spec=grid_spec,
        out_shape=jax.ShapeDtypeStruct((n_rows, d), F32),
        compiler_params=_cparams(("arbitrary",)),
        name="moe_dispatch",
    )(meta["big_list"], meta["small_list"], meta["n_big"], meta["n_small"], meta["gap_start"], meta["gap"], meta["n_used"],
      lpt, h2)


def _moe_prep_kernel(wgu_ref, wd_ref, sel_ref, wg_out, wu_out, wd_out):
    sel = sel_ref[...]
    pair = 2 * LANES
    for c in range(wgu_ref.shape[1] // pair):
        chunk = wgu_ref[:, c * pair:(c + 1) * pair].astype(BF16)
        de = jnp.dot(chunk, sel, preferred_element_type=F32)
        wg_out[:, c * LANES:(c + 1) * LANES] = de[:, :LANES].astype(BF16)
        wu_out[:, c * LANES:(c + 1) * LANES] = de[:, LANES:].astype(BF16)
    wd_out[...] = wd_ref[...].astype(BF16)


def _moe_prep(w_gu, w_down):
    depth, n_experts, d, f2 = w_gu.shape
    f = f2 // 2
    r = jnp.arange(2 * LANES)
    src = jnp.where(r < LANES, 2 * r, 2 * (r - LANES) + 1)
    sel = (r[:, None] == src[None, :]).astype(BF16)
    wspec = lambda rows, cols: pl.BlockSpec((None, None, rows, cols), lambda i, e: (i, e, 0, 0))
    return pl.pallas_call(
        _moe_prep_kernel,
        grid=(depth, n_experts),
        in_specs=[wspec(d, f2), wspec(f, d), pl.BlockSpec((2 * LANES, 2 * LANES), lambda i, e: (0, 0))],
        out_specs=[wspec(d, f), wspec(d, f), wspec(f, d)],
        out_shape=[
            jax.ShapeDtypeStruct((depth, n_experts, d, f), BF16),
            jax.ShapeDtypeStruct((depth, n_experts, d, f), BF16),
            jax.ShapeDtypeStruct((depth, n_experts, f, d), BF16),
        ],
        compiler_params=_cparams(("parallel", "parallel")),
        name="moe_prep",
    )(w_gu, w_down, sel)


def _ffn_kernel(be_ref, nu_ref, x_ref, wg_ref, wu_ref, wd_ref, bg_ref, bu_ref, bd_ref, y_ref):
    b = pl.program_id(0)

    @pl.when(b < nu_ref[0])
    def _():
        x = x_ref[...].astype(BF16)
        gate = jnp.dot(x, wg_ref[...], preferred_element_type=F32) + bg_ref[...]
        up = jnp.dot(x, wu_ref[...], preferred_element_type=F32) + bu_ref[...]
        gate = jnp.minimum(gate, SWIGLU_LIMIT)
        up = jnp.clip(up, -SWIGLU_LIMIT, SWIGLU_LIMIT)
        glu = gate / (1.0 + jnp.exp(-SWIGLU_ALPHA * gate))
        act = ((up + 1.0) * glu).astype(BF16)
        y_ref[...] = jnp.dot(act, wd_ref[...], preferred_element_type=F32) + bd_ref[...]

    @pl.when(b >= nu_ref[0])
    def _():
        y_ref[...] = jnp.zeros_like(y_ref)


def _ffn(xs, meta, layer, w_gate, w_up, w_down, b_gate, b_up, b_down):
    n_rows, d = xs.shape
    n_blocks = meta["n_blocks"]
    f = w_gate.shape[3]
    xmap = lambda b, be, nu: (jnp.minimum(b, jnp.maximum(nu[0] - 1, 0)), 0)
    wmap = lambda b, be, nu: (layer, be[b], 0, 0)
    grid_spec = pltpu.PrefetchScalarGridSpec(
        num_scalar_prefetch=2,
        grid=(n_blocks,),
        in_specs=[
            pl.BlockSpec((EXPERT_BLOCK, d), xmap),
            pl.BlockSpec((None, None, d, f), wmap),
            pl.BlockSpec((None, None, d, f), wmap),
            pl.BlockSpec((None, None, f, d), wmap),
            pl.BlockSpec((None, None, 1, f), wmap),
            pl.BlockSpec((None, None, 1, f), wmap),
            pl.BlockSpec((None, None, 1, d), wmap),
        ],
        out_specs=pl.BlockSpec((EXPERT_BLOCK, d), lambda b, be, nu: (b, 0)),
    )
    return pl.pallas_call(
        _ffn_kernel,
        grid_spec=grid_spec,
        out_shape=jax.ShapeDtypeStruct((n_rows, d), F32),
        compiler_params=_cparams(("arbitrary",)),
        name="moe_ffn",
    )(meta["block_expert"], meta["n_used"], xs, w_gate, w_up, w_down, b_gate, b_up, b_down)


def _combine_kernel(big_ref, small_ref, nbig_ref, nsmall_ref,
                    lp_ref, w_ref, s_ref, mod_ref, gf_ref, ys_ref, o_ref, buf_ref, sem, *, n_experts, local_rows, final):
    i = pl.program_id(0)
    n = pl.num_programs(0)
    slot = i % 2

    def copies(sl):
        def big(dst, src):
            return pltpu.make_async_copy(ys_ref.at[pl.ds(src, BIG_CHUNK), :], buf_ref.at[sl, pl.ds(dst, BIG_CHUNK), :], sem.at[sl])

        def small(dst, src):
            return pltpu.make_async_copy(ys_ref.at[pl.ds(src, SUBLANES), :], buf_ref.at[sl, pl.ds(dst, SUBLANES), :], sem.at[sl])

        return big, small

    def start_copies(tile, sl):
        _run_copies(tile, nbig_ref, nsmall_ref, big_ref, small_ref, _max_big(n_experts), _max_small(n_experts), *copies(sl))

    @pl.when(i == 0)
    def _():
        buf_ref[...] = jnp.zeros_like(buf_ref)
        start_copies(0, 0)

    @pl.when(i + 1 < n)
    def _():
        start_copies(i + 1, 1 - slot)

    _wait_copies(i, nbig_ref, nsmall_ref, *copies(slot))

    lp = lp_ref[...]
    w = w_ref[...]
    lanes = lax.broadcasted_iota(I32, (TT, local_rows), 1)
    pw = jnp.zeros((TT, local_rows), F32)
    for k in range(TOP_K):
        pw = pw + jnp.where(lanes == lp[:, k:k + 1], w[:, k:k + 1], 0.0)
    y = buf_ref[slot].astype(BF16)
    f = jnp.dot(pw.astype(BF16), y, preferred_element_type=F32)
    m = mod_ref[...]
    s = s_ref[...] + m[5:6] * f
    if final:
        s = _rms(s) * gf_ref[...]
    o_ref[...] = s


def _combine(ys, meta, top_w, stream, mods4, g_final, geom, layer, n_tiles, n_experts, final):
    d = stream.shape[1]
    local_rows = _local_rows(n_experts)
    grid_spec = pltpu.PrefetchScalarGridSpec(
        num_scalar_prefetch=4,
        grid=(n_tiles,),
        in_specs=[
            pl.BlockSpec((TT, TOP_K), lambda j, *_: (j, 0)),
            pl.BlockSpec((TT, TOP_K), lambda j, *_: (j, 0)),
            pl.BlockSpec((TT, d), lambda j, *_: (j, 0)),
            pl.BlockSpec((None, None, N_MOD, d), lambda j, *_: (layer, geom.group(j), 0, 0)),
            pl.BlockSpec((1, d), lambda j, *_: (0, 0)),
            pl.BlockSpec(memory_space=pl.ANY),
        ],
        out_specs=pl.BlockSpec((TT, d), lambda j, *_: (j, 0)),
        scratch_shapes=[
            pltpu.VMEM((2, local_rows, d), F32),
            pltpu.SemaphoreType.DMA((2,)),
        ],
    )
    return pl.pallas_call(
        functools.partial(_combine_kernel, n_experts=n_experts, local_rows=local_rows, final=final),
        grid_spec=grid_spec,
        out_shape=jax.ShapeDtypeStruct((n_tiles * TT, d), F32),
        compiler_params=_cparams(("arbitrary",)),
        name="moe_combine",
    )(meta["big_list"], meta["small_list"], meta["n_big"], meta["n_small"],
      meta["lpos"], top_w, stream, mods4, g_final, ys)


def kernel(x, c, ctx, c_ctx, w_mod, b_mod, g_mix, g_ffn, g_final, f_w_in, f_w_out, mla_w_in, mla_g_qa, mla_w_qb,
           mla_g_kva, mla_w_kvb, mla_w_o, gqa_w_qkv, gqa_g_q, gqa_g_k, gqa_w_o, moe_w_router, moe_b_router,
           moe_w_gu, moe_b_gu, moe_w_down, moe_b_down):
    batch, seq, d = x.shape
    n_ctx = ctx.shape[1]
    depth = w_mod.shape[0]
    n_experts = moe_w_router.shape[2]
    geom = _Geom(batch, seq, n_ctx)

    n_groups = _round_up(batch + 1, SUBLANES)
    cc = jnp.concatenate([c, c_ctx[None, :], jnp.zeros((n_groups - batch - 1, d), F32)], axis=0)
    mods4 = _mods(cc, w_mod, b_mod).reshape(depth, n_groups, N_MOD, d)

    stream = jnp.concatenate([x.reshape(batch * seq, d), ctx.reshape(batch * n_ctx, d)], axis=0)

    gd = d // FOURIER_GROUPS
    gcos, gsin = _dft_tables(gd)
    eye = jnp.eye(FOURIER_GROUPS, dtype=F32)
    bd_cs = jnp.concatenate([jnp.kron(eye, gcos), jnp.kron(eye, gsin)], axis=1)

    f = moe_w_down.shape[2]
    w_gate, w_up, w_down = _moe_prep(moe_w_gu, moe_w_down)
    b_gate = moe_b_gu[:, :, 0::2].reshape(depth, n_experts, 1, f)
    b_up = moe_b_gu[:, :, 1::2].reshape(depth, n_experts, 1, f)
    b_down = moe_b_down.reshape(depth, n_experts, 1, d)

    for i in range(depth):
        kind, j = i % N_MIXERS, i // N_MIXERS
        last = i == depth - 1
        ctx_used = (kind != 0) or (not last)
        n_tiles = geom.n_tiles if ctx_used else geom.n_lat_tiles
        gm = g_mix[i].reshape(1, d)
        gf = g_ffn[i].reshape(1, d)

        if kind == 0:
            wcs = _fold(f_w_in[j], bd_cs, BF16)
            u = _pre_fourier(stream, mods4, gm, wcs, geom, i, n_tiles)
            o = _dft(u, 0, seq, batch, d)
            if ctx_used:
                o = jnp.concatenate([o, _dft(u, geom.n_lat, n_ctx, batch, d)], axis=0)
            w_o = f_w_out[j].astype(BF16)
        elif kind == 1:
            q, k, v = _pre_mla(stream, mods4, gm, mla_w_in[j], mla_g_qa[j], mla_w_qb[j], mla_g_kva[j], mla_w_kvb[j], geom, i)
            att = functools.partial(_attention, q, k, v, geom, groups=2, n_heads=MLA_HEADS // 2, kv_heads=MLA_HEADS // 2,
                                    dk=MLA_SLOT, dv=MLA_V, k_head_major=False)
            o = att(latent_queries=True)
            if not last:
                o = jnp.concatenate([o, att(latent_queries=False)], axis=0)
            w_o = mla_w_o[j].astype(BF16)
        else:
            q, k, v = _pre_gqa(stream, mods4, gm, gqa_w_qkv[j], gqa_g_q[j], gqa_g_k[j], geom, i)
            att = functools.partial(_attention, q, k, v, geom, groups=2, n_heads=GQA_HEADS // 2, kv_heads=GQA_KV_HEADS // 2,
                                    dk=GQA_HEAD_DIM, dv=GQA_HEAD_DIM, k_head_major=True)
            o = att(latent_queries=True)
            if not last:
                o = jnp.concatenate([o, att(latent_queries=False)], axis=0)
            w_o = gqa_w_o[j].astype(BF16)

        n_moe_tiles = geom.n_lat_tiles if last else geom.n_tiles
        stream, h2, lp_slab, w_slab, cnt_slab = _post(o, w_o, stream, mods4, gf, moe_w_router[i], moe_b_router[i], geom, i,
                                                      n_moe_tiles)
        meta = _route_meta(cnt_slab[:, 0, :n_experts], lp_slab[:, :TOP_K], n_experts)
        xs = _dispatch(h2, meta, n_moe_tiles, n_experts)
        ys = _ffn(xs, meta, i, w_gate, w_up, w_down, b_gate, b_up, b_down)
        stream = _combine(ys, meta, w_slab[:, :TOP_K], stream, mods4, g_final.reshape(1, d), geom, i, n_moe_tiles, n_experts,
                          last)

    return stream[:batch * seq].reshape(batch, seq, d)
```

```python
import functools

import jax
import jax.numpy as jnp
from jax import lax
from jax.experimental import pallas as pl
from jax.experimental.pallas import tpu as pltpu

F32 = jnp.float32
BF16 = jnp.bfloat16
I32 = jnp.int32
HI = lax.Precision.HIGHEST

GRID_W = 64
N_MIXERS = 3
NORM_EPS = 1e-6
ROPE_THETA = 10000.0
FOURIER_GROUPS = 8
MLA_HEADS = 16
MLA_Q_LORA = 384
MLA_KV_LORA = 256
MLA_NOPE = 64
MLA_ROPE = 32
MLA_V = 64
GQA_HEADS = 16
GQA_KV_HEADS = 4
GQA_HEAD_DIM = 64
TOP_K = 4
SWIGLU_LIMIT = 7.0
SWIGLU_ALPHA = 1.702
N_MOD = 6

LANES = 128
SUBLANES = 8
TT = 256
KEY_CHUNK = 1024
LOG2_E = 1.4426950408889634
EXPERT_BLOCK = 1024
BIG_CHUNK = 32
LIST_SHIFT = 11
LIST_RADIX = 1 << LIST_SHIFT
MLA_SLOT = 128
VMEM_LIMIT = 56 * 1024 * 1024


def _cparams(sem):
    return pltpu.CompilerParams(dimension_semantics=sem, vmem_limit_bytes=VMEM_LIMIT)


def _round_up(x, m):
    return (x + m - 1) // m * m


def _norm_mod(x, g, shift, scale):
    y = x * lax.rsqrt(jnp.mean(x * x, axis=-1, keepdims=True) + NORM_EPS) * g
    return y * (1.0 + scale) + shift


def _rms(x):
    return x * lax.rsqrt(jnp.mean(x * x, axis=-1, keepdims=True) + NORM_EPS)


def _mod_kernel(cc_ref, w_ref, b_ref, o_ref):
    cc = cc_ref[...]
    a = cc / (1.0 + jnp.exp(-cc))
    o_ref[...] = jnp.dot(a, w_ref[...], precision=HI, preferred_element_type=F32) + b_ref[...]


def _mods(cc, w_mod, b_mod):
    depth, d, n6 = w_mod.shape
    g = cc.shape[0]
    tn = 1536 if n6 % 1536 == 0 else n6
    return pl.pallas_call(
        _mod_kernel,
        grid=(depth, n6 // tn),
        in_specs=[
            pl.BlockSpec((g, d), lambda i, n: (0, 0)),
            pl.BlockSpec((None, d, tn), lambda i, n: (i, 0, n)),
            pl.BlockSpec((None, 1, tn), lambda i, n: (i, 0, n)),
        ],
        out_specs=pl.BlockSpec((None, g, tn), lambda i, n: (i, 0, n)),
        out_shape=jax.ShapeDtypeStruct((depth, g, n6), F32),
        compiler_params=_cparams(("parallel", "parallel")),
        name="mods",
    )(cc, w_mod, b_mod.reshape(depth, 1, n6))


def _fold_kernel(a_ref, b_ref, o_ref):
    o_ref[...] = jnp.dot(a_ref[...], b_ref[...], precision=HI, preferred_element_type=F32).astype(o_ref.dtype)


def _fold(a, b, out_dtype):
    m, k = a.shape
    n = b.shape[1]
    tn = 512
    return pl.pallas_call(
        _fold_kernel,
        grid=(n // tn,),
        in_specs=[pl.BlockSpec((m, k), lambda j: (0, 0)), pl.BlockSpec((k, tn), lambda j: (0, j))],
        out_specs=pl.BlockSpec((m, tn), lambda j: (0, j)),
        out_shape=jax.ShapeDtypeStruct((m, n), out_dtype),
        compiler_params=_cparams(("parallel",)),
        name="fold",
    )(a, b)


class _Geom:
    def __init__(self, batch, seq, n_ctx):
        self.batch, self.seq, self.n_ctx = batch, seq, n_ctx
        assert seq % TT == 0 and n_ctx % TT == 0 and seq % GRID_W == 0
        self.lat_per_b = seq // TT
        self.ctx_per_b = n_ctx // TT
        self.n_lat_tiles = batch * self.lat_per_b
        self.n_ctx_tiles = batch * self.ctx_per_b
        self.n_tiles = self.n_lat_tiles + self.n_ctx_tiles
        self.n_lat = batch * seq
        self.n_tok = self.n_lat + batch * n_ctx

    def group(self, j):
        return jnp.where(j < self.n_lat_tiles, j // self.lat_per_b, self.batch)

    def pos_tile(self, j):
        return jnp.where(j < self.n_lat_tiles, j % self.lat_per_b, self.lat_per_b)


def _mod_spec(geom, layer, d):
    return pl.BlockSpec((None, None, N_MOD, d), lambda j: (layer, geom.group(j), 0, 0))


FOURIER_TILE = 1024


def _pre_fourier_kernel(s_ref, mod_ref, g_ref, perm_ref, w_ref, u_ref):
    m = mod_ref[...]
    h = _norm_mod(s_ref[...], g_ref[...], m[0:1], m[1:2]).astype(BF16)
    h = jnp.dot(perm_ref[...], h, preferred_element_type=F32).astype(BF16)
    u_ref[...] = jnp.dot(h, w_ref[...], preferred_element_type=F32).astype(BF16)


def _pre_fourier(stream, mods4, g_mix, wcs, geom, layer, n_tok):
    d = stream.shape[1]
    assert n_tok % FOURIER_TILE == 0 and geom.seq % FOURIER_TILE == 0 and FOURIER_TILE % geom.n_ctx == 0
    lat_tiles = geom.n_lat // FOURIER_TILE
    per_b = geom.seq // FOURIER_TILE
    group = lambda j: jnp.where(j < lat_tiles, j // per_b, geom.batch)
    r = jnp.arange(FOURIER_TILE)
    half = FOURIER_TILE // 2
    src = jnp.where(r < half, 2 * r, 2 * (r - half) + 1)
    perm = (src[:, None] == r[None, :]).astype(BF16)
    return pl.pallas_call(
        _pre_fourier_kernel,
        grid=(n_tok // FOURIER_TILE,),
        in_specs=[
            pl.BlockSpec((FOURIER_TILE, d), lambda j: (j, 0)),
            pl.BlockSpec((None, None, N_MOD, d), lambda j: (layer, group(j), 0, 0)),
            pl.BlockSpec((1, d), lambda j: (0, 0)),
            pl.BlockSpec((FOURIER_TILE, FOURIER_TILE), lambda j: (0, 0)),
            pl.BlockSpec((d, 2 * d), lambda j: (0, 0)),
        ],
        out_specs=pl.BlockSpec((FOURIER_TILE, 2 * d), lambda j: (j, 0)),
        out_shape=jax.ShapeDtypeStruct((n_tok, 2 * d), BF16),
        compiler_params=_cparams(("parallel",)),
        name="pre_fourier",
    )(stream, mods4, g_mix, perm, wcs)


def _dft_kernel(ce_ref, se_ref, co_ref, so_ref, ue_ref, uo_ref, o_ref, acce_ref, acco_ref, *, d):
    k = pl.program_id(2)

    @pl.when(k == 0)
    def _():
        acce_ref[...] = jnp.zeros_like(acce_ref)
        acco_ref[...] = jnp.zeros_like(acco_ref)

    acce_ref[...] += (jnp.dot(ce_ref[...], ue_ref[:, :d], preferred_element_type=F32)
                      - jnp.dot(se_ref[...], ue_ref[:, d:], preferred_element_type=F32))
    acco_ref[...] += (jnp.dot(co_ref[...], uo_ref[:, :d], preferred_element_type=F32)
                      - jnp.dot(so_ref[...], uo_ref[:, d:], preferred_element_type=F32))

    @pl.when(k == pl.num_programs(2) - 1)
    def _():
        o_ref[0] = (acce_ref[...] + acco_ref[...]).astype(o_ref.dtype)
        o_ref[1] = (acce_ref[...] - acco_ref[...]).astype(o_ref.dtype)


def _dft_tables(n, rows=None, col_step=1, col_off=0):
    rows = n if rows is None else rows
    cols = n // col_step
    scale = n ** -0.5

    def tables(n_rows, step):
        k = lax.broadcasted_iota(I32, (n_rows, cols), 0) * step
        c = lax.broadcasted_iota(I32, (n_rows, cols), 1) * col_step + col_off
        ang = ((k * c) % n).astype(F32) * (2.0 * jnp.pi / n)
        return jnp.cos(ang), jnp.sin(ang)

    coarse = 64
    if rows <= 4 * coarse:
        c, s = tables(rows, 1)
        return c * scale, s * scale
    ca, sa = tables(rows // coarse, coarse)
    cb, sb = tables(coarse, 1)
    ca, sa = ca[:, None, :] * scale, sa[:, None, :] * scale
    cos = (ca * cb[None] - sa * sb[None]).reshape(rows, cols)
    sin = (sa * cb[None] + ca * sb[None]).reshape(rows, cols)
    return cos, sin


def _dft(u, tok0, seq, batch, d):
    half = seq // 2
    ce, se = _dft_tables(seq, half, 2, 0)
    co, so = _dft_tables(seq, half, 2, 1)
    tabs = [t.astype(BF16) for t in (ce, se, co, so)]
    kb = min(FOURIER_TILE // 2, half)
    ksteps = half // kb
    tm = min(half, 1024)
    per_tile = FOURIER_TILE // kb

    def even_block(b, k):
        tok = tok0 + b * seq + k * FOURIER_TILE
        return (tok // FOURIER_TILE) * per_tile + (tok % FOURIER_TILE) // (2 * kb)

    tspec = pl.BlockSpec((tm, kb), lambda b, m, k: (m, k))
    out = pl.pallas_call(
        functools.partial(_dft_kernel, d=d),
        grid=(batch, half // tm, ksteps),
        in_specs=[
            tspec, tspec, tspec, tspec,
            pl.BlockSpec((kb, 2 * d), lambda b, m, k: (even_block(b, k), 0)),
            pl.BlockSpec((kb, 2 * d), lambda b, m, k: (even_block(b, k) + per_tile // 2, 0)),
        ],
        out_specs=pl.BlockSpec((None, 2, tm, d), lambda b, m, k: (b, 0, m, 0)),
        out_shape=jax.ShapeDtypeStruct((batch, 2, half, d), BF16),
        scratch_shapes=[pltpu.VMEM((tm, d), F32), pltpu.VMEM((tm, d), F32)],
        compiler_params=_cparams(("parallel", "parallel", "arbitrary")),
        name="dft",
    )(*tabs, u, u)
    return out.reshape(batch * seq, d)


def _pre_mla_kernel(s_ref, mod_ref, g_ref, win_ref, gqa_ref, gkva_ref, wqb_ref, wqbs_ref, wkvk_ref, wkvv_ref,
                    cos_ref, sin_ref, q_ref, k_ref, v_ref, *, scale):
    m = mod_ref[...]
    h = _norm_mod(s_ref[...], g_ref[...], m[0:1], m[1:2])
    a = jnp.dot(h.astype(BF16), win_ref[...], preferred_element_type=F32)
    aq = (_rms(a[:, :MLA_Q_LORA]) * gqa_ref[...]).astype(BF16)
    ckv = (_rms(a[:, MLA_Q_LORA:MLA_Q_LORA + MLA_KV_LORA]) * gkva_ref[...]).astype(BF16)
    o = MLA_Q_LORA + MLA_KV_LORA
    cos = cos_ref[...]
    sin = sin_ref[...]
    kpe = a[:, o:o + MLA_SLOT] * cos + a[:, o + MLA_SLOT:o + 2 * MLA_SLOT] * sin
    cos_h = jnp.tile(cos, (1, MLA_HEADS))
    sin_h = jnp.tile(sin, (1, MLA_HEADS))
    q = jnp.dot(aq, wqb_ref[...], preferred_element_type=F32)
    qs = jnp.dot(aq, wqbs_ref[...], preferred_element_type=F32)
    q_ref[...] = ((q * cos_h + qs * sin_h) * scale).astype(BF16)
    kk = jnp.dot(ckv, wkvk_ref[...], preferred_element_type=F32)
    k_ref[...] = (kk + jnp.tile(kpe, (1, MLA_HEADS))).astype(BF16)
    v_ref[...] = lax.dot_general(wkvv_ref[...], ckv, (((1,), (1,)), ((), ())), preferred_element_type=F32).astype(BF16)


def _rope_perm(rot):
    sec = rot // 2
    half = sec // 2
    d = jnp.arange(rot)
    first = (d % sec) < half
    partner = jnp.where(first, d + half, d - half)
    sign = jnp.where(first, -1.0, 1.0).astype(F32)
    return partner, sign


def _rope_tables(seq, rot, lat_per_b):
    rows = seq // GRID_W
    row = jnp.repeat(jnp.arange(rows, dtype=F32), GRID_W)
    col = jnp.tile(jnp.arange(GRID_W, dtype=F32), rows)
    n_freq = rot // 4
    inv_freq = ROPE_THETA ** (-jnp.arange(n_freq, dtype=F32) / n_freq)
    ang = jnp.stack([row[:, None] * inv_freq, col[:, None] * inv_freq], axis=1)
    cos = jnp.cos(ang)
    sin = jnp.sin(ang)
    cos_full = jnp.concatenate([cos, cos], axis=-1).reshape(seq, rot)
    sin_full = jnp.concatenate([sin, sin], axis=-1).reshape(seq, rot)
    _, sign = _rope_perm(rot)
    sin_full = sin_full * sign
    ident_c = jnp.ones((TT, rot), F32)
    ident_s = jnp.zeros((TT, rot), F32)
    assert seq == lat_per_b * TT
    return jnp.concatenate([cos_full, ident_c], 0), jnp.concatenate([sin_full, ident_s], 0)


def _pre_mla(stream, mods4, g_mix, w_in, g_qa, w_qb, g_kva, w_kvb, geom, layer):
    d = stream.shape[1]
    hd = MLA_NOPE + MLA_ROPE
    partner, _ = _rope_perm(MLA_ROPE)
    o = MLA_Q_LORA + MLA_KV_LORA
    kpe_w = w_in[:, o:]
    z_lo = jnp.zeros((d, MLA_NOPE), F32)
    z_hi = jnp.zeros((d, MLA_SLOT - hd), F32)
    w_in_ext = jnp.concatenate([w_in[:, :o], z_lo, kpe_w, z_hi, z_lo, kpe_w[:, partner], z_hi], axis=1).astype(BF16)
    wq = w_qb.reshape(MLA_Q_LORA, MLA_HEADS, hd)
    zq = jnp.zeros((MLA_Q_LORA, MLA_HEADS, MLA_SLOT - hd), F32)
    wq_p = jnp.concatenate([wq, zq], axis=2).reshape(MLA_Q_LORA, MLA_HEADS * MLA_SLOT).astype(BF16)
    wq_s = jnp.concatenate([jnp.zeros_like(wq[:, :, :MLA_NOPE]), wq[:, :, MLA_NOPE:][:, :, partner], zq], axis=2)
    wq_s = wq_s.reshape(MLA_Q_LORA, MLA_HEADS * MLA_SLOT).astype(BF16)
    wkv = w_kvb.reshape(MLA_KV_LORA, MLA_HEADS, MLA_NOPE + MLA_V)
    zk = jnp.zeros((MLA_KV_LORA, MLA_HEADS, MLA_SLOT - MLA_NOPE), F32)
    wkv_k = jnp.concatenate([wkv[:, :, :MLA_NOPE], zk], axis=2).reshape(MLA_KV_LORA, MLA_HEADS * MLA_SLOT).astype(BF16)
    wkv_v = wkv[:, :, MLA_NOPE:].reshape(MLA_KV_LORA, MLA_HEADS * MLA_V).T.astype(BF16)
    cos, sin = _rope_tables(geom.seq, MLA_ROPE, geom.lat_per_b)
    rows = cos.shape[0]
    cos_slot = jnp.concatenate([jnp.ones((rows, MLA_NOPE), F32), cos, jnp.ones((rows, MLA_SLOT - hd), F32)], axis=1)
    sin_slot = jnp.concatenate([jnp.zeros((rows, MLA_NOPE), F32), sin, jnp.zeros((rows, MLA_SLOT - hd), F32)], axis=1)
    n_tiles = geom.n_tiles
    wq_w = MLA_HEADS * MLA_SLOT
    wv_w = MLA_HEADS * MLA_V
    full = lambda a: pl.BlockSpec(a.shape, lambda j: (0,) * a.ndim)
    g_qa2, g_kva2 = g_qa.reshape(1, -1), g_kva.reshape(1, -1)
    return pl.pallas_call(
        functools.partial(_pre_mla_kernel, scale=float(hd) ** -0.5 * LOG2_E),
        grid=(n_tiles,),
        in_specs=[
            pl.BlockSpec((TT, d), lambda j: (j, 0)),
            _mod_spec(geom, layer, d),
            pl.BlockSpec((1, d), lambda j: (0, 0)),
            full(w_in_ext), full(g_qa2), full(g_kva2), full(wq_p), full(wq_s), full(wkv_k), full(wkv_v),
            pl.BlockSpec((TT, MLA_SLOT), lambda j: (geom.pos_tile(j), 0)),
            pl.BlockSpec((TT, MLA_SLOT), lambda j: (geom.pos_tile(j), 0)),
        ],
        out_specs=[
            pl.BlockSpec((TT, wq_w), lambda j: (j, 0)),
            pl.BlockSpec((TT, wq_w), lambda j: (j, 0)),
            pl.BlockSpec((wv_w, TT), lambda j: (0, j)),
        ],
        out_shape=[
            jax.ShapeDtypeStruct((n_tiles * TT, wq_w), BF16),
            jax.ShapeDtypeStruct((n_tiles * TT, wq_w), BF16),
            jax.ShapeDtypeStruct((wv_w, n_tiles * TT), BF16),
        ],
        compiler_params=_cparams(("parallel",)),
        name="pre_mla",
    )(stream, mods4, g_mix, w_in_ext, g_qa2, g_kva2, wq_p, wq_s, wkv_k, wkv_v, cos_slot, sin_slot)


def _split_hi_lo(x):
    hi = x.astype(BF16)
    lo = (x - hi.astype(F32)).astype(BF16)
    return hi, lo


def _pre_gqa_kernel(s_ref, mod_ref, g_ref, w_ref, wvt_ref, bd_ref, cq_ref, sq_ref, ck_ref, sk_ref, q_ref, k_ref, v_ref, *,
                    scale):
    m = mod_ref[...]
    h = _norm_mod(s_ref[...], g_ref[...], m[0:1], m[1:2]).astype(BF16)
    a = jnp.dot(h, w_ref[...], preferred_element_type=F32)
    nq = GQA_HEADS * GQA_HEAD_DIM
    nkv = GQA_KV_HEADS * GQA_HEAD_DIM
    aq, ak = a[:, :nq], a[:, nq:nq + nkv]
    aqs, aks = a[:, nq + nkv:2 * nq + nkv], a[:, 2 * nq + nkv:]
    bd = bd_ref[...]

    def head_rs(x, width):
        hi, lo = _split_hi_lo(x * x)
        b = bd[:width, :width]
        ssq = jnp.dot(hi, b, preferred_element_type=F32) + jnp.dot(lo, b, preferred_element_type=F32)
        return lax.rsqrt(ssq * (1.0 / GQA_HEAD_DIM) + NORM_EPS)

    reps_q = nq // cq_ref.shape[1]
    reps_k = nkv // ck_ref.shape[1]
    q = head_rs(aq, nq) * (aq * jnp.tile(cq_ref[...], (1, reps_q)) + aqs * jnp.tile(sq_ref[...], (1, reps_q)))
    q_ref[...] = (q * scale).astype(BF16)
    k = head_rs(ak, nkv) * (ak * jnp.tile(ck_ref[...], (1, reps_k)) + aks * jnp.tile(sk_ref[...], (1, reps_k)))
    k = k.astype(BF16)
    for g in range(GQA_KV_HEADS):
        k_ref[g] = k[:, g * GQA_HEAD_DIM:(g + 1) * GQA_HEAD_DIM]
    v_ref[...] = lax.dot_general(wvt_ref[...], h, (((1,), (1,)), ((), ())), preferred_element_type=F32).astype(BF16)


def _pre_gqa(stream, mods4, g_mix, w_qkv, g_q, g_k, geom, layer):
    d = stream.shape[1]
    nq = GQA_HEADS * GQA_HEAD_DIM
    nkv = GQA_KV_HEADS * GQA_HEAD_DIM
    partner, _ = _rope_perm(GQA_HEAD_DIM)
    wq = w_qkv[:, :nq].reshape(d, GQA_HEADS, GQA_HEAD_DIM)
    wk = w_qkv[:, nq:nq + nkv].reshape(d, GQA_KV_HEADS, GQA_HEAD_DIM)
    w_ext = jnp.concatenate([w_qkv[:, :nq + nkv], wq[:, :, partner].reshape(d, nq), wk[:, :, partner].reshape(d, nkv)],
                            axis=1).astype(BF16)
    w_vt = w_qkv[:, nq + nkv:].T.astype(BF16)
    cos, sin = _rope_tables(geom.seq, GQA_HEAD_DIM, geom.lat_per_b)
    per = LANES // GQA_HEAD_DIM
    cq = jnp.tile(cos * g_q[None, :], (1, per))
    sq = jnp.tile(sin * g_q[partner][None, :], (1, per))
    ck = jnp.tile(cos * g_k[None, :], (1, per))
    sk = jnp.tile(sin * g_k[partner][None, :], (1, per))
    hid = jnp.arange(nq) // GQA_HEAD_DIM
    bd = (hid[:, None] == hid[None, :]).astype(BF16)
    n_tiles = geom.n_tiles
    full = lambda a: pl.BlockSpec(a.shape, lambda j: (0,) * a.ndim)
    tab = pl.BlockSpec((TT, LANES), lambda j: (geom.pos_tile(j), 0))
    return pl.pallas_call(
        functools.partial(_pre_gqa_kernel, scale=float(GQA_HEAD_DIM) ** -0.5 * LOG2_E),
        grid=(n_tiles,),
        in_specs=[
            pl.BlockSpec((TT, d), lambda j: (j, 0)),
            _mod_spec(geom, layer, d),
            pl.BlockSpec((1, d), lambda j: (0, 0)),
            full(w_ext), full(w_vt), full(bd), tab, tab, tab, tab,
        ],
        out_specs=[
            pl.BlockSpec((TT, nq), lambda j: (j, 0)),
            pl.BlockSpec((GQA_KV_HEADS, TT, GQA_HEAD_DIM), lambda j: (0, j, 0)),
            pl.BlockSpec((nkv, TT), lambda j: (0, j)),
        ],
        out_shape=[
            jax.ShapeDtypeStruct((n_tiles * TT, nq), BF16),
            jax.ShapeDtypeStruct((GQA_KV_HEADS, n_tiles * TT, GQA_HEAD_DIM), BF16),
            jax.ShapeDtypeStruct((nkv, n_tiles * TT), BF16),
        ],
        compiler_params=_cparams(("parallel",)),
        name="pre_gqa",
    )(stream, mods4, g_mix, w_ext, w_vt, bd, cq, sq, ck, sk)


def _attn_kernel(*refs, n_heads, kv_heads, dk, dv, has_lat):
    if has_lat:
        q_ref, kl_ref, vl_ref, kc_ref, vc_ref, o_ref, s_ref = refs
    else:
        q_ref, kc_ref, vc_ref, o_ref, s_ref = refs
    nt = (((1,), (1,)), ((), ()))
    tq = q_ref.shape[0]
    chunks, row = [], 0
    for k_ref, v_ref in ([(kc_ref, vc_ref), (kl_ref, vl_ref)] if has_lat else [(kc_ref, vc_ref)]):
        n_keys = k_ref.shape[-2]
        for c0 in range(0, n_keys, KEY_CHUNK):
            kc = min(KEY_CHUNK, n_keys - c0)
            chunks.append((k_ref, v_ref, c0, row, kc))
            row += kc
    mx = [None] * n_heads
    den = [None] * n_heads
    acc = [None] * n_heads
    for phase in range(n_heads + 1):
        for k_ref, v_ref, c0, row, kc in chunks:
            if phase < n_heads:
                g = phase
                gk = g * kv_heads // n_heads
                q = q_ref[:, g * dk:(g + 1) * dk]
                k = k_ref[gk, c0:c0 + kc, :] if k_ref.ndim == 3 else k_ref[c0:c0 + kc, gk * dk:(gk + 1) * dk]
                s = lax.dot_general(k, q, nt, preferred_element_type=F32)
                s_ref[g % 2, row:row + kc, :] = s
                cmax = jnp.max(s, axis=0, keepdims=True)
                mx[g] = cmax if mx[g] is None else jnp.maximum(mx[g], cmax)
            if phase > 0:
                g = phase - 1
                gk = g * kv_heads // n_heads
                p = jnp.exp2(s_ref[g % 2, row:row + kc, :] - mx[g])
                psum = jnp.sum(p, axis=0, keepdims=True)
                pv = jnp.dot(v_ref[gk * dv:(gk + 1) * dv, c0:c0 + kc], p.astype(BF16), preferred_element_type=F32)
                den[g] = psum if den[g] is None else den[g] + psum
                acc[g] = pv if acc[g] is None else acc[g] + pv
    o_all = jnp.concatenate([acc[g] / den[g] for g in range(n_heads)], axis=0)
    o_ref[...] = jnp.transpose(o_all).astype(o_ref.dtype)


def _attention(q, k, vt, geom, *, groups, n_heads, kv_heads, dk, dv, k_head_major, latent_queries):
    wq, wk, wv = n_heads * dk, kv_heads * dk, kv_heads * dv
    wo = n_heads * dv
    batch = geom.batch
    nc = geom.n_ctx
    ctx_blk0 = geom.n_lat // nc
    assert geom.n_lat % nc == 0

    def kspec(rows, tok_blk):
        if k_head_major:
            return pl.BlockSpec((kv_heads, rows, dk), lambda b, h, *_: (h, tok_blk(b), 0))
        return pl.BlockSpec((rows, wk), lambda b, h, *_: (tok_blk(b), h))

    def vspec(rows, tok_blk):
        return pl.BlockSpec((wv, rows), lambda b, h, *_: (h, tok_blk(b)))

    kern = functools.partial(_attn_kernel, n_heads=n_heads, kv_heads=kv_heads, dk=dk, dv=dv, has_lat=latent_queries)
    if latent_queries:
        tq = TT
        qt = geom.seq // tq
        return pl.pallas_call(
            kern,
            grid=(batch, groups, qt),
            in_specs=[
                pl.BlockSpec((tq, wq), lambda b, h, t: (b * qt + t, h)),
                kspec(geom.seq, lambda b: b),
                vspec(geom.seq, lambda b: b),
                kspec(nc, lambda b: ctx_blk0 + b),
                vspec(nc, lambda b: ctx_blk0 + b),
            ],
            out_specs=pl.BlockSpec((tq, wo), lambda b, h, t: (b * qt + t, h)),
            out_shape=jax.ShapeDtypeStruct((geom.n_lat, groups * wo), BF16),
            scratch_shapes=[pltpu.VMEM((2, geom.seq + nc, tq), F32)],
            compiler_params=_cparams(("parallel", "parallel", "arbitrary")),
            name="attn_lat",
        )(q, k, vt, k, vt)
    return pl.pallas_call(
        kern,
        grid=(batch, groups),
        in_specs=[
            pl.BlockSpec((nc, wq), lambda b, h: (ctx_blk0 + b, h)),
            kspec(nc, lambda b: ctx_blk0 + b),
            vspec(nc, lambda b: ctx_blk0 + b),
        ],
        out_specs=pl.BlockSpec((nc, wo), lambda b, h: (b, h)),
        out_shape=jax.ShapeDtypeStruct((batch * nc, groups * wo), BF16),
        scratch_shapes=[pltpu.VMEM((2, nc, nc), F32)],
        compiler_params=_cparams(("parallel", "parallel")),
        name="attn_ctx",
    )(q, k, vt)


def _post_kernel(o_ref, wo_ref, s_ref, mod_ref, g_ref, wrh_ref, wrl_ref, br_ref, s_out, h2_out, idx_out, w_out, cnt_out, *, n_experts):
    m = mod_ref[...]
    s = s_ref[...] + m[2:3] * jnp.dot(o_ref[...], wo_ref[...], preferred_element_type=F32)
    s_out[...] = s
    h2 = _norm_mod(s, g_ref[...], m[3:4], m[4:5])
    h2_out[...] = h2.astype(BF16)
    h2_hi, h2_lo = _split_hi_lo(h2)
    logits = (jnp.dot(h2_hi, wrh_ref[...], preferred_element_type=F32) + jnp.dot(h2_lo, wrh_ref[...], preferred_element_type=F32)
              + jnp.dot(h2_hi, wrl_ref[...], preferred_element_type=F32) + br_ref[...])
    lane = lax.broadcasted_iota(I32, logits.shape, 1).astype(F32)
    work = jnp.where(lane < n_experts, logits, -jnp.inf)
    vals, hits = [], []
    for k in range(TOP_K):
        mx = jnp.max(work, axis=-1, keepdims=True)
        sel = jnp.min(jnp.where(work == mx, lane, float(LANES)), axis=-1, keepdims=True)
        hit = lane == sel
        vals.append(mx)
        hits.append(hit)
        work = jnp.where(hit, -jnp.inf, work)
    es = [jnp.exp(v - vals[0]) for v in vals]
    den = es[0] + es[1] + es[2] + es[3]
    w_slab = jnp.zeros(logits.shape, F32)
    for k in range(TOP_K):
        w_slab = jnp.where(lane == k, es[k] / den, w_slab)
    w_out[...] = w_slab
    onehot = jnp.where(hits[0] | hits[1] | hits[2] | hits[3], 1.0, 0.0)
    tr = lax.broadcasted_iota(I32, (TT, TT), 0)
    tc = lax.broadcasted_iota(I32, (TT, TT), 1)
    rank = jnp.dot(jnp.where(tc < tr, 1.0, 0.0).astype(BF16), onehot.astype(BF16), preferred_element_type=F32)
    cnt = jnp.sum(onehot, axis=0, keepdims=True)
    cp = jnp.floor((cnt + (SUBLANES - 1)) * (1.0 / SUBLANES)) * SUBLANES
    er = lax.broadcasted_iota(I32, (LANES, LANES), 0)
    ec = lax.broadcasted_iota(I32, (LANES, LANES), 1)
    seg = jnp.dot(jnp.broadcast_to(cp, (SUBLANES, LANES)).astype(BF16), jnp.where(er < ec, 1.0, 0.0).astype(BF16),
                  preferred_element_type=F32)[0:1]
    base = seg + rank
    lp_slab = jnp.zeros(logits.shape, F32)
    for k in range(TOP_K):
        lp_slab = jnp.where(lane == k, jnp.sum(jnp.where(hits[k], base, 0.0), axis=-1, keepdims=True), lp_slab)
    idx_out[...] = lp_slab.astype(I32)
    cnt_out[...] = jnp.broadcast_to(cnt, (SUBLANES, LANES)).astype(I32)


def _post(o, w_o, stream, mods4, g_ffn, w_router, b_router, geom, layer, n_tiles):
    d = stream.shape[1]
    wo_w = o.shape[1]
    n_experts = w_router.shape[1]
    wr_hi, wr_lo = _split_hi_lo(jnp.concatenate([w_router, jnp.zeros((d, LANES - n_experts), F32)], axis=1))
    br = jnp.concatenate([b_router, jnp.zeros((LANES - n_experts,), F32)]).reshape(1, LANES)
    n = n_tiles * TT
    return pl.pallas_call(
        functools.partial(_post_kernel, n_experts=n_experts),
        grid=(n_tiles,),
        in_specs=[
            pl.BlockSpec((TT, wo_w), lambda j: (j, 0)),
            pl.BlockSpec((wo_w, d), lambda j: (0, 0)),
            pl.BlockSpec((TT, d), lambda j: (j, 0)),
            _mod_spec(geom, layer, d),
            pl.BlockSpec((1, d), lambda j: (0, 0)),
            pl.BlockSpec((d, LANES), lambda j: (0, 0)),
            pl.BlockSpec((d, LANES), lambda j: (0, 0)),
            pl.BlockSpec((1, LANES), lambda j: (0, 0)),
        ],
        out_specs=[
            pl.BlockSpec((TT, d), lambda j: (j, 0)),
            pl.BlockSpec((TT, d), lambda j: (j, 0)),
            pl.BlockSpec((TT, LANES), lambda j: (j, 0)),
            pl.BlockSpec((TT, LANES), lambda j: (j, 0)),
            pl.BlockSpec((None, SUBLANES, LANES), lambda j: (j, 0, 0)),
        ],
        out_shape=[
            jax.ShapeDtypeStruct((n, d), F32),
            jax.ShapeDtypeStruct((n, d), BF16),
            jax.ShapeDtypeStruct((n, LANES), I32),
            jax.ShapeDtypeStruct((n, LANES), F32),
            jax.ShapeDtypeStruct((n_tiles, SUBLANES, LANES), I32),
        ],
        compiler_params=_cparams(("parallel",)),
        name="post",
    )(o, w_o, stream, mods4, g_ffn, wr_hi, wr_lo, br)


def _local_rows(n_experts):
    return _round_up(TT * TOP_K + n_experts * (SUBLANES - 1), LANES)


def _n_blocks(n_tok, n_tiles, n_experts):
    rows = n_tok * TOP_K + n_experts * n_tiles * (SUBLANES - 1) + n_experts * (EXPERT_BLOCK - 1)
    return pl.cdiv(rows, EXPERT_BLOCK)


def _route_meta(cnt, lpos, n_experts):
    n_tiles = cnt.shape[0]
    n_blocks = _n_blocks(n_tiles * TT, n_tiles, n_experts)
    cp = _round_up(cnt, SUBLANES)
    seg = jnp.cumsum(cp, axis=1) - cp
    run_rows = cp.sum(axis=0)
    reg = _round_up(run_rows, EXPERT_BLOCK)
    reg_end = jnp.cumsum(reg)
    reg_start = reg_end - reg
    off = reg_start[None, :] + jnp.cumsum(cp, axis=0) - cp
    n_used = (reg_end[-1] // EXPERT_BLOCK).astype(I32)
    bstart = jnp.arange(n_blocks, dtype=I32) * EXPERT_BLOCK
    last_start = jnp.maximum(n_used - 1, 0) * EXPERT_BLOCK
    be = (jnp.minimum(bstart, last_start)[:, None] >= reg_end[None, :]).sum(axis=1).astype(I32)
    be = jnp.minimum(be, n_experts - 1)
    per_big = cp // BIG_CHUNK
    per_small = (cp % BIG_CHUNK) // SUBLANES
    tail = per_big * BIG_CHUNK
    return dict(
        big_list=_copy_list(per_big, seg, off, BIG_CHUNK, _max_big(n_experts)),
        small_list=_copy_list(per_small, seg + tail, off + tail, SUBLANES, _max_small(n_experts)),
        n_big=per_big.sum(axis=1).astype(I32), n_small=per_small.sum(axis=1).astype(I32),
        gap_start=(reg_start + run_rows).astype(I32), gap=(reg - run_rows).astype(I32),
        lpos=lpos.astype(I32), block_expert=be, n_used=n_used.reshape(1), n_blocks=n_blocks,
    )


def _max_big(n_experts):
    return _local_rows(n_experts) // BIG_CHUNK


def _max_small(n_experts):
    return n_experts * (BIG_CHUNK // SUBLANES - 1)


def _copy_list(per_run, local_row, sorted_row, chunk, max_copies):
    ends = jnp.cumsum(per_run, axis=1)[:, None, :]
    first = ends - per_run[:, None, :]
    k = jnp.arange(max_copies, dtype=I32)[None, :, None]
    mine = (first <= k) & (k < ends)
    pick = lambda v: jnp.sum(jnp.where(mine, v[:, None, :], 0), axis=2)
    j = k[:, :, 0] - pick(first[:, 0, :])
    src = pick(local_row) + chunk * j
    dst = pick(sorted_row) + chunk * j
    return (dst * LIST_RADIX + src).reshape(-1).astype(I32)


def _run_copies(tile, nbig_ref, nsmall_ref, big_ref, small_ref, max_big, max_small, make_big, make_small):
    def start(make, packed):
        local_row = pl.multiple_of(packed & (LIST_RADIX - 1), SUBLANES)
        sorted_row = pl.multiple_of(lax.shift_right_logical(packed, LIST_SHIFT), SUBLANES)
        make(local_row, sorted_row).start()

    def big(k, c):
        start(make_big, big_ref[tile * max_big + k])
        return c

    def small(k, c):
        start(make_small, small_ref[tile * max_small + k])
        return c

    lax.fori_loop(0, nbig_ref[tile], big, 0)
    lax.fori_loop(0, nsmall_ref[tile], small, 0)


def _wait_copies(tile, nbig_ref, nsmall_ref, make_big, make_small):
    def wb(j, c):
        make_big(0, 0).wait()
        return c

    def ws(j, c):
        make_small(0, 0).wait()
        return c

    lax.fori_loop(0, nbig_ref[tile], wb, 0)
    lax.fori_loop(0, nsmall_ref[tile], ws, 0)


def _dispatch_kernel(big_ref, small_ref, nbig_ref, nsmall_ref, gs_ref, gap_ref, nu_ref,
                     lpt_ref, h2_ref, xs_ref, buf_ref, zero_ref, sem, zsem, *, n_experts, local_rows, n_blocks):
    i = pl.program_id(0)
    n = pl.num_programs(0)
    slot = i % 2

    def big(src, dst):
        return pltpu.make_async_copy(buf_ref.at[slot, pl.ds(src, BIG_CHUNK), :], xs_ref.at[pl.ds(dst, BIG_CHUNK), :], sem.at[slot])

    def small(src, dst):
        return pltpu.make_async_copy(buf_ref.at[slot, pl.ds(src, SUBLANES), :], xs_ref.at[pl.ds(dst, SUBLANES), :], sem.at[slot])

    @pl.when(i >= 2)
    def _():
        _wait_copies(i - 2, nbig_ref, nsmall_ref, big, small)

    lpt = lpt_ref[...]
    rows = lax.broadcasted_iota(I32, (local_rows, TT), 0)
    hit = rows == lpt[0:1, :]
    for k in range(1, TOP_K):
        hit = hit | (rows == lpt[k:k + 1, :])
    p = jnp.where(hit, 1.0, 0.0).astype(BF16)
    buf_ref[slot] = jnp.dot(p, h2_ref[...], preferred_element_type=F32)
    _run_copies(i, nbig_ref, nsmall_ref, big_ref, small_ref, _max_big(n_experts), _max_small(n_experts), big, small)

    @pl.when(i == n - 1)
    def _():
        zero_ref[...] = jnp.zeros_like(zero_ref)
        sizes = []
        size = EXPERT_BLOCK // 2
        while size >= SUBLANES:
            sizes.append(size)
            size //= 2

        def zcopy(dst, size):
            return pltpu.make_async_copy(zero_ref.at[pl.ds(0, size), :], xs_ref.at[pl.ds(dst, size), :], zsem)

        def per_expert(e, carry):
            gap = gap_ref[e]
            pos = gs_ref[e]
            for size in sizes:
                take = (gap & size) != 0

                @pl.when(take)
                def _():
                    zcopy(pl.multiple_of(pos, SUBLANES), size).start()

                pos = pos + jnp.where(take, size, 0)
            return carry

        lax.fori_loop(0, n_experts, per_expert, 0)

        zrows = zero_ref.shape[0]
        per_block = EXPERT_BLOCK // zrows
        tail_copies = (n_blocks - nu_ref[0]) * per_block

        def tail(t, carry):
            zcopy(pl.multiple_of(nu_ref[0] * EXPERT_BLOCK + t * zrows, SUBLANES), zrows).start()
            return carry

        lax.fori_loop(0, tail_copies, tail, 0)

        def per_expert_wait(e, carry):
            gap = gap_ref[e]
            for size in sizes:
                @pl.when((gap & size) != 0)
                def _():
                    zcopy(0, size).wait()
            return carry

        lax.fori_loop(0, n_experts, per_expert_wait, 0)

        def tail_wait(t, carry):
            zcopy(0, zrows).wait()
            return carry

        lax.fori_loop(0, tail_copies, tail_wait, 0)

        @pl.when(i >= 1)
        def _():
            def big_o(src, dst):
                return pltpu.make_async_copy(buf_ref.at[1 - slot, pl.ds(src, BIG_CHUNK), :], xs_ref.at[pl.ds(dst, BIG_CHUNK), :], sem.at[1 - slot])

            def small_o(src, dst):
                return pltpu.make_async_copy(buf_ref.at[1 - slot, pl.ds(src, SUBLANES), :], xs_ref.at[pl.ds(dst, SUBLANES), :], sem.at[1 - slot])

            _wait_copies(i - 1, nbig_ref, nsmall_ref, big_o, small_o)

        _wait_copies(i, nbig_ref, nsmall_ref, big, small)


def _dispatch(h2, meta, n_tiles, n_experts):
    n, d = h2.shape
    local_rows = _local_rows(n_experts)
    n_rows = meta["n_blocks"] * EXPERT_BLOCK
    lpt = meta["lpos"].reshape(n_tiles, TT, TOP_K).transpose(0, 2, 1)
    grid_spec = pltpu.PrefetchScalarGridSpec(
        num_scalar_prefetch=7,
        grid=(n_tiles,),
        in_specs=[
            pl.BlockSpec((None, TOP_K, TT), lambda j, *_: (j, 0, 0)),
            pl.BlockSpec((TT, d), lambda j, *_: (j, 0)),
        ],
        out_specs=pl.BlockSpec(memory_space=pl.ANY),
        scratch_shapes=[
            pltpu.VMEM((2, local_rows, d), F32),
            pltpu.VMEM((EXPERT_BLOCK // 2, d), F32),
            pltpu.SemaphoreType.DMA((2,)),
            pltpu.SemaphoreType.DMA(()),
        ],
    )
    return pl.pallas_call(
        functools.partial(_dispatch_kernel, n_experts=n_experts, local_rows=local_rows, n_blocks=meta["n_blocks"]),
        grid_spec=grid_spec,
        out_shape=jax.ShapeDtypeStruct((n_rows, d), F32),
        compiler_params=_cparams(("arbitrary",)),
        name="moe_dispatch",
    )(meta["big_list"], meta["small_list"], meta["n_big"], meta["n_small"], meta["gap_start"], meta["gap"], meta["n_used"],
      lpt, h2)


def _moe_prep_kernel(wgu_ref, wd_ref, sel_ref, wg_out, wu_out, wd_out):
    sel = sel_ref[...]
    pair = 2 * LANES
    for c in range(wgu_ref.shape[1] // pair):
        chunk = wgu_ref[:, c * pair:(c + 1) * pair].astype(BF16)
        de = jnp.dot(chunk, sel, preferred_element_type=F32)
        wg_out[:, c * LANES:(c + 1) * LANES] = de[:, :LANES].astype(BF16)
        wu_out[:, c * LANES:(c + 1) * LANES] = de[:, LANES:].astype(BF16)
    wd_out[...] = wd_ref[...].astype(BF16)


def _moe_prep(w_gu, w_down):
    depth, n_experts, d, f2 = w_gu.shape
    f = f2 // 2
    r = jnp.arange(2 * LANES)
    src = jnp.where(r < LANES, 2 * r, 2 * (r - LANES) + 1)
    sel = (r[:, None] == src[None, :]).astype(BF16)
    wspec = lambda rows, cols: pl.BlockSpec((None, None, rows, cols), lambda i, e: (i, e, 0, 0))
    return pl.pallas_call(
        _moe_prep_kernel,
        grid=(depth, n_experts),
        in_specs=[wspec(d, f2), wspec(f, d), pl.BlockSpec((2 * LANES, 2 * LANES), lambda i, e: (0, 0))],
        out_specs=[wspec(d, f), wspec(d, f), wspec(f, d)],
        out_shape=[
            jax.ShapeDtypeStruct((depth, n_experts, d, f), BF16),
            jax.ShapeDtypeStruct((depth, n_experts, d, f), BF16),
            jax.ShapeDtypeStruct((depth, n_experts, f, d), BF16),
        ],
        compiler_params=_cparams(("parallel", "parallel")),
        name="moe_prep",
    )(w_gu, w_down, sel)


def _ffn_kernel(be_ref, nu_ref, x_ref, wg_ref, wu_ref, wd_ref, bg_ref, bu_ref, bd_ref, y_ref):
    b = pl.program_id(0)

    @pl.when(b < nu_ref[0])
    def _():
        x = x_ref[...].astype(BF16)
        gate = jnp.dot(x, wg_ref[...], preferred_element_type=F32) + bg_ref[...]
        up = jnp.dot(x, wu_ref[...], preferred_element_type=F32) + bu_ref[...]
        gate = jnp.minimum(gate, SWIGLU_LIMIT)
        up = jnp.clip(up, -SWIGLU_LIMIT, SWIGLU_LIMIT)
        glu = gate / (1.0 + jnp.exp(-SWIGLU_ALPHA * gate))
        act = ((up + 1.0) * glu).astype(BF16)
        y_ref[...] = jnp.dot(act, wd_ref[...], preferred_element_type=F32) + bd_ref[...]

    @pl.when(b >= nu_ref[0])
    def _():
        y_ref[...] = jnp.zeros_like(y_ref)


def _ffn(xs, meta, layer, w_gate, w_up, w_down, b_gate, b_up, b_down):
    n_rows, d = xs.shape
    n_blocks = meta["n_blocks"]
    f = w_gate.shape[3]
    xmap = lambda b, be, nu: (jnp.minimum(b, jnp.maximum(nu[0] - 1, 0)), 0)
    wmap = lambda b, be, nu: (layer, be[b], 0, 0)
    grid_spec = pltpu.PrefetchScalarGridSpec(
        num_scalar_prefetch=2,
        grid=(n_blocks,),
        in_specs=[
            pl.BlockSpec((EXPERT_BLOCK, d), xmap),
            pl.BlockSpec((None, None, d, f), wmap),
            pl.BlockSpec((None, None, d, f), wmap),
            pl.BlockSpec((None, None, f, d), wmap),
            pl.BlockSpec((None, None, 1, f), wmap),
            pl.BlockSpec((None, None, 1, f), wmap),
            pl.BlockSpec((None, None, 1, d), wmap),
        ],
        out_specs=pl.BlockSpec((EXPERT_BLOCK, d), lambda b, be, nu: (b, 0)),
    )
    return pl.pallas_call(
        _ffn_kernel,
        grid_spec=grid_spec,
        out_shape=jax.ShapeDtypeStruct((n_rows, d), F32),
        compiler_params=_cparams(("arbitrary",)),
        name="moe_ffn",
    )(meta["block_expert"], meta["n_used"], xs, w_gate, w_up, w_down, b_gate, b_up, b_down)


def _combine_kernel(big_ref, small_ref, nbig_ref, nsmall_ref,
                    lp_ref, w_ref, s_ref, mod_ref, gf_ref, ys_ref, o_ref, buf_ref, sem, *, n_experts, local_rows, final):
    i = pl.program_id(0)
    n = pl.num_programs(0)
    slot = i % 2

    def copies(sl):
        def big(dst, src):
            return pltpu.make_async_copy(ys_ref.at[pl.ds(src, BIG_CHUNK), :], buf_ref.at[sl, pl.ds(dst, BIG_CHUNK), :], sem.at[sl])

        def small(dst, src):
            return pltpu.make_async_copy(ys_ref.at[pl.ds(src, SUBLANES), :], buf_ref.at[sl, pl.ds(dst, SUBLANES), :], sem.at[sl])

        return big, small

    def start_copies(tile, sl):
        _run_copies(tile, nbig_ref, nsmall_ref, big_ref, small_ref, _max_big(n_experts), _max_small(n_experts), *copies(sl))

    @pl.when(i == 0)
    def _():
        buf_ref[...] = jnp.zeros_like(buf_ref)
        start_copies(0, 0)

    @pl.when(i + 1 < n)
    def _():
        start_copies(i + 1, 1 - slot)

    _wait_copies(i, nbig_ref, nsmall_ref, *copies(slot))

    lp = lp_ref[...]
    w = w_ref[...]
    lanes = lax.broadcasted_iota(I32, (TT, local_rows), 1)
    pw = jnp.zeros((TT, local_rows), F32)
    for k in range(TOP_K):
        pw = pw + jnp.where(lanes == lp[:, k:k + 1], w[:, k:k + 1], 0.0)
    y = buf_ref[slot].astype(BF16)
    f = jnp.dot(pw.astype(BF16), y, preferred_element_type=F32)
    m = mod_ref[...]
    s = s_ref[...] + m[5:6] * f
    if final:
        s = _rms(s) * gf_ref[...]
    o_ref[...] = s


def _combine(ys, meta, top_w, stream, mods4, g_final, geom, layer, n_tiles, n_experts, final):
    d = stream.shape[1]
    local_rows = _local_rows(n_experts)
    grid_spec = pltpu.PrefetchScalarGridSpec(
        num_scalar_prefetch=4,
        grid=(n_tiles,),
        in_specs=[
            pl.BlockSpec((TT, TOP_K), lambda j, *_: (j, 0)),
            pl.BlockSpec((TT, TOP_K), lambda j, *_: (j, 0)),
            pl.BlockSpec((TT, d), lambda j, *_: (j, 0)),
            pl.BlockSpec((None, None, N_MOD, d), lambda j, *_: (layer, geom.group(j), 0, 0)),
            pl.BlockSpec((1, d), lambda j, *_: (0, 0)),
            pl.BlockSpec(memory_space=pl.ANY),
        ],
        out_specs=pl.BlockSpec((TT, d), lambda j, *_: (j, 0)),
        scratch_shapes=[
            pltpu.VMEM((2, local_rows, d), F32),
            pltpu.SemaphoreType.DMA((2,)),
        ],
    )
    return pl.pallas_call(
        functools.partial(_combine_kernel, n_experts=n_experts, local_rows=local_rows, final=final),
        grid_spec=grid_spec,
        out_shape=jax.ShapeDtypeStruct((n_tiles * TT, d), F32),
        compiler_params=_cparams(("arbitrary",)),
        name="moe_combine",
    )(meta["big_list"], meta["small_list"], meta["n_big"], meta["n_small"],
      meta["lpos"], top_w, stream, mods4, g_final, ys)


def kernel(x, c, ctx, c_ctx, w_mod, b_mod, g_mix, g_ffn, g_final, f_w_in, f_w_out, mla_w_in, mla_g_qa, mla_w_qb,
           mla_g_kva, mla_w_kvb, mla_w_o, gqa_w_qkv, gqa_g_q, gqa_g_k, gqa_w_o, moe_w_router, moe_b_router,
           moe_w_gu, moe_b_gu, moe_w_down, moe_b_down):
    batch, seq, d = x.shape
    n_ctx = ctx.shape[1]
    depth = w_mod.shape[0]
    n_experts = moe_w_router.shape[2]
    geom = _Geom(batch, seq, n_ctx)

    n_groups = _round_up(batch + 1, SUBLANES)
    cc = jnp.concatenate([c, c_ctx[None, :], jnp.zeros((n_groups - batch - 1, d), F32)], axis=0)
    mods4 = _mods(cc, w_mod, b_mod).reshape(depth, n_groups, N_MOD, d)

    stream = jnp.concatenate([x.reshape(batch * seq, d), ctx.reshape(batch * n_ctx, d)], axis=0)

    gd = d // FOURIER_GROUPS
    gcos, gsin = _dft_tables(gd)
    eye = jnp.eye(FOURIER_GROUPS, dtype=F32)
    bd_cs = jnp.concatenate([jnp.kron(eye, gcos), jnp.kron(eye, gsin)], axis=1)

    f = moe_w_down.shape[2]
    w_gate, w_up, w_down = _moe_prep(moe_w_gu, moe_w_down)
    b_gate = moe_b_gu[:, :, 0::2].reshape(depth, n_experts, 1, f)
    b_up = moe_b_gu[:, :, 1::2].reshape(depth, n_experts, 1, f)
    b_down = moe_b_down.reshape(depth, n_experts, 1, d)

    for i in range(depth):
        kind, j = i % N_MIXERS, i // N_MIXERS
        last = i == depth - 1
        ctx_used = (kind != 0) or (not last)
        n_tiles = geom.n_tiles if ctx_used else geom.n_lat_tiles
        gm = g_mix[i].reshape(1, d)
        gf = g_ffn[i].reshape(1, d)

        if kind == 0:
            wcs = _fold(f_w_in[j], bd_cs, BF16)
            u = _pre_fourier(stream, mods4, gm, wcs, geom, i, n_tiles * TT)
            o = _dft(u, 0, seq, batch, d)
            if ctx_used:
                o = jnp.concatenate([o, _dft(u, geom.n_lat, n_ctx, batch, d)], axis=0)
            w_o = f_w_out[j].astype(BF16)
        elif kind == 1:
            q, k, v = _pre_mla(stream, mods4, gm, mla_w_in[j], mla_g_qa[j], mla_w_qb[j], mla_g_kva[j], mla_w_kvb[j], geom, i)
            att = functools.partial(_attention, q, k, v, geom, groups=2, n_heads=MLA_HEADS // 2, kv_heads=MLA_HEADS // 2,
                                    dk=MLA_SLOT, dv=MLA_V, k_head_major=False)
            o = att(latent_queries=True)
            if not last:
                o = jnp.concatenate([o, att(latent_queries=False)], axis=0)
            w_o = mla_w_o[j].astype(BF16)
        else:
            q, k, v = _pre_gqa(stream, mods4, gm, gqa_w_qkv[j], gqa_g_q[j], gqa_g_k[j], geom, i)
            att = functools.partial(_attention, q, k, v, geom, groups=2, n_heads=GQA_HEADS // 2, kv_heads=GQA_KV_HEADS // 2,
                                    dk=GQA_HEAD_DIM, dv=GQA_HEAD_DIM, k_head_major=True)
            o = att(latent_queries=True)
            if not last:
                o = jnp.concatenate([o, att(latent_queries=False)], axis=0)
            w_o = gqa_w_o[j].astype(BF16)

        n_moe_tiles = geom.n_lat_tiles if last else geom.n_tiles
        stream, h2, lp_slab, w_slab, cnt_slab = _post(o, w_o, stream, mods4, gf, moe_w_router[i], moe_b_router[i], geom, i,
                                                      n_moe_tiles)
        meta = _route_meta(cnt_slab[:, 0, :n_experts], lp_slab[:, :TOP_K], n_experts)
        xs = _dispatch(h2, meta, n_moe_tiles, n_experts)
        ys = _ffn(xs, meta, i, w_gate, w_up, w_down, b_gate, b_up, b_down)
        stream = _combine(ys, meta, w_slab[:, :TOP_K], stream, mods4, g_final.reshape(1, d), geom, i, n_moe_tiles, n_experts,
                          last)

    return stream[:batch * seq].reshape(batch, seq, d)
```

```python
import functools

import jax
import jax.numpy as jnp
from jax import lax
from jax.experimental import pallas as pl
from jax.experimental.pallas import tpu as pltpu

F32 = jnp.float32
BF16 = jnp.bfloat16
I32 = jnp.int32
HI = lax.Precision.HIGHEST

GRID_W = 64
N_MIXERS = 3
NORM_EPS = 1e-6
ROPE_THETA = 10000.0
FOURIER_GROUPS = 8
MLA_HEADS = 16
MLA_Q_LORA = 384
MLA_KV_LORA = 256
MLA_NOPE = 64
MLA_ROPE = 32
MLA_V = 64
GQA_HEADS = 16
GQA_KV_HEADS = 4
GQA_HEAD_DIM = 64
TOP_K = 4
SWIGLU_LIMIT = 7.0
SWIGLU_ALPHA = 1.702
N_MOD = 6

LANES = 128
SUBLANES = 8
TT = 256
PROJ_TILE = 512
KEY_CHUNK = 1024
LOG2_E = 1.4426950408889634
EXPERT_BLOCK = 512
BIG_CHUNK = 32
LIST_SHIFT = 11
LIST_RADIX = 1 << LIST_SHIFT
MLA_SLOT = 128
VMEM_LIMIT = 56 * 1024 * 1024


def _cparams(sem):
    return pltpu.CompilerParams(dimension_semantics=sem, vmem_limit_bytes=VMEM_LIMIT)


def _round_up(x, m):
    return (x + m - 1) // m * m


def _norm_mod(x, g, shift, scale):
    y = x * lax.rsqrt(jnp.mean(x * x, axis=-1, keepdims=True) + NORM_EPS) * g
    return y * (1.0 + scale) + shift


def _rms(x):
    return x * lax.rsqrt(jnp.mean(x * x, axis=-1, keepdims=True) + NORM_EPS)


def _mod_kernel(cc_ref, w_ref, b_ref, o_ref):
    cc = cc_ref[...]
    a = cc / (1.0 + jnp.exp(-cc))
    o_ref[...] = jnp.dot(a, w_ref[...], precision=HI, preferred_element_type=F32) + b_ref[...]


def _mods(cc, w_mod, b_mod):
    depth, d, n6 = w_mod.shape
    g = cc.shape[0]
    tn = 1536 if n6 % 1536 == 0 else n6
    return pl.pallas_call(
        _mod_kernel,
        grid=(depth, n6 // tn),
        in_specs=[
            pl.BlockSpec((g, d), lambda i, n: (0, 0)),
            pl.BlockSpec((None, d, tn), lambda i, n: (i, 0, n)),
            pl.BlockSpec((None, 1, tn), lambda i, n: (i, 0, n)),
        ],
        out_specs=pl.BlockSpec((None, g, tn), lambda i, n: (i, 0, n)),
        out_shape=jax.ShapeDtypeStruct((depth, g, n6), F32),
        compiler_params=_cparams(("parallel", "parallel")),
        name="mods",
    )(cc, w_mod, b_mod.reshape(depth, 1, n6))


def _fold_kernel(a_ref, b_ref, o_ref):
    o_ref[...] = jnp.dot(a_ref[...], b_ref[...], precision=HI, preferred_element_type=F32).astype(o_ref.dtype)


def _fold(a, b, out_dtype):
    m, k = a.shape
    n = b.shape[1]
    tn = 512
    return pl.pallas_call(
        _fold_kernel,
        grid=(n // tn,),
        in_specs=[pl.BlockSpec((m, k), lambda j: (0, 0)), pl.BlockSpec((k, tn), lambda j: (0, j))],
        out_specs=pl.BlockSpec((m, tn), lambda j: (0, j)),
        out_shape=jax.ShapeDtypeStruct((m, n), out_dtype),
        compiler_params=_cparams(("parallel",)),
        name="fold",
    )(a, b)


class _Geom:
    def __init__(self, batch, seq, n_ctx):
        self.batch, self.seq, self.n_ctx = batch, seq, n_ctx
        assert seq % TT == 0 and n_ctx % TT == 0 and seq % GRID_W == 0
        self.lat_per_b = seq // TT
        self.ctx_per_b = n_ctx // TT
        self.n_lat_tiles = batch * self.lat_per_b
        self.n_ctx_tiles = batch * self.ctx_per_b
        self.n_tiles = self.n_lat_tiles + self.n_ctx_tiles
        self.n_lat = batch * seq
        self.n_tok = self.n_lat + batch * n_ctx

    def group(self, j, tile=TT):
        return jnp.where(j < self.n_lat // tile, j // (self.seq // tile), self.batch)

    def pos_tile(self, j, tile=TT):
        return jnp.where(j < self.n_lat // tile, j % (self.seq // tile), self.seq // tile)


def _mod_spec(geom, layer, d, tile=TT):
    return pl.BlockSpec((None, None, N_MOD, d), lambda j: (layer, geom.group(j, tile), 0, 0))


FOURIER_TILE = 1024


def _pre_fourier_kernel(s_ref, mod_ref, g_ref, perm_ref, w_ref, u_ref):
    m = mod_ref[...]
    h = _norm_mod(s_ref[...], g_ref[...], m[0:1], m[1:2]).astype(BF16)
    h = jnp.dot(perm_ref[...], h, preferred_element_type=F32).astype(BF16)
    u_ref[...] = jnp.dot(h, w_ref[...], preferred_element_type=F32).astype(BF16)


def _pre_fourier(stream, mods4, g_mix, wcs, geom, layer, n_tok):
    d = stream.shape[1]
    assert n_tok % FOURIER_TILE == 0 and geom.seq % FOURIER_TILE == 0 and FOURIER_TILE % geom.n_ctx == 0
    lat_tiles = geom.n_lat // FOURIER_TILE
    per_b = geom.seq // FOURIER_TILE
    group = lambda j: jnp.where(j < lat_tiles, j // per_b, geom.batch)
    r = jnp.arange(FOURIER_TILE)
    half = FOURIER_TILE // 2
    src = jnp.where(r < half, 2 * r, 2 * (r - half) + 1)
    perm = (src[:, None] == r[None, :]).astype(BF16)
    return pl.pallas_call(
        _pre_fourier_kernel,
        grid=(n_tok // FOURIER_TILE,),
        in_specs=[
            pl.BlockSpec((FOURIER_TILE, d), lambda j: (j, 0)),
            pl.BlockSpec((None, None, N_MOD, d), lambda j: (layer, group(j), 0, 0)),
            pl.BlockSpec((1, d), lambda j: (0, 0)),
            pl.BlockSpec((FOURIER_TILE, FOURIER_TILE), lambda j: (0, 0)),
            pl.BlockSpec((d, 2 * d), lambda j: (0, 0)),
        ],
        out_specs=pl.BlockSpec((FOURIER_TILE, 2 * d), lambda j: (j, 0)),
        out_shape=jax.ShapeDtypeStruct((n_tok, 2 * d), BF16),
        compiler_params=_cparams(("parallel",)),
        name="pre_fourier",
    )(stream, mods4, g_mix, perm, wcs)


def _dft_kernel(ce_ref, se_ref, co_ref, so_ref, ue_ref, uo_ref, o_ref, acce_ref, acco_ref, *, d):
    k = pl.program_id(2)

    @pl.when(k == 0)
    def _():
        acce_ref[...] = jnp.zeros_like(acce_ref)
        acco_ref[...] = jnp.zeros_like(acco_ref)

    acce_ref[...] += (jnp.dot(ce_ref[...], ue_ref[:, :d], preferred_element_type=F32)
                      - jnp.dot(se_ref[...], ue_ref[:, d:], preferred_element_type=F32))
    acco_ref[...] += (jnp.dot(co_ref[...], uo_ref[:, :d], preferred_element_type=F32)
                      - jnp.dot(so_ref[...], uo_ref[:, d:], preferred_element_type=F32))

    @pl.when(k == pl.num_programs(2) - 1)
    def _():
        o_ref[0] = (acce_ref[...] + acco_ref[...]).astype(o_ref.dtype)
        o_ref[1] = (acce_ref[...] - acco_ref[...]).astype(o_ref.dtype)


def _dft_tables(n, rows=None, col_step=1, col_off=0):
    rows = n if rows is None else rows
    cols = n // col_step
    scale = n ** -0.5

    def tables(n_rows, step):
        k = lax.broadcasted_iota(I32, (n_rows, cols), 0) * step
        c = lax.broadcasted_iota(I32, (n_rows, cols), 1) * col_step + col_off
        ang = ((k * c) % n).astype(F32) * (2.0 * jnp.pi / n)
        return jnp.cos(ang), jnp.sin(ang)

    coarse = 64
    if rows <= 4 * coarse:
        c, s = tables(rows, 1)
        return c * scale, s * scale
    ca, sa = tables(rows // coarse, coarse)
    cb, sb = tables(coarse, 1)
    ca, sa = ca[:, None, :] * scale, sa[:, None, :] * scale
    cos = (ca * cb[None] - sa * sb[None]).reshape(rows, cols)
    sin = (sa * cb[None] + ca * sb[None]).reshape(rows, cols)
    return cos, sin


def _dft(u, tok0, seq, batch, d):
    half = seq // 2
    ce, se = _dft_tables(seq, half, 2, 0)
    co, so = _dft_tables(seq, half, 2, 1)
    tabs = [t.astype(BF16) for t in (ce, se, co, so)]
    kb = min(FOURIER_TILE // 2, half)
    ksteps = half // kb
    tm = min(half, 1024)
    per_tile = FOURIER_TILE // kb

    def even_block(b, k):
        tok = tok0 + b * seq + k * FOURIER_TILE
        return (tok // FOURIER_TILE) * per_tile + (tok % FOURIER_TILE) // (2 * kb)

    tspec = pl.BlockSpec((tm, kb), lambda b, m, k: (m, k))
    out = pl.pallas_call(
        functools.partial(_dft_kernel, d=d),
        grid=(batch, half // tm, ksteps),
        in_specs=[
            tspec, tspec, tspec, tspec,
            pl.BlockSpec((kb, 2 * d), lambda b, m, k: (even_block(b, k), 0)),
            pl.BlockSpec((kb, 2 * d), lambda b, m, k: (even_block(b, k) + per_tile // 2, 0)),
        ],
        out_specs=pl.BlockSpec((None, 2, tm, d), lambda b, m, k: (b, 0, m, 0)),
        out_shape=jax.ShapeDtypeStruct((batch, 2, half, d), BF16),
        scratch_shapes=[pltpu.VMEM((tm, d), F32), pltpu.VMEM((tm, d), F32)],
        compiler_params=_cparams(("parallel", "parallel", "arbitrary")),
        name="dft",
    )(*tabs, u, u)
    return out.reshape(batch * seq, d)


def _pre_mla_kernel(s_ref, mod_ref, g_ref, win_ref, gqa_ref, gkva_ref, wqb_ref, wqbs_ref, wkvk_ref, wkvv_ref,
                    cos_ref, sin_ref, q_ref, k_ref, v_ref, *, scale):
    m = mod_ref[...]
    h = _norm_mod(s_ref[...], g_ref[...], m[0:1], m[1:2])
    a = jnp.dot(h.astype(BF16), win_ref[...], preferred_element_type=F32)
    aq = (_rms(a[:, :MLA_Q_LORA]) * gqa_ref[...]).astype(BF16)
    ckv = (_rms(a[:, MLA_Q_LORA:MLA_Q_LORA + MLA_KV_LORA]) * gkva_ref[...]).astype(BF16)
    o = MLA_Q_LORA + MLA_KV_LORA
    cos = cos_ref[...]
    sin = sin_ref[...]
    kpe = a[:, o:o + MLA_SLOT] * cos + a[:, o + MLA_SLOT:o + 2 * MLA_SLOT] * sin
    cos_h = jnp.tile(cos, (1, MLA_HEADS))
    sin_h = jnp.tile(sin, (1, MLA_HEADS))
    q = jnp.dot(aq, wqb_ref[...], preferred_element_type=F32)
    qs = jnp.dot(aq, wqbs_ref[...], preferred_element_type=F32)
    q_ref[...] = ((q * cos_h + qs * sin_h) * scale).astype(BF16)
    kk = jnp.dot(ckv, wkvk_ref[...], preferred_element_type=F32)
    k_ref[...] = (kk + jnp.tile(kpe, (1, MLA_HEADS))).astype(BF16)
    v_ref[...] = lax.dot_general(wkvv_ref[...], ckv, (((1,), (1,)), ((), ())), preferred_element_type=F32).astype(BF16)


def _rope_perm(rot):
    sec = rot // 2
    half = sec // 2
    d = jnp.arange(rot)
    first = (d % sec) < half
    partner = jnp.where(first, d + half, d - half)
    sign = jnp.where(first, -1.0, 1.0).astype(F32)
    return partner, sign


def _rope_tables(seq, rot, tile):
    rows = seq // GRID_W
    row = jnp.repeat(jnp.arange(rows, dtype=F32), GRID_W)
    col = jnp.tile(jnp.arange(GRID_W, dtype=F32), rows)
    n_freq = rot // 4
    inv_freq = ROPE_THETA ** (-jnp.arange(n_freq, dtype=F32) / n_freq)
    ang = jnp.stack([row[:, None] * inv_freq, col[:, None] * inv_freq], axis=1)
    cos = jnp.cos(ang)
    sin = jnp.sin(ang)
    cos_full = jnp.concatenate([cos, cos], axis=-1).reshape(seq, rot)
    sin_full = jnp.concatenate([sin, sin], axis=-1).reshape(seq, rot)
    _, sign = _rope_perm(rot)
    sin_full = sin_full * sign
    ident_c = jnp.ones((tile, rot), F32)
    ident_s = jnp.zeros((tile, rot), F32)
    assert seq % tile == 0
    return jnp.concatenate([cos_full, ident_c], 0), jnp.concatenate([sin_full, ident_s], 0)


def _pre_mla(stream, mods4, g_mix, w_in, g_qa, w_qb, g_kva, w_kvb, geom, layer):
    d = stream.shape[1]
    hd = MLA_NOPE + MLA_ROPE
    partner, _ = _rope_perm(MLA_ROPE)
    o = MLA_Q_LORA + MLA_KV_LORA
    kpe_w = w_in[:, o:]
    z_lo = jnp.zeros((d, MLA_NOPE), F32)
    z_hi = jnp.zeros((d, MLA_SLOT - hd), F32)
    w_in_ext = jnp.concatenate([w_in[:, :o], z_lo, kpe_w, z_hi, z_lo, kpe_w[:, partner], z_hi], axis=1).astype(BF16)
    wq = w_qb.reshape(MLA_Q_LORA, MLA_HEADS, hd)
    zq = jnp.zeros((MLA_Q_LORA, MLA_HEADS, MLA_SLOT - hd), F32)
    wq_p = jnp.concatenate([wq, zq], axis=2).reshape(MLA_Q_LORA, MLA_HEADS * MLA_SLOT).astype(BF16)
    wq_s = jnp.concatenate([jnp.zeros_like(wq[:, :, :MLA_NOPE]), wq[:, :, MLA_NOPE:][:, :, partner], zq], axis=2)
    wq_s = wq_s.reshape(MLA_Q_LORA, MLA_HEADS * MLA_SLOT).astype(BF16)
    wkv = w_kvb.reshape(MLA_KV_LORA, MLA_HEADS, MLA_NOPE + MLA_V)
    zk = jnp.zeros((MLA_KV_LORA, MLA_HEADS, MLA_SLOT - MLA_NOPE), F32)
    wkv_k = jnp.concatenate([wkv[:, :, :MLA_NOPE], zk], axis=2).reshape(MLA_KV_LORA, MLA_HEADS * MLA_SLOT).astype(BF16)
    wkv_v = wkv[:, :, MLA_NOPE:].reshape(MLA_KV_LORA, MLA_HEADS * MLA_V).T.astype(BF16)
    cos, sin = _rope_tables(geom.seq, MLA_ROPE, PROJ_TILE)
    rows = cos.shape[0]
    cos_slot = jnp.concatenate([jnp.ones((rows, MLA_NOPE), F32), cos, jnp.ones((rows, MLA_SLOT - hd), F32)], axis=1)
    sin_slot = jnp.concatenate([jnp.zeros((rows, MLA_NOPE), F32), sin, jnp.zeros((rows, MLA_SLOT - hd), F32)], axis=1)
    n_tiles = geom.n_tok // PROJ_TILE
    wq_w = MLA_HEADS * MLA_SLOT
    wv_w = MLA_HEADS * MLA_V
    full = lambda a: pl.BlockSpec(a.shape, lambda j: (0,) * a.ndim)
    g_qa2, g_kva2 = g_qa.reshape(1, -1), g_kva.reshape(1, -1)
    return pl.pallas_call(
        functools.partial(_pre_mla_kernel, scale=float(hd) ** -0.5 * LOG2_E),
        grid=(n_tiles,),
        in_specs=[
            pl.BlockSpec((PROJ_TILE, d), lambda j: (j, 0)),
            _mod_spec(geom, layer, d, PROJ_TILE),
            pl.BlockSpec((1, d), lambda j: (0, 0)),
            full(w_in_ext), full(g_qa2), full(g_kva2), full(wq_p), full(wq_s), full(wkv_k), full(wkv_v),
            pl.BlockSpec((PROJ_TILE, MLA_SLOT), lambda j: (geom.pos_tile(j, PROJ_TILE), 0)),
            pl.BlockSpec((PROJ_TILE, MLA_SLOT), lambda j: (geom.pos_tile(j, PROJ_TILE), 0)),
        ],
        out_specs=[
            pl.BlockSpec((PROJ_TILE, wq_w), lambda j: (j, 0)),
            pl.BlockSpec((PROJ_TILE, wq_w), lambda j: (j, 0)),
            pl.BlockSpec((wv_w, PROJ_TILE), lambda j: (0, j)),
        ],
        out_shape=[
            jax.ShapeDtypeStruct((n_tiles * PROJ_TILE, wq_w), BF16),
            jax.ShapeDtypeStruct((n_tiles * PROJ_TILE, wq_w), BF16),
            jax.ShapeDtypeStruct((wv_w, n_tiles * PROJ_TILE), BF16),
        ],
        compiler_params=_cparams(("parallel",)),
        name="pre_mla",
    )(stream, mods4, g_mix, w_in_ext, g_qa2, g_kva2, wq_p, wq_s, wkv_k, wkv_v, cos_slot, sin_slot)


def _split_hi_lo(x):
    hi = x.astype(BF16)
    lo = (x - hi.astype(F32)).astype(BF16)
    return hi, lo


def _pre_gqa_kernel(s_ref, mod_ref, g_ref, w_ref, wvt_ref, bd_ref, cq_ref, sq_ref, ck_ref, sk_ref, q_ref, k_ref, v_ref, *,
                    scale):
    m = mod_ref[...]
    h = _norm_mod(s_ref[...], g_ref[...], m[0:1], m[1:2]).astype(BF16)
    a = jnp.dot(h, w_ref[...], preferred_element_type=F32)
    nq = GQA_HEADS * GQA_HEAD_DIM
    nkv = GQA_KV_HEADS * GQA_HEAD_DIM
    aq, ak = a[:, :nq], a[:, nq:nq + nkv]
    aqs, aks = a[:, nq + nkv:2 * nq + nkv], a[:, 2 * nq + nkv:]
    bd = bd_ref[...]

    def head_rs(x, width):
        hi, lo = _split_hi_lo(x * x)
        b = bd[:width, :width]
        ssq = jnp.dot(hi, b, preferred_element_type=F32) + jnp.dot(lo, b, preferred_element_type=F32)
        return lax.rsqrt(ssq * (1.0 / GQA_HEAD_DIM) + NORM_EPS)

    reps_q = nq // cq_ref.shape[1]
    reps_k = nkv // ck_ref.shape[1]
    q = head_rs(aq, nq) * (aq * jnp.tile(cq_ref[...], (1, reps_q)) + aqs * jnp.tile(sq_ref[...], (1, reps_q)))
    q_ref[...] = (q * scale).astype(BF16)
    k = head_rs(ak, nkv) * (ak * jnp.tile(ck_ref[...], (1, reps_k)) + aks * jnp.tile(sk_ref[...], (1, reps_k)))
    k = k.astype(BF16)
    for g in range(GQA_KV_HEADS):
        k_ref[g] = k[:, g * GQA_HEAD_DIM:(g + 1) * GQA_HEAD_DIM]
    v_ref[...] = lax.dot_general(wvt_ref[...], h, (((1,), (1,)), ((), ())), preferred_element_type=F32).astype(BF16)


def _pre_gqa(stream, mods4, g_mix, w_qkv, g_q, g_k, geom, layer):
    d = stream.shape[1]
    nq = GQA_HEADS * GQA_HEAD_DIM
    nkv = GQA_KV_HEADS * GQA_HEAD_DIM
    partner, _ = _rope_perm(GQA_HEAD_DIM)
    wq = w_qkv[:, :nq].reshape(d, GQA_HEADS, GQA_HEAD_DIM)
    wk = w_qkv[:, nq:nq + nkv].reshape(d, GQA_KV_HEADS, GQA_HEAD_DIM)
    w_ext = jnp.concatenate([w_qkv[:, :nq + nkv], wq[:, :, partner].reshape(d, nq), wk[:, :, partner].reshape(d, nkv)],
                            axis=1).astype(BF16)
    w_vt = w_qkv[:, nq + nkv:].T.astype(BF16)
    cos, sin = _rope_tables(geom.seq, GQA_HEAD_DIM, PROJ_TILE)
    per = LANES // GQA_HEAD_DIM
    cq = jnp.tile(cos * g_q[None, :], (1, per))
    sq = jnp.tile(sin * g_q[partner][None, :], (1, per))
    ck = jnp.tile(cos * g_k[None, :], (1, per))
    sk = jnp.tile(sin * g_k[partner][None, :], (1, per))
    hid = jnp.arange(nq) // GQA_HEAD_DIM
    bd = (hid[:, None] == hid[None, :]).astype(BF16)
    n_tiles = geom.n_tok // PROJ_TILE
    full = lambda a: pl.BlockSpec(a.shape, lambda j: (0,) * a.ndim)
    tab = pl.BlockSpec((PROJ_TILE, LANES), lambda j: (geom.pos_tile(j, PROJ_TILE), 0))
    return pl.pallas_call(
        functools.partial(_pre_gqa_kernel, scale=float(GQA_HEAD_DIM) ** -0.5 * LOG2_E),
        grid=(n_tiles,),
        in_specs=[
            pl.BlockSpec((PROJ_TILE, d), lambda j: (j, 0)),
            _mod_spec(geom, layer, d, PROJ_TILE),
            pl.BlockSpec((1, d), lambda j: (0, 0)),
            full(w_ext), full(w_vt), full(bd), tab, tab, tab, tab,
        ],
        out_specs=[
            pl.BlockSpec((PROJ_TILE, nq), lambda j: (j, 0)),
            pl.BlockSpec((GQA_KV_HEADS, PROJ_TILE, GQA_HEAD_DIM), lambda j: (0, j, 0)),
            pl.BlockSpec((nkv, PROJ_TILE), lambda j: (0, j)),
        ],
        out_shape=[
            jax.ShapeDtypeStruct((n_tiles * PROJ_TILE, nq), BF16),
            jax.ShapeDtypeStruct((GQA_KV_HEADS, n_tiles * PROJ_TILE, GQA_HEAD_DIM), BF16),
            jax.ShapeDtypeStruct((nkv, n_tiles * PROJ_TILE), BF16),
        ],
        compiler_params=_cparams(("parallel",)),
        name="pre_gqa",
    )(stream, mods4, g_mix, w_ext, w_vt, bd, cq, sq, ck, sk)


def _attn_kernel(*refs, n_heads, kv_heads, dk, dv, has_lat):
    if has_lat:
        q_ref, kl_ref, vl_ref, kc_ref, vc_ref, o_ref, s_ref = refs
    else:
        q_ref, kc_ref, vc_ref, o_ref, s_ref = refs
    nt = (((1,), (1,)), ((), ()))
    tq = q_ref.shape[0]
    chunks, row = [], 0
    for k_ref, v_ref in ([(kc_ref, vc_ref), (kl_ref, vl_ref)] if has_lat else [(kc_ref, vc_ref)]):
        n_keys = k_ref.shape[-2]
        for c0 in range(0, n_keys, KEY_CHUNK):
            kc = min(KEY_CHUNK, n_keys - c0)
            chunks.append((k_ref, v_ref, c0, row, kc))
            row += kc
    mx = [None] * n_heads
    den = [None] * n_heads
    acc = [None] * n_heads
    for phase in range(n_heads + 1):
        for k_ref, v_ref, c0, row, kc in chunks:
            if phase < n_heads:
                g = phase
                gk = g * kv_heads // n_heads
                q = q_ref[:, g * dk:(g + 1) * dk]
                k = k_ref[gk, c0:c0 + kc, :] if k_ref.ndim == 3 else k_ref[c0:c0 + kc, gk * dk:(gk + 1) * dk]
                s = lax.dot_general(k, q, nt, preferred_element_type=F32)
                s_ref[g % 2, row:row + kc, :] = s
                cmax = jnp.max(s, axis=0, keepdims=True)
                mx[g] = cmax if mx[g] is None else jnp.maximum(mx[g], cmax)
            if phase > 0:
                g = phase - 1
                gk = g * kv_heads // n_heads
                p = jnp.exp2(s_ref[g % 2, row:row + kc, :] - mx[g])
                psum = jnp.sum(p, axis=0, keepdims=True)
                pv = jnp.dot(v_ref[gk * dv:(gk + 1) * dv, c0:c0 + kc], p.astype(BF16), preferred_element_type=F32)
                den[g] = psum if den[g] is None else den[g] + psum
                acc[g] = pv if acc[g] is None else acc[g] + pv
    o_all = jnp.concatenate([acc[g] / den[g] for g in range(n_heads)], axis=0)
    o_ref[...] = jnp.transpose(o_all).astype(o_ref.dtype)


def _attention(q, k, vt, geom, *, groups, n_heads, kv_heads, dk, dv, k_head_major, latent_queries):
    wq, wk, wv = n_heads * dk, kv_heads * dk, kv_heads * dv
    wo = n_heads * dv
    batch = geom.batch
    nc = geom.n_ctx
    ctx_blk0 = geom.n_lat // nc
    assert geom.n_lat % nc == 0

    def kspec(rows, tok_blk):
        if k_head_major:
            return pl.BlockSpec((kv_heads, rows, dk), lambda b, h, *_: (h, tok_blk(b), 0))
        return pl.BlockSpec((rows, wk), lambda b, h, *_: (tok_blk(b), h))

    def vspec(rows, tok_blk):
        return pl.BlockSpec((wv, rows), lambda b, h, *_: (h, tok_blk(b)))

    kern = functools.partial(_attn_kernel, n_heads=n_heads, kv_heads=kv_heads, dk=dk, dv=dv, has_lat=latent_queries)
    if latent_queries:
        tq = TT
        qt = geom.seq // tq
        return pl.pallas_call(
            kern,
            grid=(batch, groups, qt),
            in_specs=[
                pl.BlockSpec((tq, wq), lambda b, h, t: (b * qt + t, h)),
                kspec(geom.seq, lambda b: b),
                vspec(geom.seq, lambda b: b),
                kspec(nc, lambda b: ctx_blk0 + b),
                vspec(nc, lambda b: ctx_blk0 + b),
            ],
            out_specs=pl.BlockSpec((tq, wo), lambda b, h, t: (b * qt + t, h)),
            out_shape=jax.ShapeDtypeStruct((geom.n_lat, groups * wo), BF16),
            scratch_shapes=[pltpu.VMEM((2, geom.seq + nc, tq), F32)],
            compiler_params=_cparams(("parallel", "parallel", "arbitrary")),
            name="attn_lat",
        )(q, k, vt, k, vt)
    return pl.pallas_call(
        kern,
        grid=(batch, groups),
        in_specs=[
            pl.BlockSpec((nc, wq), lambda b, h: (ctx_blk0 + b, h)),
            kspec(nc, lambda b: ctx_blk0 + b),
            vspec(nc, lambda b: ctx_blk0 + b),
        ],
        out_specs=pl.BlockSpec((nc, wo), lambda b, h: (b, h)),
        out_shape=jax.ShapeDtypeStruct((batch * nc, groups * wo), BF16),
        scratch_shapes=[pltpu.VMEM((2, nc, nc), F32)],
        compiler_params=_cparams(("parallel", "parallel")),
        name="attn_ctx",
    )(q, k, vt)


def _post_kernel(o_ref, wo_ref, s_ref, mod_ref, g_ref, wrh_ref, wrl_ref, br_ref, s_out, h2_out, idx_out, w_out, cnt_out, *, n_experts):
    m = mod_ref[...]
    s = s_ref[...] + m[2:3] * jnp.dot(o_ref[...], wo_ref[...], preferred_element_type=F32)
    s_out[...] = s
    h2 = _norm_mod(s, g_ref[...], m[3:4], m[4:5])
    h2_out[...] = h2.astype(BF16)
    h2_hi, h2_lo = _split_hi_lo(h2)
    logits = (jnp.dot(h2_hi, wrh_ref[...], preferred_element_type=F32) + jnp.dot(h2_lo, wrh_ref[...], preferred_element_type=F32)
              + jnp.dot(h2_hi, wrl_ref[...], preferred_element_type=F32) + br_ref[...])
    lane = lax.broadcasted_iota(I32, logits.shape, 1).astype(F32)
    work = jnp.where(lane < n_experts, logits, -jnp.inf)
    vals, hits = [], []
    for k in range(TOP_K):
        mx = jnp.max(work, axis=-1, keepdims=True)
        sel = jnp.min(jnp.where(work == mx, lane, float(LANES)), axis=-1, keepdims=True)
        hit = lane == sel
        vals.append(mx)
        hits.append(hit)
        work = jnp.where(hit, -jnp.inf, work)
    es = [jnp.exp(v - vals[0]) for v in vals]
    den = es[0] + es[1] + es[2] + es[3]
    w_slab = jnp.zeros(logits.shape, F32)
    for k in range(TOP_K):
        w_slab = jnp.where(lane == k, es[k] / den, w_slab)
    w_out[...] = w_slab
    onehot = jnp.where(hits[0] | hits[1] | hits[2] | hits[3], 1.0, 0.0).astype(BF16)
    tr = lax.broadcasted_iota(I32, (TT, TT), 0)
    tc = lax.broadcasted_iota(I32, (TT, TT), 1)
    earlier = jnp.where(tc < tr, 1.0, 0.0).astype(BF16)
    er = lax.broadcasted_iota(I32, (LANES, LANES), 0)
    ec = lax.broadcasted_iota(I32, (LANES, LANES), 1)
    lower = jnp.where(er < ec, 1.0, 0.0).astype(BF16)
    bases = []
    for t in range(onehot.shape[0] // TT):
        oh = onehot[t * TT:(t + 1) * TT]
        rank = jnp.dot(earlier, oh, preferred_element_type=F32)
        cnt = jnp.sum(oh.astype(F32), axis=0, keepdims=True)
        cp = jnp.floor((cnt + (SUBLANES - 1)) * (1.0 / SUBLANES)) * SUBLANES
        seg = jnp.dot(jnp.broadcast_to(cp, (SUBLANES, LANES)).astype(BF16), lower, preferred_element_type=F32)[0:1]
        bases.append(seg + rank)
        cnt_out[t] = jnp.broadcast_to(cnt, (SUBLANES, LANES)).astype(I32)
    base = jnp.concatenate(bases, axis=0)
    lp_slab = jnp.zeros(logits.shape, F32)
    for k in range(TOP_K):
        lp_slab = jnp.where(lane == k, jnp.sum(jnp.where(hits[k], base, 0.0), axis=-1, keepdims=True), lp_slab)
    idx_out[...] = lp_slab.astype(I32)


def _post(o, w_o, stream, mods4, g_ffn, w_router, b_router, geom, layer, n_tiles):
    d = stream.shape[1]
    wo_w = o.shape[1]
    n_experts = w_router.shape[1]
    wr_hi, wr_lo = _split_hi_lo(jnp.concatenate([w_router, jnp.zeros((d, LANES - n_experts), F32)], axis=1))
    br = jnp.concatenate([b_router, jnp.zeros((LANES - n_experts,), F32)]).reshape(1, LANES)
    n = n_tiles * TT
    return pl.pallas_call(
        functools.partial(_post_kernel, n_experts=n_experts),
        grid=(n // PROJ_TILE,),
        in_specs=[
            pl.BlockSpec((PROJ_TILE, wo_w), lambda j: (j, 0)),
            pl.BlockSpec((wo_w, d), lambda j: (0, 0)),
            pl.BlockSpec((PROJ_TILE, d), lambda j: (j, 0)),
            _mod_spec(geom, layer, d, PROJ_TILE),
            pl.BlockSpec((1, d), lambda j: (0, 0)),
            pl.BlockSpec((d, LANES), lambda j: (0, 0)),
            pl.BlockSpec((d, LANES), lambda j: (0, 0)),
            pl.BlockSpec((1, LANES), lambda j: (0, 0)),
        ],
        out_specs=[
            pl.BlockSpec((PROJ_TILE, d), lambda j: (j, 0)),
            pl.BlockSpec((PROJ_TILE, d), lambda j: (j, 0)),
            pl.BlockSpec((PROJ_TILE, LANES), lambda j: (j, 0)),
            pl.BlockSpec((PROJ_TILE, LANES), lambda j: (j, 0)),
            pl.BlockSpec((PROJ_TILE // TT, SUBLANES, LANES), lambda j: (j, 0, 0)),
        ],
        out_shape=[
            jax.ShapeDtypeStruct((n, d), F32),
            jax.ShapeDtypeStruct((n, d), BF16),
            jax.ShapeDtypeStruct((n, LANES), I32),
            jax.ShapeDtypeStruct((n, LANES), F32),
            jax.ShapeDtypeStruct((n_tiles, SUBLANES, LANES), I32),
        ],
        compiler_params=_cparams(("parallel",)),
        name="post",
    )(o, w_o, stream, mods4, g_ffn, wr_hi, wr_lo, br)


def _local_rows(n_experts):
    return _round_up(TT * TOP_K + n_experts * (SUBLANES - 1), LANES)


def _n_blocks(n_tok, n_tiles, n_experts):
    rows = n_tok * TOP_K + n_experts * n_tiles * (SUBLANES - 1) + n_experts * (EXPERT_BLOCK - 1)
    return pl.cdiv(rows, EXPERT_BLOCK)


def _route_meta(cnt, lpos, n_experts):
    n_tiles = cnt.shape[0]
    n_blocks = _n_blocks(n_tiles * TT, n_tiles, n_experts)
    cp = _round_up(cnt, SUBLANES)
    seg = jnp.cumsum(cp, axis=1) - cp
    run_rows = cp.sum(axis=0)
    reg = _round_up(run_rows, EXPERT_BLOCK)
    reg_end = jnp.cumsum(reg)
    reg_start = reg_end - reg
    off = reg_start[None, :] + jnp.cumsum(cp, axis=0) - cp
    n_used = (reg_end[-1] // EXPERT_BLOCK).astype(I32)
    bstart = jnp.arange(n_blocks, dtype=I32) * EXPERT_BLOCK
    last_start = jnp.maximum(n_used - 1, 0) * EXPERT_BLOCK
    be = (jnp.minimum(bstart, last_start)[:, None] >= reg_end[None, :]).sum(axis=1).astype(I32)
    be = jnp.minimum(be, n_experts - 1)
    per_big = cp // BIG_CHUNK
    per_small = (cp % BIG_CHUNK) // SUBLANES
    tail = per_big * BIG_CHUNK
    return dict(
        big_list=_copy_list(per_big, seg, off, BIG_CHUNK, _max_big(n_experts)),
        small_list=_copy_list(per_small, seg + tail, off + tail, SUBLANES, _max_small(n_experts)),
        n_big=per_big.sum(axis=1).astype(I32), n_small=per_small.sum(axis=1).astype(I32),
        gap_start=(reg_start + run_rows).astype(I32), gap=(reg - run_rows).astype(I32),
        lpos=lpos.astype(I32), block_expert=be, n_used=n_used.reshape(1), n_blocks=n_blocks,
    )


def _max_big(n_experts):
    return _local_rows(n_experts) // BIG_CHUNK


def _max_small(n_experts):
    return n_experts * (BIG_CHUNK // SUBLANES - 1)


def _copy_list(per_run, local_row, sorted_row, chunk, max_copies):
    ends = jnp.cumsum(per_run, axis=1)[:, None, :]
    first = ends - per_run[:, None, :]
    k = jnp.arange(max_copies, dtype=I32)[None, :, None]
    mine = (first <= k) & (k < ends)
    pick = lambda v: jnp.sum(jnp.where(mine, v[:, None, :], 0), axis=2)
    j = k[:, :, 0] - pick(first[:, 0, :])
    src = pick(local_row) + chunk * j
    dst = pick(sorted_row) + chunk * j
    return (dst * LIST_RADIX + src).reshape(-1).astype(I32)


def _run_copies(tile, nbig_ref, nsmall_ref, big_ref, small_ref, max_big, max_small, make_big, make_small):
    def start(make, packed):
        local_row = pl.multiple_of(packed & (LIST_RADIX - 1), SUBLANES)
        sorted_row = pl.multiple_of(lax.shift_right_logical(packed, LIST_SHIFT), SUBLANES)
        make(local_row, sorted_row).start()

    def big(k, c):
        start(make_big, big_ref[tile * max_big + k])
        return c

    def small(k, c):
        start(make_small, small_ref[tile * max_small + k])
        return c

    lax.fori_loop(0, nbig_ref[tile], big, 0)
    lax.fori_loop(0, nsmall_ref[tile], small, 0)


def _wait_copies(tile, nbig_ref, nsmall_ref, make_big, make_small):
    def wb(j, c):
        make_big(0, 0).wait()
        return c

    def ws(j, c):
        make_small(0, 0).wait()
        return c

    lax.fori_loop(0, nbig_ref[tile], wb, 0)
    lax.fori_loop(0, nsmall_ref[tile], ws, 0)


def _dispatch_kernel(big_ref, small_ref, nbig_ref, nsmall_ref, gs_ref, gap_ref, nu_ref,
                     lpt_ref, h2_ref, xs_ref, buf_ref, zero_ref, sem, zsem, *, n_experts, local_rows, n_blocks):
    i = pl.program_id(0)
    n = pl.num_programs(0)
    slot = i % 2

    def big(src, dst):
        return pltpu.make_async_copy(buf_ref.at[slot, pl.ds(src, BIG_CHUNK), :], xs_ref.at[pl.ds(dst, BIG_CHUNK), :], sem.at[slot])

    def small(src, dst):
        return pltpu.make_async_copy(buf_ref.at[slot, pl.ds(src, SUBLANES), :], xs_ref.at[pl.ds(dst, SUBLANES), :], sem.at[slot])

    @pl.when(i >= 2)
    def _():
        _wait_copies(i - 2, nbig_ref, nsmall_ref, big, small)

    lpt = lpt_ref[...]
    rows = lax.broadcasted_iota(I32, (local_rows, TT), 0)
    hit = rows == lpt[0:1, :]
    for k in range(1, TOP_K):
        hit = hit | (rows == lpt[k:k + 1, :])
    p = jnp.where(hit, 1.0, 0.0).astype(BF16)
    buf_ref[slot] = jnp.dot(p, h2_ref[...], preferred_element_type=F32)
    _run_copies(i, nbig_ref, nsmall_ref, big_ref, small_ref, _max_big(n_experts), _max_small(n_experts), big, small)

    @pl.when(i == n - 1)
    def _():
        zero_ref[...] = jnp.zeros_like(zero_ref)
        sizes = []
        size = EXPERT_BLOCK // 2
        while size >= SUBLANES:
            sizes.append(size)
            size //= 2

        def zcopy(dst, size):
            return pltpu.make_async_copy(zero_ref.at[pl.ds(0, size), :], xs_ref.at[pl.ds(dst, size), :], zsem)

        def per_expert(e, carry):
            gap = gap_ref[e]
            pos = gs_ref[e]
            for size in sizes:
                take = (gap & size) != 0

                @pl.when(take)
                def _():
                    zcopy(pl.multiple_of(pos, SUBLANES), size).start()

                pos = pos + jnp.where(take, size, 0)
            return carry

        lax.fori_loop(0, n_experts, per_expert, 0)

        zrows = zero_ref.shape[0]
        per_block = EXPERT_BLOCK // zrows
        tail_copies = (n_blocks - nu_ref[0]) * per_block

        def tail(t, carry):
            zcopy(pl.multiple_of(nu_ref[0] * EXPERT_BLOCK + t * zrows, SUBLANES), zrows).start()
            return carry

        lax.fori_loop(0, tail_copies, tail, 0)

        def per_expert_wait(e, carry):
            gap = gap_ref[e]
            for size in sizes:
                @pl.when((gap & size) != 0)
                def _():
                    zcopy(0, size).wait()
            return carry

        lax.fori_loop(0, n_experts, per_expert_wait, 0)

        def tail_wait(t, carry):
            zcopy(0, zrows).wait()
            return carry

        lax.fori_loop(0, tail_copies, tail_wait, 0)

        @pl.when(i >= 1)
        def _():
            def big_o(src, dst):
                return pltpu.make_async_copy(buf_ref.at[1 - slot, pl.ds(src, BIG_CHUNK), :], xs_ref.at[pl.ds(dst, BIG_CHUNK), :], sem.at[1 - slot])

            def small_o(src, dst):
                return pltpu.make_async_copy(buf_ref.at[1 - slot, pl.ds(src, SUBLANES), :], xs_ref.at[pl.ds(dst, SUBLANES), :], sem.at[1 - slot])

            _wait_copies(i - 1, nbig_ref, nsmall_ref, big_o, small_o)

        _wait_copies(i, nbig_ref, nsmall_ref, big, small)


def _dispatch(h2, meta, n_tiles, n_experts):
    n, d = h2.shape
    local_rows = _local_rows(n_experts)
    n_rows = meta["n_blocks"] * EXPERT_BLOCK
    lpt = meta["lpos"].reshape(n_tiles, TT, TOP_K).transpose(0, 2, 1)
    grid_spec = pltpu.PrefetchScalarGridSpec(
        num_scalar_prefetch=7,
        grid=(n_tiles,),
        in_specs=[
            pl.BlockSpec((None, TOP_K, TT), lambda j, *_: (j, 0, 0)),
            pl.BlockSpec((TT, d), lambda j, *_: (j, 0)),
        ],
        out_specs=pl.BlockSpec(memory_space=pl.ANY),
        scratch_shapes=[
            pltpu.VMEM((2, local_rows, d), F32),
            pltpu.VMEM((EXPERT_BLOCK // 2, d), F32),
            pltpu.SemaphoreType.DMA((2,)),
            pltpu.SemaphoreType.DMA(()),
        ],
    )
    return pl.pallas_call(
        functools.partial(_dispatch_kernel, n_experts=n_experts, local_rows=local_rows, n_blocks=meta["n_blocks"]),
        grid_spec=grid_spec,
        out_shape=jax.ShapeDtypeStruct((n_rows, d), F32),
        compiler_params=_cparams(("arbitrary",)),
        name="moe_dispatch",
    )(meta["big_list"], meta["small_list"], meta["n_big"], meta["n_small"], meta["gap_start"], meta["gap"], meta["n_used"],
      lpt, h2)


def _ffn_kernel(be_ref, nu_ref, x_ref, wgu_ref, wd_ref, sel_ref, bg_ref, bu_ref, bd_ref, y_ref, wg_s, wu_s, wd_s):
    b = pl.program_id(0)
    used = b < nu_ref[0]

    @pl.when(used & ((b == 0) | (be_ref[b] != be_ref[jnp.maximum(b - 1, 0)])))
    def _():
        sel = sel_ref[...]
        pair = 2 * LANES
        for c in range(wgu_ref.shape[1] // pair):
            chunk = wgu_ref[:, c * pair:(c + 1) * pair].astype(BF16)
            de = jnp.dot(chunk, sel, preferred_element_type=F32)
            wg_s[:, c * LANES:(c + 1) * LANES] = de[:, :LANES].astype(BF16)
            wu_s[:, c * LANES:(c + 1) * LANES] = de[:, LANES:].astype(BF16)
        wd_s[...] = wd_ref[...].astype(BF16)

    @pl.when(used)
    def _():
        x = x_ref[...].astype(BF16)
        gate = jnp.dot(x, wg_s[...], preferred_element_type=F32) + bg_ref[...]
        up = jnp.dot(x, wu_s[...], preferred_element_type=F32) + bu_ref[...]
        gate = jnp.minimum(gate, SWIGLU_LIMIT)
        up = jnp.clip(up, -SWIGLU_LIMIT, SWIGLU_LIMIT)
        glu = gate / (1.0 + jnp.exp(-SWIGLU_ALPHA * gate))
        act = ((up + 1.0) * glu).astype(BF16)
        y_ref[...] = jnp.dot(act, wd_s[...], preferred_element_type=F32) + bd_ref[...]

    @pl.when(jnp.logical_not(used))
    def _():
        y_ref[...] = jnp.zeros_like(y_ref)


def _ffn(xs, meta, layer, w_gu, w_down, b_gate, b_up, b_down):
    n_rows, d = xs.shape
    n_blocks = meta["n_blocks"]
    f = w_down.shape[2]
    r = jnp.arange(2 * LANES)
    src = jnp.where(r < LANES, 2 * r, 2 * (r - LANES) + 1)
    sel = (r[:, None] == src[None, :]).astype(BF16)
    xmap = lambda b, be, nu: (jnp.minimum(b, jnp.maximum(nu[0] - 1, 0)), 0)
    wmap = lambda b, be, nu: (layer, be[b], 0, 0)
    grid_spec = pltpu.PrefetchScalarGridSpec(
        num_scalar_prefetch=2,
        grid=(n_blocks,),
        in_specs=[
            pl.BlockSpec((EXPERT_BLOCK, d), xmap),
            pl.BlockSpec((None, None, d, 2 * f), wmap),
            pl.BlockSpec((None, None, f, d), wmap),
            pl.BlockSpec((2 * LANES, 2 * LANES), lambda b, be, nu: (0, 0)),
            pl.BlockSpec((None, None, 1, f), wmap),
            pl.BlockSpec((None, None, 1, f), wmap),
            pl.BlockSpec((None, None, 1, d), wmap),
        ],
        out_specs=pl.BlockSpec((EXPERT_BLOCK, d), lambda b, be, nu: (b, 0)),
        scratch_shapes=[pltpu.VMEM((d, f), BF16), pltpu.VMEM((d, f), BF16), pltpu.VMEM((f, d), BF16)],
    )
    return pl.pallas_call(
        _ffn_kernel,
        grid_spec=grid_spec,
        out_shape=jax.ShapeDtypeStruct((n_rows, d), F32),
        compiler_params=_cparams(("arbitrary",)),
        name="moe_ffn",
    )(meta["block_expert"], meta["n_used"], xs, w_gu, w_down, sel, b_gate, b_up, b_down)


def _combine_kernel(big_ref, small_ref, nbig_ref, nsmall_ref,
                    lp_ref, w_ref, s_ref, mod_ref, gf_ref, ys_ref, o_ref, buf_ref, sem, *, n_experts, local_rows, final):
    i = pl.program_id(0)
    n = pl.num_programs(0)
    slot = i % 2

    def copies(sl):
        def big(dst, src):
            return pltpu.make_async_copy(ys_ref.at[pl.ds(src, BIG_CHUNK), :], buf_ref.at[sl, pl.ds(dst, BIG_CHUNK), :], sem.at[sl])

        def small(dst, src):
            return pltpu.make_async_copy(ys_ref.at[pl.ds(src, SUBLANES), :], buf_ref.at[sl, pl.ds(dst, SUBLANES), :], sem.at[sl])

        return big, small

    def start_copies(tile, sl):
        _run_copies(tile, nbig_ref, nsmall_ref, big_ref, small_ref, _max_big(n_experts), _max_small(n_experts), *copies(sl))

    @pl.when(i == 0)
    def _():
        buf_ref[...] = jnp.zeros_like(buf_ref)
        start_copies(0, 0)

    @pl.when(i + 1 < n)
    def _():
        start_copies(i + 1, 1 - slot)

    _wait_copies(i, nbig_ref, nsmall_ref, *copies(slot))

    lp = lp_ref[...]
    w = w_ref[...]
    lanes = lax.broadcasted_iota(I32, (TT, local_rows), 1)
    pw = jnp.zeros((TT, local_rows), F32)
    for k in range(TOP_K):
        pw = pw + jnp.where(lanes == lp[:, k:k + 1], w[:, k:k + 1], 0.0)
    y = buf_ref[slot].astype(BF16)
    f = jnp.dot(pw.astype(BF16), y, preferred_element_type=F32)
    m = mod_ref[...]
    s = s_ref[...] + m[5:6] * f
    if final:
        s = _rms(s) * gf_ref[...]
    o_ref[...] = s


def _combine(ys, meta, top_w, stream, mods4, g_final, geom, layer, n_tiles, n_experts, final):
    d = stream.shape[1]
    local_rows = _local_rows(n_experts)
    grid_spec = pltpu.PrefetchScalarGridSpec(
        num_scalar_prefetch=4,
        grid=(n_tiles,),
        in_specs=[
            pl.BlockSpec((TT, TOP_K), lambda j, *_: (j, 0)),
            pl.BlockSpec((TT, TOP_K), lambda j, *_: (j, 0)),
            pl.BlockSpec((TT, d), lambda j, *_: (j, 0)),
            pl.BlockSpec((None, None, N_MOD, d), lambda j, *_: (layer, geom.group(j), 0, 0)),
            pl.BlockSpec((1, d), lambda j, *_: (0, 0)),
            pl.BlockSpec(memory_space=pl.ANY),
        ],
        out_specs=pl.BlockSpec((TT, d), lambda j, *_: (j, 0)),
        scratch_shapes=[
            pltpu.VMEM((2, local_rows, d), F32),
            pltpu.SemaphoreType.DMA((2,)),
        ],
    )
    return pl.pallas_call(
        functools.partial(_combine_kernel, n_experts=n_experts, local_rows=local_rows, final=final),
        grid_spec=grid_spec,
        out_shape=jax.ShapeDtypeStruct((n_tiles * TT, d), F32),
        compiler_params=_cparams(("arbitrary",)),
        name="moe_combine",
    )(meta["big_list"], meta["small_list"], meta["n_big"], meta["n_small"],
      meta["lpos"], top_w, stream, mods4, g_final, ys)


def kernel(x, c, ctx, c_ctx, w_mod, b_mod, g_mix, g_ffn, g_final, f_w_in, f_w_out, mla_w_in, mla_g_qa, mla_w_qb,
           mla_g_kva, mla_w_kvb, mla_w_o, gqa_w_qkv, gqa_g_q, gqa_g_k, gqa_w_o, moe_w_router, moe_b_router,
           moe_w_gu, moe_b_gu, moe_w_down, moe_b_down):
    batch, seq, d = x.shape
    n_ctx = ctx.shape[1]
    depth = w_mod.shape[0]
    n_experts = moe_w_router.shape[2]
    geom = _Geom(batch, seq, n_ctx)

    n_groups = _round_up(batch + 1, SUBLANES)
    cc = jnp.concatenate([c, c_ctx[None, :], jnp.zeros((n_groups - batch - 1, d), F32)], axis=0)
    mods4 = _mods(cc, w_mod, b_mod).reshape(depth, n_groups, N_MOD, d)

    stream = jnp.concatenate([x.reshape(batch * seq, d), ctx.reshape(batch * n_ctx, d)], axis=0)

    gd = d // FOURIER_GROUPS
    gcos, gsin = _dft_tables(gd)
    eye = jnp.eye(FOURIER_GROUPS, dtype=F32)
    bd_cs = jnp.concatenate([jnp.kron(eye, gcos), jnp.kron(eye, gsin)], axis=1)

    f = moe_w_down.shape[2]
    b_gate = moe_b_gu[:, :, 0::2].reshape(depth, n_experts, 1, f)
    b_up = moe_b_gu[:, :, 1::2].reshape(depth, n_experts, 1, f)
    b_down = moe_b_down.reshape(depth, n_experts, 1, d)

    for i in range(depth):
        kind, j = i % N_MIXERS, i // N_MIXERS
        last = i == depth - 1
        ctx_used = (kind != 0) or (not last)
        n_tiles = geom.n_tiles if ctx_used else geom.n_lat_tiles
        gm = g_mix[i].reshape(1, d)
        gf = g_ffn[i].reshape(1, d)

        if kind == 0:
            wcs = _fold(f_w_in[j], bd_cs, BF16)
            u = _pre_fourier(stream, mods4, gm, wcs, geom, i, n_tiles * TT)
            o = _dft(u, 0, seq, batch, d)
            if ctx_used:
                o = jnp.concatenate([o, _dft(u, geom.n_lat, n_ctx, batch, d)], axis=0)
            w_o = f_w_out[j].astype(BF16)
        elif kind == 1:
            q, k, v = _pre_mla(stream, mods4, gm, mla_w_in[j], mla_g_qa[j], mla_w_qb[j], mla_g_kva[j], mla_w_kvb[j], geom, i)
            att = functools.partial(_attention, q, k, v, geom, groups=2, n_heads=MLA_HEADS // 2, kv_heads=MLA_HEADS // 2,
                                    dk=MLA_SLOT, dv=MLA_V, k_head_major=False)
            o = att(latent_queries=True)
            if not last:
                o = jnp.concatenate([o, att(latent_queries=False)], axis=0)
            w_o = mla_w_o[j].astype(BF16)
        else:
            q, k, v = _pre_gqa(stream, mods4, gm, gqa_w_qkv[j], gqa_g_q[j], gqa_g_k[j], geom, i)
            att = functools.partial(_attention, q, k, v, geom, groups=2, n_heads=GQA_HEADS // 2, kv_heads=GQA_KV_HEADS // 2,
                                    dk=GQA_HEAD_DIM, dv=GQA_HEAD_DIM, k_head_major=True)
            o = att(latent_queries=True)
            if not last:
                o = jnp.concatenate([o, att(latent_queries=False)], axis=0)
            w_o = gqa_w_o[j].astype(BF16)

        n_moe_tiles = geom.n_lat_tiles if last else geom.n_tiles
        stream, h2, lp_slab, w_slab, cnt_slab = _post(o, w_o, stream, mods4, gf, moe_w_router[i], moe_b_router[i], geom, i,
                                                      n_moe_tiles)
        meta = _route_meta(cnt_slab[:, 0, :n_experts], lp_slab[:, :TOP_K], n_experts)
        xs = _dispatch(h2, meta, n_moe_tiles, n_experts)
        ys = _ffn(xs, meta, i, moe_w_gu, moe_w_down, b_gate, b_up, b_down)
        stream = _combine(ys, meta, w_slab[:, :TOP_K], stream, mods4, g_final.reshape(1, d), geom, i, n_moe_tiles, n_experts,
                          last)

    return stream[:batch * seq].reshape(batch, seq, d)
```

```python
import functools

import jax
import jax.numpy as jnp
from jax import lax
from jax.experimental import pallas as pl
from jax.experimental.pallas import tpu as pltpu

F32 = jnp.float32
BF16 = jnp.bfloat16
I32 = jnp.int32
HI = lax.Precision.HIGHEST

GRID_W = 64
N_MIXERS = 3
NORM_EPS = 1e-6
ROPE_THETA = 10000.0
FOURIER_GROUPS = 8
MLA_HEADS = 16
MLA_Q_LORA = 384
MLA_KV_LORA = 256
MLA_NOPE = 64
MLA_ROPE = 32
MLA_V = 64
GQA_HEADS = 16
GQA_KV_HEADS = 4
GQA_HEAD_DIM = 64
TOP_K = 4
SWIGLU_LIMIT = 7.0
SWIGLU_ALPHA = 1.702
N_MOD = 6

LANES = 128
SUBLANES = 8
TT = 256
PROJ_TILE = 512
KEY_CHUNK = 1024
LOG2_E = 1.4426950408889634
EXPERT_BLOCK = 512
MOE_TILES_PER_STEP = 2
BIG_CHUNK = 32
LIST_SHIFT = 11
LIST_RADIX = 1 << LIST_SHIFT
MLA_SLOT = 128
VMEM_LIMIT = 56 * 1024 * 1024


def _cparams(sem):
    return pltpu.CompilerParams(dimension_semantics=sem, vmem_limit_bytes=VMEM_LIMIT)


def _round_up(x, m):
    return (x + m - 1) // m * m


def _norm_mod(x, g, shift, scale):
    y = x * lax.rsqrt(jnp.mean(x * x, axis=-1, keepdims=True) + NORM_EPS) * g
    return y * (1.0 + scale) + shift


def _rms(x):
    return x * lax.rsqrt(jnp.mean(x * x, axis=-1, keepdims=True) + NORM_EPS)


def _mod_kernel(cc_ref, w_ref, b_ref, o_ref):
    cc = cc_ref[...]
    a = cc / (1.0 + jnp.exp(-cc))
    o_ref[...] = jnp.dot(a, w_ref[...], precision=HI, preferred_element_type=F32) + b_ref[...]


def _mods(cc, w_mod, b_mod):
    depth, d, n6 = w_mod.shape
    g = cc.shape[0]
    tn = 1536 if n6 % 1536 == 0 else n6
    return pl.pallas_call(
        _mod_kernel,
        grid=(depth, n6 // tn),
        in_specs=[
            pl.BlockSpec((g, d), lambda i, n: (0, 0)),
            pl.BlockSpec((None, d, tn), lambda i, n: (i, 0, n)),
            pl.BlockSpec((None, 1, tn), lambda i, n: (i, 0, n)),
        ],
        out_specs=pl.BlockSpec((None, g, tn), lambda i, n: (i, 0, n)),
        out_shape=jax.ShapeDtypeStruct((depth, g, n6), F32),
        compiler_params=_cparams(("parallel", "parallel")),
        name="mods",
    )(cc, w_mod, b_mod.reshape(depth, 1, n6))


def _fold_kernel(a_ref, b_ref, o_ref):
    o_ref[...] = jnp.dot(a_ref[...], b_ref[...], precision=HI, preferred_element_type=F32).astype(o_ref.dtype)


def _fold(a, b, out_dtype):
    m, k = a.shape
    n = b.shape[1]
    tn = 512
    return pl.pallas_call(
        _fold_kernel,
        grid=(n // tn,),
        in_specs=[pl.BlockSpec((m, k), lambda j: (0, 0)), pl.BlockSpec((k, tn), lambda j: (0, j))],
        out_specs=pl.BlockSpec((m, tn), lambda j: (0, j)),
        out_shape=jax.ShapeDtypeStruct((m, n), out_dtype),
        compiler_params=_cparams(("parallel",)),
        name="fold",
    )(a, b)


class _Geom:
    def __init__(self, batch, seq, n_ctx):
        self.batch, self.seq, self.n_ctx = batch, seq, n_ctx
        assert seq % TT == 0 and n_ctx % TT == 0 and seq % GRID_W == 0
        self.lat_per_b = seq // TT
        self.ctx_per_b = n_ctx // TT
        self.n_lat_tiles = batch * self.lat_per_b
        self.n_ctx_tiles = batch * self.ctx_per_b
        self.n_tiles = self.n_lat_tiles + self.n_ctx_tiles
        self.n_lat = batch * seq
        self.n_tok = self.n_lat + batch * n_ctx

    def group(self, j, tile=TT):
        return jnp.where(j < self.n_lat // tile, j // (self.seq // tile), self.batch)

    def pos_tile(self, j, tile=TT):
        return jnp.where(j < self.n_lat // tile, j % (self.seq // tile), self.seq // tile)


def _mod_spec(geom, layer, d, tile=TT):
    return pl.BlockSpec((None, None, N_MOD, d), lambda j: (layer, geom.group(j, tile), 0, 0))


FOURIER_TILE = 1024


def _pre_fourier_kernel(s_ref, mod_ref, g_ref, perm_ref, w_ref, u_ref):
    m = mod_ref[...]
    h = _norm_mod(s_ref[...], g_ref[...], m[0:1], m[1:2]).astype(BF16)
    h = jnp.dot(perm_ref[...], h, preferred_element_type=F32).astype(BF16)
    u_ref[...] = jnp.dot(h, w_ref[...], preferred_element_type=F32).astype(BF16)


def _pre_fourier(stream, mods4, g_mix, wcs, geom, layer, n_tok):
    d = stream.shape[1]
    assert n_tok % FOURIER_TILE == 0 and geom.seq % FOURIER_TILE == 0 and FOURIER_TILE % geom.n_ctx == 0
    lat_tiles = geom.n_lat // FOURIER_TILE
    per_b = geom.seq // FOURIER_TILE
    group = lambda j: jnp.where(j < lat_tiles, j // per_b, geom.batch)
    r = jnp.arange(FOURIER_TILE)
    half = FOURIER_TILE // 2
    src = jnp.where(r < half, 2 * r, 2 * (r - half) + 1)
    perm = (src[:, None] == r[None, :]).astype(BF16)
    return pl.pallas_call(
        _pre_fourier_kernel,
        grid=(n_tok // FOURIER_TILE,),
        in_specs=[
            pl.BlockSpec((FOURIER_TILE, d), lambda j: (j, 0)),
            pl.BlockSpec((None, None, N_MOD, d), lambda j: (layer, group(j), 0, 0)),
            pl.BlockSpec((1, d), lambda j: (0, 0)),
            pl.BlockSpec((FOURIER_TILE, FOURIER_TILE), lambda j: (0, 0)),
            pl.BlockSpec((d, 2 * d), lambda j: (0, 0)),
        ],
        out_specs=pl.BlockSpec((FOURIER_TILE, 2 * d), lambda j: (j, 0)),
        out_shape=jax.ShapeDtypeStruct((n_tok, 2 * d), BF16),
        compiler_params=_cparams(("parallel",)),
        name="pre_fourier",
    )(stream, mods4, g_mix, perm, wcs)


def _dft_kernel(ce_ref, se_ref, co_ref, so_ref, ue_ref, uo_ref, o_ref, acce_ref, acco_ref, *, d):
    k = pl.program_id(2)

    @pl.when(k == 0)
    def _():
        acce_ref[...] = jnp.zeros_like(acce_ref)
        acco_ref[...] = jnp.zeros_like(acco_ref)

    acce_ref[...] += (jnp.dot(ce_ref[...], ue_ref[:, :d], preferred_element_type=F32)
                      - jnp.dot(se_ref[...], ue_ref[:, d:], preferred_element_type=F32))
    acco_ref[...] += (jnp.dot(co_ref[...], uo_ref[:, :d], preferred_element_type=F32)
                      - jnp.dot(so_ref[...], uo_ref[:, d:], preferred_element_type=F32))

    @pl.when(k == pl.num_programs(2) - 1)
    def _():
        o_ref[0] = (acce_ref[...] + acco_ref[...]).astype(o_ref.dtype)
        o_ref[1] = (acce_ref[...] - acco_ref[...]).astype(o_ref.dtype)


def _dft_tables(n, rows=None, col_step=1, col_off=0):
    rows = n if rows is None else rows
    cols = n // col_step
    scale = n ** -0.5

    def tables(n_rows, step):
        k = lax.broadcasted_iota(I32, (n_rows, cols), 0) * step
        c = lax.broadcasted_iota(I32, (n_rows, cols), 1) * col_step + col_off
        ang = ((k * c) % n).astype(F32) * (2.0 * jnp.pi / n)
        return jnp.cos(ang), jnp.sin(ang)

    coarse = 64
    if rows <= 4 * coarse:
        c, s = tables(rows, 1)
        return c * scale, s * scale
    ca, sa = tables(rows // coarse, coarse)
    cb, sb = tables(coarse, 1)
    ca, sa = ca[:, None, :] * scale, sa[:, None, :] * scale
    cos = (ca * cb[None] - sa * sb[None]).reshape(rows, cols)
    sin = (sa * cb[None] + ca * sb[None]).reshape(rows, cols)
    return cos, sin


def _dft(u, tok0, seq, batch, d):
    half = seq // 2
    ce, se = _dft_tables(seq, half, 2, 0)
    co, so = _dft_tables(seq, half, 2, 1)
    tabs = [t.astype(BF16) for t in (ce, se, co, so)]
    kb = min(FOURIER_TILE // 2, half)
    ksteps = half // kb
    tm = min(half, 1024)
    per_tile = FOURIER_TILE // kb

    def even_block(b, k):
        tok = tok0 + b * seq + k * FOURIER_TILE
        return (tok // FOURIER_TILE) * per_tile + (tok % FOURIER_TILE) // (2 * kb)

    tspec = pl.BlockSpec((tm, kb), lambda b, m, k: (m, k))
    out = pl.pallas_call(
        functools.partial(_dft_kernel, d=d),
        grid=(batch, half // tm, ksteps),
        in_specs=[
            tspec, tspec, tspec, tspec,
            pl.BlockSpec((kb, 2 * d), lambda b, m, k: (even_block(b, k), 0)),
            pl.BlockSpec((kb, 2 * d), lambda b, m, k: (even_block(b, k) + per_tile // 2, 0)),
        ],
        out_specs=pl.BlockSpec((None, 2, tm, d), lambda b, m, k: (b, 0, m, 0)),
        out_shape=jax.ShapeDtypeStruct((batch, 2, half, d), BF16),
        scratch_shapes=[pltpu.VMEM((tm, d), F32), pltpu.VMEM((tm, d), F32)],
        compiler_params=_cparams(("parallel", "parallel", "arbitrary")),
        name="dft",
    )(*tabs, u, u)
    return out.reshape(batch * seq, d)


def _pre_mla_kernel(s_ref, mod_ref, g_ref, win_ref, gqa_ref, gkva_ref, wqb_ref, wqbs_ref, wkvk_ref, wkvv_ref,
                    cos_ref, sin_ref, q_ref, k_ref, v_ref, *, scale):
    m = mod_ref[...]
    h = _norm_mod(s_ref[...], g_ref[...], m[0:1], m[1:2])
    a = jnp.dot(h.astype(BF16), win_ref[...], preferred_element_type=F32)
    aq = (_rms(a[:, :MLA_Q_LORA]) * gqa_ref[...]).astype(BF16)
    ckv = (_rms(a[:, MLA_Q_LORA:MLA_Q_LORA + MLA_KV_LORA]) * gkva_ref[...]).astype(BF16)
    o = MLA_Q_LORA + MLA_KV_LORA
    cos = cos_ref[...]
    sin = sin_ref[...]
    kpe = a[:, o:o + MLA_SLOT] * cos + a[:, o + MLA_SLOT:o + 2 * MLA_SLOT] * sin
    cos_h = jnp.tile(cos, (1, MLA_HEADS))
    sin_h = jnp.tile(sin, (1, MLA_HEADS))
    q = jnp.dot(aq, wqb_ref[...], preferred_element_type=F32)
    qs = jnp.dot(aq, wqbs_ref[...], preferred_element_type=F32)
    q_ref[...] = ((q * cos_h + qs * sin_h) * scale).astype(BF16)
    kk = jnp.dot(ckv, wkvk_ref[...], preferred_element_type=F32)
    k_ref[...] = (kk + jnp.tile(kpe, (1, MLA_HEADS))).astype(BF16)
    v_ref[...] = lax.dot_general(wkvv_ref[...], ckv, (((1,), (1,)), ((), ())), preferred_element_type=F32).astype(BF16)


def _rope_perm(rot):
    sec = rot // 2
    half = sec // 2
    d = jnp.arange(rot)
    first = (d % sec) < half
    partner = jnp.where(first, d + half, d - half)
    sign = jnp.where(first, -1.0, 1.0).astype(F32)
    return partner, sign


def _rope_tables(seq, rot, tile):
    rows = seq // GRID_W
    row = jnp.repeat(jnp.arange(rows, dtype=F32), GRID_W)
    col = jnp.tile(jnp.arange(GRID_W, dtype=F32), rows)
    n_freq = rot // 4
    inv_freq = ROPE_THETA ** (-jnp.arange(n_freq, dtype=F32) / n_freq)
    ang = jnp.stack([row[:, None] * inv_freq, col[:, None] * inv_freq], axis=1)
    cos = jnp.cos(ang)
    sin = jnp.sin(ang)
    cos_full = jnp.concatenate([cos, cos], axis=-1).reshape(seq, rot)
    sin_full = jnp.concatenate([sin, sin], axis=-1).reshape(seq, rot)
    _, sign = _rope_perm(rot)
    sin_full = sin_full * sign
    ident_c = jnp.ones((tile, rot), F32)
    ident_s = jnp.zeros((tile, rot), F32)
    assert seq % tile == 0
    return jnp.concatenate([cos_full, ident_c], 0), jnp.concatenate([sin_full, ident_s], 0)


def _pre_mla(stream, mods4, g_mix, w_in, g_qa, w_qb, g_kva, w_kvb, geom, layer):
    d = stream.shape[1]
    hd = MLA_NOPE + MLA_ROPE
    partner, _ = _rope_perm(MLA_ROPE)
    o = MLA_Q_LORA + MLA_KV_LORA
    kpe_w = w_in[:, o:]
    z_lo = jnp.zeros((d, MLA_NOPE), F32)
    z_hi = jnp.zeros((d, MLA_SLOT - hd), F32)
    w_in_ext = jnp.concatenate([w_in[:, :o], z_lo, kpe_w, z_hi, z_lo, kpe_w[:, partner], z_hi], axis=1).astype(BF16)
    wq = w_qb.reshape(MLA_Q_LORA, MLA_HEADS, hd)
    zq = jnp.zeros((MLA_Q_LORA, MLA_HEADS, MLA_SLOT - hd), F32)
    wq_p = jnp.concatenate([wq, zq], axis=2).reshape(MLA_Q_LORA, MLA_HEADS * MLA_SLOT).astype(BF16)
    wq_s = jnp.concatenate([jnp.zeros_like(wq[:, :, :MLA_NOPE]), wq[:, :, MLA_NOPE:][:, :, partner], zq], axis=2)
    wq_s = wq_s.reshape(MLA_Q_LORA, MLA_HEADS * MLA_SLOT).astype(BF16)
    wkv = w_kvb.reshape(MLA_KV_LORA, MLA_HEADS, MLA_NOPE + MLA_V)
    zk = jnp.zeros((MLA_KV_LORA, MLA_HEADS, MLA_SLOT - MLA_NOPE), F32)
    wkv_k = jnp.concatenate([wkv[:, :, :MLA_NOPE], zk], axis=2).reshape(MLA_KV_LORA, MLA_HEADS * MLA_SLOT).astype(BF16)
    wkv_v = wkv[:, :, MLA_NOPE:].reshape(MLA_KV_LORA, MLA_HEADS * MLA_V).T.astype(BF16)
    cos, sin = _rope_tables(geom.seq, MLA_ROPE, PROJ_TILE)
    rows = cos.shape[0]
    cos_slot = jnp.concatenate([jnp.ones((rows, MLA_NOPE), F32), cos, jnp.ones((rows, MLA_SLOT - hd), F32)], axis=1)
    sin_slot = jnp.concatenate([jnp.zeros((rows, MLA_NOPE), F32), sin, jnp.zeros((rows, MLA_SLOT - hd), F32)], axis=1)
    n_tiles = geom.n_tok // PROJ_TILE
    wq_w = MLA_HEADS * MLA_SLOT
    wv_w = MLA_HEADS * MLA_V
    full = lambda a: pl.BlockSpec(a.shape, lambda j: (0,) * a.ndim)
    g_qa2, g_kva2 = g_qa.reshape(1, -1), g_kva.reshape(1, -1)
    return pl.pallas_call(
        functools.partial(_pre_mla_kernel, scale=float(hd) ** -0.5 * LOG2_E),
        grid=(n_tiles,),
        in_specs=[
            pl.BlockSpec((PROJ_TILE, d), lambda j: (j, 0)),
            _mod_spec(geom, layer, d, PROJ_TILE),
            pl.BlockSpec((1, d), lambda j: (0, 0)),
            full(w_in_ext), full(g_qa2), full(g_kva2), full(wq_p), full(wq_s), full(wkv_k), full(wkv_v),
            pl.BlockSpec((PROJ_TILE, MLA_SLOT), lambda j: (geom.pos_tile(j, PROJ_TILE), 0)),
            pl.BlockSpec((PROJ_TILE, MLA_SLOT), lambda j: (geom.pos_tile(j, PROJ_TILE), 0)),
        ],
        out_specs=[
            pl.BlockSpec((PROJ_TILE, wq_w), lambda j: (j, 0)),
            pl.BlockSpec((PROJ_TILE, wq_w), lambda j: (j, 0)),
            pl.BlockSpec((wv_w, PROJ_TILE), lambda j: (0, j)),
        ],
        out_shape=[
            jax.ShapeDtypeStruct((n_tiles * PROJ_TILE, wq_w), BF16),
            jax.ShapeDtypeStruct((n_tiles * PROJ_TILE, wq_w), BF16),
            jax.ShapeDtypeStruct((wv_w, n_tiles * PROJ_TILE), BF16),
        ],
        compiler_params=_cparams(("parallel",)),
        name="pre_mla",
    )(stream, mods4, g_mix, w_in_ext, g_qa2, g_kva2, wq_p, wq_s, wkv_k, wkv_v, cos_slot, sin_slot)


def _split_hi_lo(x):
    hi = x.astype(BF16)
    lo = (x - hi.astype(F32)).astype(BF16)
    return hi, lo


def _pre_gqa_kernel(s_ref, mod_ref, g_ref, w_ref, wvt_ref, bd_ref, cq_ref, sq_ref, ck_ref, sk_ref, q_ref, k_ref, v_ref, *,
                    scale):
    m = mod_ref[...]
    h = _norm_mod(s_ref[...], g_ref[...], m[0:1], m[1:2]).astype(BF16)
    a = jnp.dot(h, w_ref[...], preferred_element_type=F32)
    nq = GQA_HEADS * GQA_HEAD_DIM
    nkv = GQA_KV_HEADS * GQA_HEAD_DIM
    aq, ak = a[:, :nq], a[:, nq:nq + nkv]
    aqs, aks = a[:, nq + nkv:2 * nq + nkv], a[:, 2 * nq + nkv:]
    bd = bd_ref[...]

    def head_rs(x, width):
        hi, lo = _split_hi_lo(x * x)
        b = bd[:width, :width]
        ssq = jnp.dot(hi, b, preferred_element_type=F32) + jnp.dot(lo, b, preferred_element_type=F32)
        return lax.rsqrt(ssq * (1.0 / GQA_HEAD_DIM) + NORM_EPS)

    reps_q = nq // cq_ref.shape[1]
    reps_k = nkv // ck_ref.shape[1]
    q = head_rs(aq, nq) * (aq * jnp.tile(cq_ref[...], (1, reps_q)) + aqs * jnp.tile(sq_ref[...], (1, reps_q)))
    q_ref[...] = (q * scale).astype(BF16)
    k = head_rs(ak, nkv) * (ak * jnp.tile(ck_ref[...], (1, reps_k)) + aks * jnp.tile(sk_ref[...], (1, reps_k)))
    k = k.astype(BF16)
    for g in range(GQA_KV_HEADS):
        k_ref[g] = k[:, g * GQA_HEAD_DIM:(g + 1) * GQA_HEAD_DIM]
    v_ref[...] = lax.dot_general(wvt_ref[...], h, (((1,), (1,)), ((), ())), preferred_element_type=F32).astype(BF16)


def _pre_gqa(stream, mods4, g_mix, w_qkv, g_q, g_k, geom, layer):
    d = stream.shape[1]
    nq = GQA_HEADS * GQA_HEAD_DIM
    nkv = GQA_KV_HEADS * GQA_HEAD_DIM
    partner, _ = _rope_perm(GQA_HEAD_DIM)
    wq = w_qkv[:, :nq].reshape(d, GQA_HEADS, GQA_HEAD_DIM)
    wk = w_qkv[:, nq:nq + nkv].reshape(d, GQA_KV_HEADS, GQA_HEAD_DIM)
    w_ext = jnp.concatenate([w_qkv[:, :nq + nkv], wq[:, :, partner].reshape(d, nq), wk[:, :, partner].reshape(d, nkv)],
                            axis=1).astype(BF16)
    w_vt = w_qkv[:, nq + nkv:].T.astype(BF16)
    cos, sin = _rope_tables(geom.seq, GQA_HEAD_DIM, PROJ_TILE)
    per = LANES // GQA_HEAD_DIM
    cq = jnp.tile(cos * g_q[None, :], (1, per))
    sq = jnp.tile(sin * g_q[partner][None, :], (1, per))
    ck = jnp.tile(cos * g_k[None, :], (1, per))
    sk = jnp.tile(sin * g_k[partner][None, :], (1, per))
    hid = jnp.arange(nq) // GQA_HEAD_DIM
    bd = (hid[:, None] == hid[None, :]).astype(BF16)
    n_tiles = geom.n_tok // PROJ_TILE
    full = lambda a: pl.BlockSpec(a.shape, lambda j: (0,) * a.ndim)
    tab = pl.BlockSpec((PROJ_TILE, LANES), lambda j: (geom.pos_tile(j, PROJ_TILE), 0))
    return pl.pallas_call(
        functools.partial(_pre_gqa_kernel, scale=float(GQA_HEAD_DIM) ** -0.5 * LOG2_E),
        grid=(n_tiles,),
        in_specs=[
            pl.BlockSpec((PROJ_TILE, d), lambda j: (j, 0)),
            _mod_spec(geom, layer, d, PROJ_TILE),
            pl.BlockSpec((1, d), lambda j: (0, 0)),
            full(w_ext), full(w_vt), full(bd), tab, tab, tab, tab,
        ],
        out_specs=[
            pl.BlockSpec((PROJ_TILE, nq), lambda j: (j, 0)),
            pl.BlockSpec((GQA_KV_HEADS, PROJ_TILE, GQA_HEAD_DIM), lambda j: (0, j, 0)),
            pl.BlockSpec((nkv, PROJ_TILE), lambda j: (0, j)),
        ],
        out_shape=[
            jax.ShapeDtypeStruct((n_tiles * PROJ_TILE, nq), BF16),
            jax.ShapeDtypeStruct((GQA_KV_HEADS, n_tiles * PROJ_TILE, GQA_HEAD_DIM), BF16),
            jax.ShapeDtypeStruct((nkv, n_tiles * PROJ_TILE), BF16),
        ],
        compiler_params=_cparams(("parallel",)),
        name="pre_gqa",
    )(stream, mods4, g_mix, w_ext, w_vt, bd, cq, sq, ck, sk)


def _attn_kernel(q_ref, kl_ref, vl_ref, kc_ref, vc_ref, o_ref, s_ref, *, n_lat_steps, **head_args):
    t = pl.program_id(2)

    @pl.when(t < n_lat_steps)
    def _():
        _attn_tile(q_ref, kl_ref, vl_ref, kc_ref, vc_ref, o_ref, s_ref, has_lat=True, **head_args)

    @pl.when(t >= n_lat_steps)
    def _():
        _attn_tile(q_ref, kl_ref, vl_ref, kc_ref, vc_ref, o_ref, s_ref, has_lat=False, **head_args)


def _attn_tile(q_ref, kl_ref, vl_ref, kc_ref, vc_ref, o_ref, s_ref, *, n_heads, kv_heads, dk, dv, has_lat):
    nt = (((1,), (1,)), ((), ()))
    tq = q_ref.shape[0]
    chunks, row = [], 0
    for k_ref, v_ref in ([(kc_ref, vc_ref), (kl_ref, vl_ref)] if has_lat else [(kc_ref, vc_ref)]):
        n_keys = k_ref.shape[-2]
        for c0 in range(0, n_keys, KEY_CHUNK):
            kc = min(KEY_CHUNK, n_keys - c0)
            chunks.append((k_ref, v_ref, c0, row, kc))
            row += kc
    mx = [None] * n_heads
    den = [None] * n_heads
    acc = [None] * n_heads
    for phase in range(n_heads + 1):
        for k_ref, v_ref, c0, row, kc in chunks:
            if phase < n_heads:
                g = phase
                gk = g * kv_heads // n_heads
                q = q_ref[:, g * dk:(g + 1) * dk]
                k = k_ref[gk, c0:c0 + kc, :] if k_ref.ndim == 3 else k_ref[c0:c0 + kc, gk * dk:(gk + 1) * dk]
                s = lax.dot_general(k, q, nt, preferred_element_type=F32)
                s_ref[g % 2, row:row + kc, :] = s
                cmax = jnp.max(s, axis=0, keepdims=True)
                mx[g] = cmax if mx[g] is None else jnp.maximum(mx[g], cmax)
            if phase > 0:
                g = phase - 1
                gk = g * kv_heads // n_heads
                p = jnp.exp2(s_ref[g % 2, row:row + kc, :] - mx[g])
                psum = jnp.sum(p, axis=0, keepdims=True)
                pv = jnp.dot(v_ref[gk * dv:(gk + 1) * dv, c0:c0 + kc], p.astype(BF16), preferred_element_type=F32)
                den[g] = psum if den[g] is None else den[g] + psum
                acc[g] = pv if acc[g] is None else acc[g] + pv
    o_all = jnp.concatenate([acc[g] / den[g] for g in range(n_heads)], axis=0)
    o_ref[...] = jnp.transpose(o_all).astype(o_ref.dtype)


def _attention(q, k, vt, geom, *, groups, n_heads, kv_heads, dk, dv, k_head_major, context_queries):
    wq, wk, wv = n_heads * dk, kv_heads * dk, kv_heads * dv
    wo = n_heads * dv
    batch = geom.batch
    nc = geom.n_ctx
    ctx_blk0 = geom.n_lat // nc
    assert geom.n_lat % nc == 0

    def kspec(rows, tok_blk):
        if k_head_major:
            return pl.BlockSpec((kv_heads, rows, dk), lambda b, h, *_: (h, tok_blk(b), 0))
        return pl.BlockSpec((rows, wk), lambda b, h, *_: (tok_blk(b), h))

    def vspec(rows, tok_blk):
        return pl.BlockSpec((wv, rows), lambda b, h, *_: (h, tok_blk(b)))

    tq = TT
    qt = geom.seq // tq
    assert nc == tq or not context_queries
    steps = qt + 1 if context_queries else qt
    qrow = lambda b, t: jnp.where(t < qt, b * qt + t, ctx_blk0 + b)
    return pl.pallas_call(
        functools.partial(_attn_kernel, n_lat_steps=qt, n_heads=n_heads, kv_heads=kv_heads, dk=dk, dv=dv),
        grid=(batch, groups, steps),
        in_specs=[
            pl.BlockSpec((tq, wq), lambda b, h, t: (qrow(b, t), h)),
            kspec(geom.seq, lambda b: b),
            vspec(geom.seq, lambda b: b),
            kspec(nc, lambda b: ctx_blk0 + b),
            vspec(nc, lambda b: ctx_blk0 + b),
        ],
        out_specs=pl.BlockSpec((tq, wo), lambda b, h, t: (qrow(b, t), h)),
        out_shape=jax.ShapeDtypeStruct((geom.n_tok if context_queries else geom.n_lat, groups * wo), BF16),
        scratch_shapes=[pltpu.VMEM((2, geom.seq + nc, tq), F32)],
        compiler_params=_cparams(("parallel", "parallel", "arbitrary")),
        name="attention",
    )(q, k, vt, k, vt)


def _post_kernel(o_ref, wo_ref, s_ref, mod_ref, g_ref, wrh_ref, wrl_ref, br_ref, s_out, h2_out, idx_out, w_out, cnt_out, *, n_experts):
    m = mod_ref[...]
    s = s_ref[...] + m[2:3] * jnp.dot(o_ref[...], wo_ref[...], preferred_element_type=F32)
    s_out[...] = s
    h2 = _norm_mod(s, g_ref[...], m[3:4], m[4:5])
    h2_out[...] = h2.astype(BF16)
    h2_hi, h2_lo = _split_hi_lo(h2)
    logits = (jnp.dot(h2_hi, wrh_ref[...], preferred_element_type=F32) + jnp.dot(h2_lo, wrh_ref[...], preferred_element_type=F32)
              + jnp.dot(h2_hi, wrl_ref[...], preferred_element_type=F32) + br_ref[...])
    lane = lax.broadcasted_iota(I32, logits.shape, 1).astype(F32)
    work = jnp.where(lane < n_experts, logits, -jnp.inf)
    vals, hits = [], []
    for k in range(TOP_K):
        mx = jnp.max(work, axis=-1, keepdims=True)
        sel = jnp.min(jnp.where(work == mx, lane, float(LANES)), axis=-1, keepdims=True)
        hit = lane == sel
        vals.append(mx)
        hits.append(hit)
        work = jnp.where(hit, -jnp.inf, work)
    es = [jnp.exp(v - vals[0]) for v in vals]
    den = es[0] + es[1] + es[2] + es[3]
    w_slab = jnp.zeros(logits.shape, F32)
    for k in range(TOP_K):
        w_slab = jnp.where(lane == k, es[k] / den, w_slab)
    w_out[...] = w_slab
    onehot = jnp.where(hits[0] | hits[1] | hits[2] | hits[3], 1.0, 0.0).astype(BF16)
    tr = lax.broadcasted_iota(I32, (TT, TT), 0)
    tc = lax.broadcasted_iota(I32, (TT, TT), 1)
    earlier = jnp.where(tc < tr, 1.0, 0.0).astype(BF16)
    er = lax.broadcasted_iota(I32, (LANES, LANES), 0)
    ec = lax.broadcasted_iota(I32, (LANES, LANES), 1)
    lower = jnp.where(er < ec, 1.0, 0.0).astype(BF16)
    bases = []
    for t in range(onehot.shape[0] // TT):
        oh = onehot[t * TT:(t + 1) * TT]
        rank = jnp.dot(earlier, oh, preferred_element_type=F32)
        cnt = jnp.sum(oh.astype(F32), axis=0, keepdims=True)
        cp = jnp.floor((cnt + (SUBLANES - 1)) * (1.0 / SUBLANES)) * SUBLANES
        seg = jnp.dot(jnp.broadcast_to(cp, (SUBLANES, LANES)).astype(BF16), lower, preferred_element_type=F32)[0:1]
        bases.append(seg + rank)
        cnt_out[t] = jnp.broadcast_to(cnt, (SUBLANES, LANES)).astype(I32)
    base = jnp.concatenate(bases, axis=0)
    lp_slab = jnp.zeros(logits.shape, F32)
    for k in range(TOP_K):
        lp_slab = jnp.where(lane == k, jnp.sum(jnp.where(hits[k], base, 0.0), axis=-1, keepdims=True), lp_slab)
    idx_out[...] = lp_slab.astype(I32)


def _post(o, w_o, stream, mods4, g_ffn, w_router, b_router, geom, layer, n_tiles):
    d = stream.shape[1]
    wo_w = o.shape[1]
    n_experts = w_router.shape[1]
    wr_hi, wr_lo = _split_hi_lo(jnp.concatenate([w_router, jnp.zeros((d, LANES - n_experts), F32)], axis=1))
    br = jnp.concatenate([b_router, jnp.zeros((LANES - n_experts,), F32)]).reshape(1, LANES)
    n = n_tiles * TT
    return pl.pallas_call(
        functools.partial(_post_kernel, n_experts=n_experts),
        grid=(n // PROJ_TILE,),
        in_specs=[
            pl.BlockSpec((PROJ_TILE, wo_w), lambda j: (j, 0)),
            pl.BlockSpec((wo_w, d), lambda j: (0, 0)),
            pl.BlockSpec((PROJ_TILE, d), lambda j: (j, 0)),
            _mod_spec(geom, layer, d, PROJ_TILE),
            pl.BlockSpec((1, d), lambda j: (0, 0)),
            pl.BlockSpec((d, LANES), lambda j: (0, 0)),
            pl.BlockSpec((d, LANES), lambda j: (0, 0)),
            pl.BlockSpec((1, LANES), lambda j: (0, 0)),
        ],
        out_specs=[
            pl.BlockSpec((PROJ_TILE, d), lambda j: (j, 0)),
            pl.BlockSpec((PROJ_TILE, d), lambda j: (j, 0)),
            pl.BlockSpec((PROJ_TILE, LANES), lambda j: (j, 0)),
            pl.BlockSpec((PROJ_TILE, LANES), lambda j: (j, 0)),
            pl.BlockSpec((PROJ_TILE // TT, SUBLANES, LANES), lambda j: (j, 0, 0)),
        ],
        out_shape=[
            jax.ShapeDtypeStruct((n, d), F32),
            jax.ShapeDtypeStruct((n, d), BF16),
            jax.ShapeDtypeStruct((n, LANES), I32),
            jax.ShapeDtypeStruct((n, LANES), F32),
            jax.ShapeDtypeStruct((n_tiles, SUBLANES, LANES), I32),
        ],
        compiler_params=_cparams(("parallel",)),
        name="post",
    )(o, w_o, stream, mods4, g_ffn, wr_hi, wr_lo, br)


def _local_rows(n_experts):
    return _round_up(TT * TOP_K + n_experts * (SUBLANES - 1), LANES)


def _n_blocks(n_tok, n_tiles, n_experts):
    rows = n_tok * TOP_K + n_experts * n_tiles * (SUBLANES - 1) + n_experts * (EXPERT_BLOCK - 1)
    return pl.cdiv(rows, EXPERT_BLOCK)


def _route_meta(cnt, lpos, n_experts):
    n_tiles = cnt.shape[0]
    n_blocks = _n_blocks(n_tiles * TT, n_tiles, n_experts)
    cp = _round_up(cnt, SUBLANES)
    seg = jnp.cumsum(cp, axis=1) - cp
    run_rows = cp.sum(axis=0)
    reg = _round_up(run_rows, EXPERT_BLOCK)
    reg_end = jnp.cumsum(reg)
    reg_start = reg_end - reg
    off = reg_start[None, :] + jnp.cumsum(cp, axis=0) - cp
    n_used = (reg_end[-1] // EXPERT_BLOCK).astype(I32)
    bstart = jnp.arange(n_blocks, dtype=I32) * EXPERT_BLOCK
    last_start = jnp.maximum(n_used - 1, 0) * EXPERT_BLOCK
    be = (jnp.minimum(bstart, last_start)[:, None] >= reg_end[None, :]).sum(axis=1).astype(I32)
    be = jnp.minimum(be, n_experts - 1)
    per_big = cp // BIG_CHUNK
    per_small = (cp % BIG_CHUNK) // SUBLANES
    tail = per_big * BIG_CHUNK
    return dict(
        big_list=_copy_list(per_big, seg, off, BIG_CHUNK, _max_big(n_experts)),
        small_list=_copy_list(per_small, seg + tail, off + tail, SUBLANES, _max_small(n_experts)),
        n_big=per_big.sum(axis=1).astype(I32), n_small=per_small.sum(axis=1).astype(I32),
        gap_start=(reg_start + run_rows).astype(I32), gap=(reg - run_rows).astype(I32),
        lpos=lpos.astype(I32), block_expert=be, n_used=n_used.reshape(1), n_blocks=n_blocks,
    )


def _max_big(n_experts):
    return _local_rows(n_experts) // BIG_CHUNK


def _max_small(n_experts):
    return n_experts * (BIG_CHUNK // SUBLANES - 1)


def _copy_list(per_run, local_row, sorted_row, chunk, max_copies):
    ends = jnp.cumsum(per_run, axis=1)[:, None, :]
    first = ends - per_run[:, None, :]
    k = jnp.arange(max_copies, dtype=I32)[None, :, None]
    mine = (first <= k) & (k < ends)
    pick = lambda v: jnp.sum(jnp.where(mine, v[:, None, :], 0), axis=2)
    j = k[:, :, 0] - pick(first[:, 0, :])
    src = pick(local_row) + chunk * j
    dst = pick(sorted_row) + chunk * j
    return (dst * LIST_RADIX + src).reshape(-1).astype(I32)


def _run_copies(tile, nbig_ref, nsmall_ref, big_ref, small_ref, max_big, max_small, make_big, make_small):
    def start(make, packed):
        local_row = pl.multiple_of(packed & (LIST_RADIX - 1), SUBLANES)
        sorted_row = pl.multiple_of(lax.shift_right_logical(packed, LIST_SHIFT), SUBLANES)
        make(local_row, sorted_row).start()

    def big(k, c):
        start(make_big, big_ref[tile * max_big + k])
        return c

    def small(k, c):
        start(make_small, small_ref[tile * max_small + k])
        return c

    lax.fori_loop(0, nbig_ref[tile], big, 0)
    lax.fori_loop(0, nsmall_ref[tile], small, 0)


def _wait_copies(tile, nbig_ref, nsmall_ref, make_big, make_small):
    def wb(j, c):
        make_big(0, 0).wait()
        return c

    def ws(j, c):
        make_small(0, 0).wait()
        return c

    lax.fori_loop(0, nbig_ref[tile], wb, 0)
    lax.fori_loop(0, nsmall_ref[tile], ws, 0)


def _dispatch_kernel(big_ref, small_ref, nbig_ref, nsmall_ref, gs_ref, gap_ref, nu_ref,
                     lpt_ref, h2_ref, xs_ref, buf_ref, zero_ref, sem, zsem, *, n_experts, local_rows, n_blocks):
    i = pl.program_id(0)
    n = pl.num_programs(0)
    slot = i % 2
    tps = MOE_TILES_PER_STEP

    def copies(sl, sub):
        def big(src, dst):
            return pltpu.make_async_copy(buf_ref.at[sl, sub, pl.ds(src, BIG_CHUNK), :], xs_ref.at[pl.ds(dst, BIG_CHUNK), :],
                                         sem.at[sl])

        def small(src, dst):
            return pltpu.make_async_copy(buf_ref.at[sl, sub, pl.ds(src, SUBLANES), :], xs_ref.at[pl.ds(dst, SUBLANES), :],
                                         sem.at[sl])

        return big, small

    def wait_step(step, sl):
        for sub in range(tps):
            _wait_copies(step * tps + sub, nbig_ref, nsmall_ref, *copies(sl, sub))

    @pl.when(i >= 2)
    def _():
        wait_step(i - 2, slot)

    for sub in range(tps):
        lpt = lpt_ref[sub]
        rows = lax.broadcasted_iota(I32, (local_rows, TT), 0)
        hit = rows == lpt[0:1, :]
        for k in range(1, TOP_K):
            hit = hit | (rows == lpt[k:k + 1, :])
        p = jnp.where(hit, 1.0, 0.0).astype(BF16)
        buf_ref[slot, sub] = jnp.dot(p, h2_ref[sub * TT:(sub + 1) * TT, :], preferred_element_type=F32)
        _run_copies(i * tps + sub, nbig_ref, nsmall_ref, big_ref, small_ref, _max_big(n_experts), _max_small(n_experts),
                    *copies(slot, sub))

    @pl.when(i == n - 1)
    def _():
        zero_ref[...] = jnp.zeros_like(zero_ref)
        sizes = []
        size = EXPERT_BLOCK // 2
        while size >= SUBLANES:
            sizes.append(size)
            size //= 2

        def zcopy(dst, size):
            return pltpu.make_async_copy(zero_ref.at[pl.ds(0, size), :], xs_ref.at[pl.ds(dst, size), :], zsem)

        def per_expert(e, carry):
            gap = gap_ref[e]
            pos = gs_ref[e]
            for size in sizes:
                take = (gap & size) != 0

                @pl.when(take)
                def _():
                    zcopy(pl.multiple_of(pos, SUBLANES), size).start()

                pos = pos + jnp.where(take, size, 0)
            return carry

        lax.fori_loop(0, n_experts, per_expert, 0)

        zrows = zero_ref.shape[0]
        per_block = EXPERT_BLOCK // zrows
        tail_copies = (n_blocks - nu_ref[0]) * per_block

        def tail(t, carry):
            zcopy(pl.multiple_of(nu_ref[0] * EXPERT_BLOCK + t * zrows, SUBLANES), zrows).start()
            return carry

        lax.fori_loop(0, tail_copies, tail, 0)

        def per_expert_wait(e, carry):
            gap = gap_ref[e]
            for size in sizes:
                @pl.when((gap & size) != 0)
                def _():
                    zcopy(0, size).wait()
            return carry

        lax.fori_loop(0, n_experts, per_expert_wait, 0)

        def tail_wait(t, carry):
            zcopy(0, zrows).wait()
            return carry

        lax.fori_loop(0, tail_copies, tail_wait, 0)

        @pl.when(i >= 1)
        def _():
            wait_step(i - 1, 1 - slot)

        wait_step(i, slot)


def _dispatch(h2, meta, n_tiles, n_experts):
    n, d = h2.shape
    local_rows = _local_rows(n_experts)
    n_rows = meta["n_blocks"] * EXPERT_BLOCK
    lpt = meta["lpos"].reshape(n_tiles, TT, TOP_K).transpose(0, 2, 1)
    tps = MOE_TILES_PER_STEP
    assert n_tiles % tps == 0
    grid_spec = pltpu.PrefetchScalarGridSpec(
        num_scalar_prefetch=7,
        grid=(n_tiles // tps,),
        in_specs=[
            pl.BlockSpec((tps, TOP_K, TT), lambda j, *_: (j, 0, 0)),
            pl.BlockSpec((tps * TT, d), lambda j, *_: (j, 0)),
        ],
        out_specs=pl.BlockSpec(memory_space=pl.ANY),
        scratch_shapes=[
            pltpu.VMEM((2, tps, local_rows, d), F32),
            pltpu.VMEM((EXPERT_BLOCK // 2, d), F32),
            pltpu.SemaphoreType.DMA((2,)),
            pltpu.SemaphoreType.DMA(()),
        ],
    )
    return pl.pallas_call(
        functools.partial(_dispatch_kernel, n_experts=n_experts, local_rows=local_rows, n_blocks=meta["n_blocks"]),
        grid_spec=grid_spec,
        out_shape=jax.ShapeDtypeStruct((n_rows, d), F32),
        compiler_params=_cparams(("arbitrary",)),
        name="moe_dispatch",
    )(meta["big_list"], meta["small_list"], meta["n_big"], meta["n_small"], meta["gap_start"], meta["gap"], meta["n_used"],
      lpt, h2)


def _ffn_kernel(be_ref, nu_ref, x_ref, wgu_ref, wd_ref, sel_ref, bg_ref, bu_ref, bd_ref, y_ref, wg_s, wu_s, wd_s):
    b = pl.program_id(0)
    used = b < nu_ref[0]

    @pl.when(used & ((b == 0) | (be_ref[b] != be_ref[jnp.maximum(b - 1, 0)])))
    def _():
        sel = sel_ref[...]
        pair = 2 * LANES
        for c in range(wgu_ref.shape[1] // pair):
            chunk = wgu_ref[:, c * pair:(c + 1) * pair].astype(BF16)
            de = jnp.dot(chunk, sel, preferred_element_type=F32)
            wg_s[:, c * LANES:(c + 1) * LANES] = de[:, :LANES].astype(BF16)
            wu_s[:, c * LANES:(c + 1) * LANES] = de[:, LANES:].astype(BF16)
        wd_s[...] = wd_ref[...].astype(BF16)

    @pl.when(used)
    def _():
        x = x_ref[...].astype(BF16)
        gate = jnp.dot(x, wg_s[...], preferred_element_type=F32) + bg_ref[...]
        up = jnp.dot(x, wu_s[...], preferred_element_type=F32) + bu_ref[...]
        gate = jnp.minimum(gate, SWIGLU_LIMIT)
        up = jnp.clip(up, -SWIGLU_LIMIT, SWIGLU_LIMIT)
        glu = gate / (1.0 + jnp.exp(-SWIGLU_ALPHA * gate))
        act = ((up + 1.0) * glu).astype(BF16)
        y_ref[...] = jnp.dot(act, wd_s[...], preferred_element_type=F32) + bd_ref[...]

    @pl.when(jnp.logical_not(used))
    def _():
        y_ref[...] = jnp.zeros_like(y_ref)


def _ffn(xs, meta, layer, w_gu, w_down, b_gate, b_up, b_down):
    n_rows, d = xs.shape
    n_blocks = meta["n_blocks"]
    f = w_down.shape[2]
    r = jnp.arange(2 * LANES)
    src = jnp.where(r < LANES, 2 * r, 2 * (r - LANES) + 1)
    sel = (r[:, None] == src[None, :]).astype(BF16)
    xmap = lambda b, be, nu: (jnp.minimum(b, jnp.maximum(nu[0] - 1, 0)), 0)
    wmap = lambda b, be, nu: (layer, be[b], 0, 0)
    grid_spec = pltpu.PrefetchScalarGridSpec(
        num_scalar_prefetch=2,
        grid=(n_blocks,),
        in_specs=[
            pl.BlockSpec((EXPERT_BLOCK, d), xmap),
            pl.BlockSpec((None, None, d, 2 * f), wmap),
            pl.BlockSpec((None, None, f, d), wmap),
            pl.BlockSpec((2 * LANES, 2 * LANES), lambda b, be, nu: (0, 0)),
            pl.BlockSpec((None, None, 1, f), wmap),
            pl.BlockSpec((None, None, 1, f), wmap),
            pl.BlockSpec((None, None, 1, d), wmap),
        ],
        out_specs=pl.BlockSpec((EXPERT_BLOCK, d), lambda b, be, nu: (b, 0)),
        scratch_shapes=[pltpu.VMEM((d, f), BF16), pltpu.VMEM((d, f), BF16), pltpu.VMEM((f, d), BF16)],
    )
    return pl.pallas_call(
        _ffn_kernel,
        grid_spec=grid_spec,
        out_shape=jax.ShapeDtypeStruct((n_rows, d), F32),
        compiler_params=_cparams(("arbitrary",)),
        name="moe_ffn",
    )(meta["block_expert"], meta["n_used"], xs, w_gu, w_down, sel, b_gate, b_up, b_down)


def _combine_kernel(big_ref, small_ref, nbig_ref, nsmall_ref,
                    lp_ref, w_ref, s_ref, mod_ref, gf_ref, ys_ref, o_ref, buf_ref, sem, *, n_experts, local_rows, final):
    i = pl.program_id(0)
    n = pl.num_programs(0)
    slot = i % 2
    tps = MOE_TILES_PER_STEP

    def copies(sl, sub):
        def big(dst, src):
            return pltpu.make_async_copy(ys_ref.at[pl.ds(src, BIG_CHUNK), :], buf_ref.at[sl, sub, pl.ds(dst, BIG_CHUNK), :],
                                         sem.at[sl])

        def small(dst, src):
            return pltpu.make_async_copy(ys_ref.at[pl.ds(src, SUBLANES), :], buf_ref.at[sl, sub, pl.ds(dst, SUBLANES), :],
                                         sem.at[sl])

        return big, small

    def start_step(step, sl):
        for sub in range(tps):
            _run_copies(step * tps + sub, nbig_ref, nsmall_ref, big_ref, small_ref, _max_big(n_experts),
                        _max_small(n_experts), *copies(sl, sub))

    def wait_step(step, sl):
        for sub in range(tps):
            _wait_copies(step * tps + sub, nbig_ref, nsmall_ref, *copies(sl, sub))

    @pl.when(i == 0)
    def _():
        buf_ref[...] = jnp.zeros_like(buf_ref)
        start_step(0, 0)

    @pl.when(i + 1 < n)
    def _():
        start_step(i + 1, 1 - slot)

    wait_step(i, slot)

    m = mod_ref[...]
    for sub in range(tps):
        rows = slice(sub * TT, (sub + 1) * TT)
        lp = lp_ref[rows, :]
        w = w_ref[rows, :]
        lanes = lax.broadcasted_iota(I32, (TT, local_rows), 1)
        pw = jnp.zeros((TT, local_rows), F32)
        for k in range(TOP_K):
            pw = pw + jnp.where(lanes == lp[:, k:k + 1], w[:, k:k + 1], 0.0)
        y = buf_ref[slot, sub].astype(BF16)
        f = jnp.dot(pw.astype(BF16), y, preferred_element_type=F32)
        s = s_ref[rows, :] + m[5:6] * f
        if final:
            s = _rms(s) * gf_ref[...]
        o_ref[rows, :] = s


def _combine(ys, meta, top_w, stream, mods4, g_final, geom, layer, n_tiles, n_experts, final):
    d = stream.shape[1]
    local_rows = _local_rows(n_experts)
    tps = MOE_TILES_PER_STEP
    assert n_tiles % tps == 0
    grid_spec = pltpu.PrefetchScalarGridSpec(
        num_scalar_prefetch=4,
        grid=(n_tiles // tps,),
        in_specs=[
            pl.BlockSpec((tps * TT, TOP_K), lambda j, *_: (j, 0)),
            pl.BlockSpec((tps * TT, TOP_K), lambda j, *_: (j, 0)),
            pl.BlockSpec((tps * TT, d), lambda j, *_: (j, 0)),
            pl.BlockSpec((None, None, N_MOD, d), lambda j, *_: (layer, geom.group(j, tps * TT), 0, 0)),
            pl.BlockSpec((1, d), lambda j, *_: (0, 0)),
            pl.BlockSpec(memory_space=pl.ANY),
        ],
        out_specs=pl.BlockSpec((tps * TT, d), lambda j, *_: (j, 0)),
        scratch_shapes=[
            pltpu.VMEM((2, tps, local_rows, d), F32),
            pltpu.SemaphoreType.DMA((2,)),
        ],
    )
    return pl.pallas_call(
        functools.partial(_combine_kernel, n_experts=n_experts, local_rows=local_rows, final=final),
        grid_spec=grid_spec,
        out_shape=jax.ShapeDtypeStruct((n_tiles * TT, d), F32),
        compiler_params=_cparams(("arbitrary",)),
        name="moe_combine",
    )(meta["big_list"], meta["small_list"], meta["n_big"], meta["n_small"],
      meta["lpos"], top_w, stream, mods4, g_final, ys)


def kernel(x, c, ctx, c_ctx, w_mod, b_mod, g_mix, g_ffn, g_final, f_w_in, f_w_out, mla_w_in, mla_g_qa, mla_w_qb,
           mla_g_kva, mla_w_kvb, mla_w_o, gqa_w_qkv, gqa_g_q, gqa_g_k, gqa_w_o, moe_w_router, moe_b_router,
           moe_w_gu, moe_b_gu, moe_w_down, moe_b_down):
    batch, seq, d = x.shape
    n_ctx = ctx.shape[1]
    depth = w_mod.shape[0]
    n_experts = moe_w_router.shape[2]
    geom = _Geom(batch, seq, n_ctx)

    n_groups = _round_up(batch + 1, SUBLANES)
    cc = jnp.concatenate([c, c_ctx[None, :], jnp.zeros((n_groups - batch - 1, d), F32)], axis=0)
    mods4 = _mods(cc, w_mod, b_mod).reshape(depth, n_groups, N_MOD, d)

    stream = jnp.concatenate([x.reshape(batch * seq, d), ctx.reshape(batch * n_ctx, d)], axis=0)

    gd = d // FOURIER_GROUPS
    gcos, gsin = _dft_tables(gd)
    eye = jnp.eye(FOURIER_GROUPS, dtype=F32)
    bd_cs = jnp.concatenate([jnp.kron(eye, gcos), jnp.kron(eye, gsin)], axis=1)

    f = moe_w_down.shape[2]
    b_gate = moe_b_gu[:, :, 0::2].reshape(depth, n_experts, 1, f)
    b_up = moe_b_gu[:, :, 1::2].reshape(depth, n_experts, 1, f)
    b_down = moe_b_down.reshape(depth, n_experts, 1, d)

    for i in range(depth):
        kind, j = i % N_MIXERS, i // N_MIXERS
        last = i == depth - 1
        ctx_used = (kind != 0) or (not last)
        n_tiles = geom.n_tiles if ctx_used else geom.n_lat_tiles
        gm = g_mix[i].reshape(1, d)
        gf = g_ffn[i].reshape(1, d)

        if kind == 0:
            wcs = _fold(f_w_in[j], bd_cs, BF16)
            u = _pre_fourier(stream, mods4, gm, wcs, geom, i, n_tiles * TT)
            o = _dft(u, 0, seq, batch, d)
            if ctx_used:
                o = jnp.concatenate([o, _dft(u, geom.n_lat, n_ctx, batch, d)], axis=0)
            w_o = f_w_out[j].astype(BF16)
        elif kind == 1:
            q, k, v = _pre_mla(stream, mods4, gm, mla_w_in[j], mla_g_qa[j], mla_w_qb[j], mla_g_kva[j], mla_w_kvb[j], geom, i)
            o = _attention(q, k, v, geom, groups=2, n_heads=MLA_HEADS // 2, kv_heads=MLA_HEADS // 2, dk=MLA_SLOT, dv=MLA_V,
                           k_head_major=False, context_queries=not last)
            w_o = mla_w_o[j].astype(BF16)
        else:
            q, k, v = _pre_gqa(stream, mods4, gm, gqa_w_qkv[j], gqa_g_q[j], gqa_g_k[j], geom, i)
            o = _attention(q, k, v, geom, groups=1, n_heads=GQA_HEADS, kv_heads=GQA_KV_HEADS, dk=GQA_HEAD_DIM,
                           dv=GQA_HEAD_DIM, k_head_major=True, context_queries=not last)
            w_o = gqa_w_o[j].astype(BF16)

        n_moe_tiles = geom.n_lat_tiles if last else geom.n_tiles
        stream, h2, lp_slab, w_slab, cnt_slab = _post(o, w_o, stream, mods4, gf, moe_w_router[i], moe_b_router[i], geom, i,
                                                      n_moe_tiles)
        meta = _route_meta(cnt_slab[:, 0, :n_experts], lp_slab[:, :TOP_K], n_experts)
        xs = _dispatch(h2, meta, n_moe_tiles, n_experts)
        ys = _ffn(xs, meta, i, moe_w_gu, moe_w_down, b_gate, b_up, b_down)
        stream = _combine(ys, meta, w_slab[:, :TOP_K], stream, mods4, g_final.reshape(1, d), geom, i, n_moe_tiles, n_experts,
                          last)

    return stream[:batch * seq].reshape(batch, seq, d)
```

```python
import functools

import jax
import jax.numpy as jnp
from jax import lax
from jax.experimental import pallas as pl
from jax.experimental.pallas import tpu as pltpu

F32 = jnp.float32
BF16 = jnp.bfloat16
I32 = jnp.int32
U32 = jnp.uint32
HI = lax.Precision.HIGHEST

GRID_W = 64
N_MIXERS = 3
NORM_EPS = 1e-6
ROPE_THETA = 10000.0
FOURIER_GROUPS = 8
MLA_HEADS = 16
MLA_Q_LORA = 384
MLA_KV_LORA = 256
MLA_NOPE = 64
MLA_ROPE = 32
MLA_V = 64
GQA_HEADS = 16
GQA_KV_HEADS = 4
GQA_HEAD_DIM = 64
TOP_K = 4
SWIGLU_LIMIT = 7.0
SWIGLU_ALPHA = 1.702
N_MOD = 6

LANES = 128
SUBLANES = 8
TT = 256
PROJ_TILE = 512
KEY_CHUNK = 1024
LOG2_E = 1.4426950408889634
EXPERT_BLOCK = 512
MOE_TILES_PER_STEP = 2
BIG_CHUNK = 32
LIST_SHIFT = 11
LIST_RADIX = 1 << LIST_SHIFT
MLA_SLOT = 128
VMEM_LIMIT = 56 * 1024 * 1024


def _cparams(sem):
    return pltpu.CompilerParams(dimension_semantics=sem, vmem_limit_bytes=VMEM_LIMIT)


def _round_up(x, m):
    return (x + m - 1) // m * m


def _norm_mod(x, g, shift, scale):
    y = x * lax.rsqrt(jnp.mean(x * x, axis=-1, keepdims=True) + NORM_EPS) * g
    return y * (1.0 + scale) + shift


def _rms(x):
    return x * lax.rsqrt(jnp.mean(x * x, axis=-1, keepdims=True) + NORM_EPS)


def _pack_bf16_pairs(x):
    bits = lax.bitcast_convert_type(x, U32)
    half = x.shape[1] // 2
    return (bits[:, :half] & jnp.uint32(0xFFFF0000)) | lax.shift_right_logical(bits[:, half:], jnp.uint32(16))


def _unpack_bf16_pairs(u):
    hi = lax.bitcast_convert_type(u & jnp.uint32(0xFFFF0000), F32)
    lo = lax.bitcast_convert_type(lax.shift_left(u, jnp.uint32(16)), F32)
    return jnp.concatenate([hi, lo], axis=1).astype(BF16)


def _mod_kernel(cc_ref, w_ref, b_ref, o_ref):
    cc = cc_ref[...]
    a = cc / (1.0 + jnp.exp(-cc))
    o_ref[...] = jnp.dot(a, w_ref[...], precision=HI, preferred_element_type=F32) + b_ref[...]


def _mods(cc, w_mod, b_mod):
    depth, d, n6 = w_mod.shape
    g = cc.shape[0]
    tn = 1536 if n6 % 1536 == 0 else n6
    return pl.pallas_call(
        _mod_kernel,
        grid=(depth, n6 // tn),
        in_specs=[
            pl.BlockSpec((g, d), lambda i, n: (0, 0)),
            pl.BlockSpec((None, d, tn), lambda i, n: (i, 0, n)),
            pl.BlockSpec((None, 1, tn), lambda i, n: (i, 0, n)),
        ],
        out_specs=pl.BlockSpec((None, g, tn), lambda i, n: (i, 0, n)),
        out_shape=jax.ShapeDtypeStruct((depth, g, n6), F32),
        compiler_params=_cparams(("parallel", "parallel")),
        name="mods",
    )(cc, w_mod, b_mod.reshape(depth, 1, n6))


def _fold_kernel(a_ref, b_ref, o_ref):
    o_ref[...] = jnp.dot(a_ref[...], b_ref[...], precision=HI, preferred_element_type=F32).astype(o_ref.dtype)


def _fold(a, b, out_dtype):
    m, k = a.shape
    n = b.shape[1]
    tn = 512
    return pl.pallas_call(
        _fold_kernel,
        grid=(n // tn,),
        in_specs=[pl.BlockSpec((m, k), lambda j: (0, 0)), pl.BlockSpec((k, tn), lambda j: (0, j))],
        out_specs=pl.BlockSpec((m, tn), lambda j: (0, j)),
        out_shape=jax.ShapeDtypeStruct((m, n), out_dtype),
        compiler_params=_cparams(("parallel",)),
        name="fold",
    )(a, b)


class _Geom:
    def __init__(self, batch, seq, n_ctx):
        self.batch, self.seq, self.n_ctx = batch, seq, n_ctx
        assert seq % TT == 0 and n_ctx % TT == 0 and seq % GRID_W == 0
        self.lat_per_b = seq // TT
        self.ctx_per_b = n_ctx // TT
        self.n_lat_tiles = batch * self.lat_per_b
        self.n_ctx_tiles = batch * self.ctx_per_b
        self.n_tiles = self.n_lat_tiles + self.n_ctx_tiles
        self.n_lat = batch * seq
        self.n_tok = self.n_lat + batch * n_ctx

    def group(self, j, tile=TT):
        return jnp.where(j < self.n_lat // tile, j // (self.seq // tile), self.batch)

    def pos_tile(self, j, tile=TT):
        return jnp.where(j < self.n_lat // tile, j % (self.seq // tile), self.seq // tile)


def _mod_spec(geom, layer, d, tile=TT):
    return pl.BlockSpec((None, None, N_MOD, d), lambda j: (layer, geom.group(j, tile), 0, 0))


FOURIER_TILE = 1024


def _pre_fourier_kernel(s_ref, mod_ref, g_ref, perm_ref, w_ref, u_ref):
    m = mod_ref[...]
    h = _norm_mod(s_ref[...], g_ref[...], m[0:1], m[1:2]).astype(BF16)
    h = jnp.dot(perm_ref[...], h, preferred_element_type=F32).astype(BF16)
    u_ref[...] = jnp.dot(h, w_ref[...], preferred_element_type=F32).astype(BF16)


def _pre_fourier(stream, mods4, g_mix, wcs, geom, layer, n_tok):
    d = stream.shape[1]
    assert n_tok % FOURIER_TILE == 0 and geom.seq % FOURIER_TILE == 0 and FOURIER_TILE % geom.n_ctx == 0
    lat_tiles = geom.n_lat // FOURIER_TILE
    per_b = geom.seq // FOURIER_TILE
    group = lambda j: jnp.where(j < lat_tiles, j // per_b, geom.batch)
    r = jnp.arange(FOURIER_TILE)
    half = FOURIER_TILE // 2
    src = jnp.where(r < half, 2 * r, 2 * (r - half) + 1)
    perm = (src[:, None] == r[None, :]).astype(BF16)
    return pl.pallas_call(
        _pre_fourier_kernel,
        grid=(n_tok // FOURIER_TILE,),
        in_specs=[
            pl.BlockSpec((FOURIER_TILE, d), lambda j: (j, 0)),
            pl.BlockSpec((None, None, N_MOD, d), lambda j: (layer, group(j), 0, 0)),
            pl.BlockSpec((1, d), lambda j: (0, 0)),
            pl.BlockSpec((FOURIER_TILE, FOURIER_TILE), lambda j: (0, 0)),
            pl.BlockSpec((d, 2 * d), lambda j: (0, 0)),
        ],
        out_specs=pl.BlockSpec((FOURIER_TILE, 2 * d), lambda j: (j, 0)),
        out_shape=jax.ShapeDtypeStruct((n_tok, 2 * d), BF16),
        compiler_params=_cparams(("parallel",)),
        name="pre_fourier",
    )(stream, mods4, g_mix, perm, wcs)


def _dft_kernel(ce_ref, se_ref, co_ref, so_ref, ue_ref, uo_ref, o_ref, acce_ref, acco_ref, *, d):
    k = pl.program_id(2)

    @pl.when(k == 0)
    def _():
        acce_ref[...] = jnp.zeros_like(acce_ref)
        acco_ref[...] = jnp.zeros_like(acco_ref)

    acce_ref[...] += (jnp.dot(ce_ref[...], ue_ref[:, :d], preferred_element_type=F32)
                      - jnp.dot(se_ref[...], ue_ref[:, d:], preferred_element_type=F32))
    acco_ref[...] += (jnp.dot(co_ref[...], uo_ref[:, :d], preferred_element_type=F32)
                      - jnp.dot(so_ref[...], uo_ref[:, d:], preferred_element_type=F32))

    @pl.when(k == pl.num_programs(2) - 1)
    def _():
        o_ref[0] = (acce_ref[...] + acco_ref[...]).astype(o_ref.dtype)
        o_ref[1] = (acce_ref[...] - acco_ref[...]).astype(o_ref.dtype)


def _dft_tables(n, rows=None, col_step=1, col_off=0):
    rows = n if rows is None else rows
    cols = n // col_step
    scale = n ** -0.5

    def tables(n_rows, step):
        k = lax.broadcasted_iota(I32, (n_rows, cols), 0) * step
        c = lax.broadcasted_iota(I32, (n_rows, cols), 1) * col_step + col_off
        ang = ((k * c) % n).astype(F32) * (2.0 * jnp.pi / n)
        return jnp.cos(ang), jnp.sin(ang)

    coarse = 64
    if rows <= 4 * coarse:
        c, s = tables(rows, 1)
        return c * scale, s * scale
    ca, sa = tables(rows // coarse, coarse)
    cb, sb = tables(coarse, 1)
    ca, sa = ca[:, None, :] * scale, sa[:, None, :] * scale
    cos = (ca * cb[None] - sa * sb[None]).reshape(rows, cols)
    sin = (sa * cb[None] + ca * sb[None]).reshape(rows, cols)
    return cos, sin


def _dft(u, tok0, seq, batch, d):
    half = seq // 2
    ce, se = _dft_tables(seq, half, 2, 0)
    co, so = _dft_tables(seq, half, 2, 1)
    tabs = [t.astype(BF16) for t in (ce, se, co, so)]
    kb = min(FOURIER_TILE // 2, half)
    ksteps = half // kb
    tm = min(half, 1024)
    per_tile = FOURIER_TILE // kb

    def even_block(b, k):
        tok = tok0 + b * seq + k * FOURIER_TILE
        return (tok // FOURIER_TILE) * per_tile + (tok % FOURIER_TILE) // (2 * kb)

    tspec = pl.BlockSpec((tm, kb), lambda b, m, k: (m, k))
    out = pl.pallas_call(
        functools.partial(_dft_kernel, d=d),
        grid=(batch, half // tm, ksteps),
        in_specs=[
            tspec, tspec, tspec, tspec,
            pl.BlockSpec((kb, 2 * d), lambda b, m, k: (even_block(b, k), 0)),
            pl.BlockSpec((kb, 2 * d), lambda b, m, k: (even_block(b, k) + per_tile // 2, 0)),
        ],
        out_specs=pl.BlockSpec((None, 2, tm, d), lambda b, m, k: (b, 0, m, 0)),
        out_shape=jax.ShapeDtypeStruct((batch, 2, half, d), BF16),
        scratch_shapes=[pltpu.VMEM((tm, d), F32), pltpu.VMEM((tm, d), F32)],
        compiler_params=_cparams(("parallel", "parallel", "arbitrary")),
        name="dft",
    )(*tabs, u, u)
    return out.reshape(batch * seq, d)


def _pre_mla_kernel(s_ref, mod_ref, g_ref, win_ref, gqa_ref, gkva_ref, wqb_ref, wqbs_ref, wkvk_ref, wkvv_ref,
                    cos_ref, sin_ref, q_ref, k_ref, v_ref, *, scale):
    m = mod_ref[...]
    h = _norm_mod(s_ref[...], g_ref[...], m[0:1], m[1:2])
    a = jnp.dot(h.astype(BF16), win_ref[...], preferred_element_type=F32)
    aq = (_rms(a[:, :MLA_Q_LORA]) * gqa_ref[...]).astype(BF16)
    ckv = (_rms(a[:, MLA_Q_LORA:MLA_Q_LORA + MLA_KV_LORA]) * gkva_ref[...]).astype(BF16)
    o = MLA_Q_LORA + MLA_KV_LORA
    cos = cos_ref[...]
    sin = sin_ref[...]
    kpe = a[:, o:o + MLA_SLOT] * cos + a[:, o + MLA_SLOT:o + 2 * MLA_SLOT] * sin
    cos_h = jnp.tile(cos, (1, MLA_HEADS))
    sin_h = jnp.tile(sin, (1, MLA_HEADS))
    q = jnp.dot(aq, wqb_ref[...], preferred_element_type=F32)
    qs = jnp.dot(aq, wqbs_ref[...], preferred_element_type=F32)
    q_ref[...] = ((q * cos_h + qs * sin_h) * scale).astype(BF16)
    kk = jnp.dot(ckv, wkvk_ref[...], preferred_element_type=F32)
    k_ref[...] = (kk + jnp.tile(kpe, (1, MLA_HEADS))).astype(BF16)
    v_ref[...] = lax.dot_general(wkvv_ref[...], ckv, (((1,), (1,)), ((), ())), preferred_element_type=F32).astype(BF16)


def _rope_perm(rot):
    sec = rot // 2
    half = sec // 2
    d = jnp.arange(rot)
    first = (d % sec) < half
    partner = jnp.where(first, d + half, d - half)
    sign = jnp.where(first, -1.0, 1.0).astype(F32)
    return partner, sign


def _rope_tables(seq, rot, tile):
    rows = seq // GRID_W
    row = jnp.repeat(jnp.arange(rows, dtype=F32), GRID_W)
    col = jnp.tile(jnp.arange(GRID_W, dtype=F32), rows)
    n_freq = rot // 4
    inv_freq = ROPE_THETA ** (-jnp.arange(n_freq, dtype=F32) / n_freq)
    ang = jnp.stack([row[:, None] * inv_freq, col[:, None] * inv_freq], axis=1)
    cos = jnp.cos(ang)
    sin = jnp.sin(ang)
    cos_full = jnp.concatenate([cos, cos], axis=-1).reshape(seq, rot)
    sin_full = jnp.concatenate([sin, sin], axis=-1).reshape(seq, rot)
    _, sign = _rope_perm(rot)
    sin_full = sin_full * sign
    ident_c = jnp.ones((tile, rot), F32)
    ident_s = jnp.zeros((tile, rot), F32)
    assert seq % tile == 0
    return jnp.concatenate([cos_full, ident_c], 0), jnp.concatenate([sin_full, ident_s], 0)


def _pre_mla(stream, mods4, g_mix, w_in, g_qa, w_qb, g_kva, w_kvb, geom, layer):
    d = stream.shape[1]
    hd = MLA_NOPE + MLA_ROPE
    partner, _ = _rope_perm(MLA_ROPE)
    o = MLA_Q_LORA + MLA_KV_LORA
    kpe_w = w_in[:, o:]
    z_lo = jnp.zeros((d, MLA_NOPE), F32)
    z_hi = jnp.zeros((d, MLA_SLOT - hd), F32)
    w_in_ext = jnp.concatenate([w_in[:, :o], z_lo, kpe_w, z_hi, z_lo, kpe_w[:, partner], z_hi], axis=1).astype(BF16)
    wq = w_qb.reshape(MLA_Q_LORA, MLA_HEADS, hd)
    zq = jnp.zeros((MLA_Q_LORA, MLA_HEADS, MLA_SLOT - hd), F32)
    wq_p = jnp.concatenate([wq, zq], axis=2).reshape(MLA_Q_LORA, MLA_HEADS * MLA_SLOT).astype(BF16)
    wq_s = jnp.concatenate([jnp.zeros_like(wq[:, :, :MLA_NOPE]), wq[:, :, MLA_NOPE:][:, :, partner], zq], axis=2)
    wq_s = wq_s.reshape(MLA_Q_LORA, MLA_HEADS * MLA_SLOT).astype(BF16)
    wkv = w_kvb.reshape(MLA_KV_LORA, MLA_HEADS, MLA_NOPE + MLA_V)
    zk = jnp.zeros((MLA_KV_LORA, MLA_HEADS, MLA_SLOT - MLA_NOPE), F32)
    wkv_k = jnp.concatenate([wkv[:, :, :MLA_NOPE], zk], axis=2).reshape(MLA_KV_LORA, MLA_HEADS * MLA_SLOT).astype(BF16)
    wkv_v = wkv[:, :, MLA_NOPE:].reshape(MLA_KV_LORA, MLA_HEADS * MLA_V).T.astype(BF16)
    cos, sin = _rope_tables(geom.seq, MLA_ROPE, PROJ_TILE)
    rows = cos.shape[0]
    cos_slot = jnp.concatenate([jnp.ones((rows, MLA_NOPE), F32), cos, jnp.ones((rows, MLA_SLOT - hd), F32)], axis=1)
    sin_slot = jnp.concatenate([jnp.zeros((rows, MLA_NOPE), F32), sin, jnp.zeros((rows, MLA_SLOT - hd), F32)], axis=1)
    n_tiles = geom.n_tok // PROJ_TILE
    wq_w = MLA_HEADS * MLA_SLOT
    wv_w = MLA_HEADS * MLA_V
    full = lambda a: pl.BlockSpec(a.shape, lambda j: (0,) * a.ndim)
    g_qa2, g_kva2 = g_qa.reshape(1, -1), g_kva.reshape(1, -1)
    return pl.pallas_call(
        functools.partial(_pre_mla_kernel, scale=float(hd) ** -0.5 * LOG2_E),
        grid=(n_tiles,),
        in_specs=[
            pl.BlockSpec((PROJ_TILE, d), lambda j: (j, 0)),
            _mod_spec(geom, layer, d, PROJ_TILE),
            pl.BlockSpec((1, d), lambda j: (0, 0)),
            full(w_in_ext), full(g_qa2), full(g_kva2), full(wq_p), full(wq_s), full(wkv_k), full(wkv_v),
            pl.BlockSpec((PROJ_TILE, MLA_SLOT), lambda j: (geom.pos_tile(j, PROJ_TILE), 0)),
            pl.BlockSpec((PROJ_TILE, MLA_SLOT), lambda j: (geom.pos_tile(j, PROJ_TILE), 0)),
        ],
        out_specs=[
            pl.BlockSpec((PROJ_TILE, wq_w), lambda j: (j, 0)),
            pl.BlockSpec((PROJ_TILE, wq_w), lambda j: (j, 0)),
            pl.BlockSpec((wv_w, PROJ_TILE), lambda j: (0, j)),
        ],
        out_shape=[
            jax.ShapeDtypeStruct((n_tiles * PROJ_TILE, wq_w), BF16),
            jax.ShapeDtypeStruct((n_tiles * PROJ_TILE, wq_w), BF16),
            jax.ShapeDtypeStruct((wv_w, n_tiles * PROJ_TILE), BF16),
        ],
        compiler_params=_cparams(("parallel",)),
        name="pre_mla",
    )(stream, mods4, g_mix, w_in_ext, g_qa2, g_kva2, wq_p, wq_s, wkv_k, wkv_v, cos_slot, sin_slot)


def _split_hi_lo(x):
    hi = x.astype(BF16)
    lo = (x - hi.astype(F32)).astype(BF16)
    return hi, lo


def _pre_gqa_kernel(s_ref, mod_ref, g_ref, w_ref, wvt_ref, bd_ref, cq_ref, sq_ref, ck_ref, sk_ref, q_ref, k_ref, v_ref, *,
                    scale):
    m = mod_ref[...]
    h = _norm_mod(s_ref[...], g_ref[...], m[0:1], m[1:2]).astype(BF16)
    a = jnp.dot(h, w_ref[...], preferred_element_type=F32)
    nq = GQA_HEADS * GQA_HEAD_DIM
    nkv = GQA_KV_HEADS * GQA_HEAD_DIM
    aq, ak = a[:, :nq], a[:, nq:nq + nkv]
    aqs, aks = a[:, nq + nkv:2 * nq + nkv], a[:, 2 * nq + nkv:]
    bd = bd_ref[...]

    def head_rs(x, width):
        hi, lo = _split_hi_lo(x * x)
        b = bd[:width, :width]
        ssq = jnp.dot(hi, b, preferred_element_type=F32) + jnp.dot(lo, b, preferred_element_type=F32)
        return lax.rsqrt(ssq * (1.0 / GQA_HEAD_DIM) + NORM_EPS)

    reps_q = nq // cq_ref.shape[1]
    reps_k = nkv // ck_ref.shape[1]
    q = head_rs(aq, nq) * (aq * jnp.tile(cq_ref[...], (1, reps_q)) + aqs * jnp.tile(sq_ref[...], (1, reps_q)))
    q_ref[...] = (q * scale).astype(BF16)
    k = head_rs(ak, nkv) * (ak * jnp.tile(ck_ref[...], (1, reps_k)) + aks * jnp.tile(sk_ref[...], (1, reps_k)))
    k = k.astype(BF16)
    for g in range(GQA_KV_HEADS):
        k_ref[g] = k[:, g * GQA_HEAD_DIM:(g + 1) * GQA_HEAD_DIM]
    v_ref[...] = lax.dot_general(wvt_ref[...], h, (((1,), (1,)), ((), ())), preferred_element_type=F32).astype(BF16)


def _pre_gqa(stream, mods4, g_mix, w_qkv, g_q, g_k, geom, layer):
    d = stream.shape[1]
    nq = GQA_HEADS * GQA_HEAD_DIM
    nkv = GQA_KV_HEADS * GQA_HEAD_DIM
    partner, _ = _rope_perm(GQA_HEAD_DIM)
    wq = w_qkv[:, :nq].reshape(d, GQA_HEADS, GQA_HEAD_DIM)
    wk = w_qkv[:, nq:nq + nkv].reshape(d, GQA_KV_HEADS, GQA_HEAD_DIM)
    w_ext = jnp.concatenate([w_qkv[:, :nq + nkv], wq[:, :, partner].reshape(d, nq), wk[:, :, partner].reshape(d, nkv)],
                            axis=1).astype(BF16)
    w_vt = w_qkv[:, nq + nkv:].T.astype(BF16)
    cos, sin = _rope_tables(geom.seq, GQA_HEAD_DIM, PROJ_TILE)
    per = LANES // GQA_HEAD_DIM
    cq = jnp.tile(cos * g_q[None, :], (1, per))
    sq = jnp.tile(sin * g_q[partner][None, :], (1, per))
    ck = jnp.tile(cos * g_k[None, :], (1, per))
    sk = jnp.tile(sin * g_k[partner][None, :], (1, per))
    hid = jnp.arange(nq) // GQA_HEAD_DIM
    bd = (hid[:, None] == hid[None, :]).astype(BF16)
    n_tiles = geom.n_tok // PROJ_TILE
    full = lambda a: pl.BlockSpec(a.shape, lambda j: (0,) * a.ndim)
    tab = pl.BlockSpec((PROJ_TILE, LANES), lambda j: (geom.pos_tile(j, PROJ_TILE), 0))
    return pl.pallas_call(
        functools.partial(_pre_gqa_kernel, scale=float(GQA_HEAD_DIM) ** -0.5 * LOG2_E),
        grid=(n_tiles,),
        in_specs=[
            pl.BlockSpec((PROJ_TILE, d), lambda j: (j, 0)),
            _mod_spec(geom, layer, d, PROJ_TILE),
            pl.BlockSpec((1, d), lambda j: (0, 0)),
            full(w_ext), full(w_vt), full(bd), tab, tab, tab, tab,
        ],
        out_specs=[
            pl.BlockSpec((PROJ_TILE, nq), lambda j: (j, 0)),
            pl.BlockSpec((GQA_KV_HEADS, PROJ_TILE, GQA_HEAD_DIM), lambda j: (0, j, 0)),
            pl.BlockSpec((nkv, PROJ_TILE), lambda j: (0, j)),
        ],
        out_shape=[
            jax.ShapeDtypeStruct((n_tiles * PROJ_TILE, nq), BF16),
            jax.ShapeDtypeStruct((GQA_KV_HEADS, n_tiles * PROJ_TILE, GQA_HEAD_DIM), BF16),
            jax.ShapeDtypeStruct((nkv, n_tiles * PROJ_TILE), BF16),
        ],
        compiler_params=_cparams(("parallel",)),
        name="pre_gqa",
    )(stream, mods4, g_mix, w_ext, w_vt, bd, cq, sq, ck, sk)


def _attn_kernel(q_ref, kl_ref, vl_ref, kc_ref, vc_ref, o_ref, s_ref, *, n_lat_steps, **head_args):
    t = pl.program_id(2)

    @pl.when(t < n_lat_steps)
    def _():
        _attn_tile(q_ref, kl_ref, vl_ref, kc_ref, vc_ref, o_ref, s_ref, has_lat=True, **head_args)

    @pl.when(t >= n_lat_steps)
    def _():
        _attn_tile(q_ref, kl_ref, vl_ref, kc_ref, vc_ref, o_ref, s_ref, has_lat=False, **head_args)


def _attn_tile(q_ref, kl_ref, vl_ref, kc_ref, vc_ref, o_ref, s_ref, *, n_heads, kv_heads, dk, dv, has_lat):
    nt = (((1,), (1,)), ((), ()))
    tq = q_ref.shape[0]
    chunks, row = [], 0
    for k_ref, v_ref in ([(kc_ref, vc_ref), (kl_ref, vl_ref)] if has_lat else [(kc_ref, vc_ref)]):
        n_keys = k_ref.shape[-2]
        for c0 in range(0, n_keys, KEY_CHUNK):
            kc = min(KEY_CHUNK, n_keys - c0)
            chunks.append((k_ref, v_ref, c0, row, kc))
            row += kc
    mx = [None] * n_heads
    den = [None] * n_heads
    acc = [None] * n_heads
    for phase in range(n_heads + 1):
        for k_ref, v_ref, c0, row, kc in chunks:
            if phase < n_heads:
                g = phase
                gk = g * kv_heads // n_heads
                q = q_ref[:, g * dk:(g + 1) * dk]
                k = k_ref[gk, c0:c0 + kc, :] if k_ref.ndim == 3 else k_ref[c0:c0 + kc, gk * dk:(gk + 1) * dk]
                s = lax.dot_general(k, q, nt, preferred_element_type=F32)
                s_ref[g % 2, row:row + kc, :] = s
                cmax = jnp.max(s, axis=0, keepdims=True)
                mx[g] = cmax if mx[g] is None else jnp.maximum(mx[g], cmax)
            if phase > 0:
                g = phase - 1
                gk = g * kv_heads // n_heads
                p = jnp.exp2(s_ref[g % 2, row:row + kc, :] - mx[g])
                psum = jnp.sum(p, axis=0, keepdims=True)
                pv = jnp.dot(v_ref[gk * dv:(gk + 1) * dv, c0:c0 + kc], p.astype(BF16), preferred_element_type=F32)
                den[g] = psum if den[g] is None else den[g] + psum
                acc[g] = pv if acc[g] is None else acc[g] + pv
    o_all = jnp.concatenate([acc[g] / den[g] for g in range(n_heads)], axis=0)
    o_ref[...] = jnp.transpose(o_all).astype(o_ref.dtype)


def _attention(q, k, vt, geom, *, groups, n_heads, kv_heads, dk, dv, k_head_major, context_queries, single_buffer_kv=False):
    wq, wk, wv = n_heads * dk, kv_heads * dk, kv_heads * dv
    wo = n_heads * dv
    batch = geom.batch
    nc = geom.n_ctx
    ctx_blk0 = geom.n_lat // nc
    assert geom.n_lat % nc == 0

    mode = dict(pipeline_mode=pl.Buffered(1)) if single_buffer_kv else {}

    def kspec(rows, tok_blk):
        if k_head_major:
            return pl.BlockSpec((kv_heads, rows, dk), lambda b, h, *_: (h, tok_blk(b), 0), **mode)
        return pl.BlockSpec((rows, wk), lambda b, h, *_: (tok_blk(b), h), **mode)

    def vspec(rows, tok_blk):
        return pl.BlockSpec((wv, rows), lambda b, h, *_: (h, tok_blk(b)), **mode)

    tq = TT
    qt = geom.seq // tq
    assert nc == tq or not context_queries
    steps = qt + 1 if context_queries else qt
    qrow = lambda b, t: jnp.where(t < qt, b * qt + t, ctx_blk0 + b)
    return pl.pallas_call(
        functools.partial(_attn_kernel, n_lat_steps=qt, n_heads=n_heads, kv_heads=kv_heads, dk=dk, dv=dv),
        grid=(batch, groups, steps),
        in_specs=[
            pl.BlockSpec((tq, wq), lambda b, h, t: (qrow(b, t), h)),
            kspec(geom.seq, lambda b: b),
            vspec(geom.seq, lambda b: b),
            kspec(nc, lambda b: ctx_blk0 + b),
            vspec(nc, lambda b: ctx_blk0 + b),
        ],
        out_specs=pl.BlockSpec((tq, wo), lambda b, h, t: (qrow(b, t), h)),
        out_shape=jax.ShapeDtypeStruct((geom.n_tok if context_queries else geom.n_lat, groups * wo), BF16),
        scratch_shapes=[pltpu.VMEM((2, geom.seq + nc, tq), F32)],
        compiler_params=_cparams(("parallel", "parallel", "arbitrary")),
        name="attention",
    )(q, k, vt, k, vt)


def _post_kernel(o_ref, wo_ref, s_ref, mod_ref, g_ref, wrh_ref, wrl_ref, br_ref, s_out, h2_out, idx_out, w_out, cnt_out, *, n_experts):
    m = mod_ref[...]
    s = s_ref[...] + m[2:3] * jnp.dot(o_ref[...], wo_ref[...], preferred_element_type=F32)
    s_out[...] = s
    h2 = _norm_mod(s, g_ref[...], m[3:4], m[4:5])
    h2_out[...] = h2.astype(BF16)
    h2_hi, h2_lo = _split_hi_lo(h2)
    logits = (jnp.dot(h2_hi, wrh_ref[...], preferred_element_type=F32) + jnp.dot(h2_lo, wrh_ref[...], preferred_element_type=F32)
              + jnp.dot(h2_hi, wrl_ref[...], preferred_element_type=F32) + br_ref[...])
    lane = lax.broadcasted_iota(I32, logits.shape, 1).astype(F32)
    work = jnp.where(lane < n_experts, logits, -jnp.inf)
    vals, hits = [], []
    for k in range(TOP_K):
        mx = jnp.max(work, axis=-1, keepdims=True)
        sel = jnp.min(jnp.where(work == mx, lane, float(LANES)), axis=-1, keepdims=True)
        hit = lane == sel
        vals.append(mx)
        hits.append(hit)
        work = jnp.where(hit, -jnp.inf, work)
    es = [jnp.exp(v - vals[0]) for v in vals]
    den = es[0] + es[1] + es[2] + es[3]
    w_slab = jnp.zeros(logits.shape, F32)
    for k in range(TOP_K):
        w_slab = jnp.where(lane == k, es[k] / den, w_slab)
    w_out[...] = w_slab
    onehot = jnp.where(hits[0] | hits[1] | hits[2] | hits[3], 1.0, 0.0).astype(BF16)
    tr = lax.broadcasted_iota(I32, (TT, TT), 0)
    tc = lax.broadcasted_iota(I32, (TT, TT), 1)
    earlier = jnp.where(tc < tr, 1.0, 0.0).astype(BF16)
    er = lax.broadcasted_iota(I32, (LANES, LANES), 0)
    ec = lax.broadcasted_iota(I32, (LANES, LANES), 1)
    lower = jnp.where(er < ec, 1.0, 0.0).astype(BF16)
    bases = []
    for t in range(onehot.shape[0] // TT):
        oh = onehot[t * TT:(t + 1) * TT]
        rank = jnp.dot(earlier, oh, preferred_element_type=F32)
        cnt = jnp.sum(oh.astype(F32), axis=0, keepdims=True)
        cp = jnp.floor((cnt + (SUBLANES - 1)) * (1.0 / SUBLANES)) * SUBLANES
        seg = jnp.dot(jnp.broadcast_to(cp, (SUBLANES, LANES)).astype(BF16), lower, preferred_element_type=F32)[0:1]
        bases.append(seg + rank)
        cnt_out[t] = jnp.broadcast_to(cnt, (SUBLANES, LANES)).astype(I32)
    base = jnp.concatenate(bases, axis=0)
    lp_slab = jnp.zeros(logits.shape, F32)
    for k in range(TOP_K):
        lp_slab = jnp.where(lane == k, jnp.sum(jnp.where(hits[k], base, 0.0), axis=-1, keepdims=True), lp_slab)
    idx_out[...] = lp_slab.astype(I32)


def _post(o, w_o, stream, mods4, g_ffn, w_router, b_router, geom, layer, n_tiles):
    d = stream.shape[1]
    wo_w = o.shape[1]
    n_experts = w_router.shape[1]
    wr_hi, wr_lo = _split_hi_lo(jnp.concatenate([w_router, jnp.zeros((d, LANES - n_experts), F32)], axis=1))
    br = jnp.concatenate([b_router, jnp.zeros((LANES - n_experts,), F32)]).reshape(1, LANES)
    n = n_tiles * TT
    return pl.pallas_call(
        functools.partial(_post_kernel, n_experts=n_experts),
        grid=(n // PROJ_TILE,),
        in_specs=[
            pl.BlockSpec((PROJ_TILE, wo_w), lambda j: (j, 0)),
            pl.BlockSpec((wo_w, d), lambda j: (0, 0)),
            pl.BlockSpec((PROJ_TILE, d), lambda j: (j, 0)),
            _mod_spec(geom, layer, d, PROJ_TILE),
            pl.BlockSpec((1, d), lambda j: (0, 0)),
            pl.BlockSpec((d, LANES), lambda j: (0, 0)),
            pl.BlockSpec((d, LANES), lambda j: (0, 0)),
            pl.BlockSpec((1, LANES), lambda j: (0, 0)),
        ],
        out_specs=[
            pl.BlockSpec((PROJ_TILE, d), lambda j: (j, 0)),
            pl.BlockSpec((PROJ_TILE, d), lambda j: (j, 0)),
            pl.BlockSpec((PROJ_TILE, LANES), lambda j: (j, 0)),
            pl.BlockSpec((PROJ_TILE, LANES), lambda j: (j, 0)),
            pl.BlockSpec((PROJ_TILE // TT, SUBLANES, LANES), lambda j: (j, 0, 0)),
        ],
        out_shape=[
            jax.ShapeDtypeStruct((n, d), F32),
            jax.ShapeDtypeStruct((n, d), BF16),
            jax.ShapeDtypeStruct((n, LANES), I32),
            jax.ShapeDtypeStruct((n, LANES), F32),
            jax.ShapeDtypeStruct((n_tiles, SUBLANES, LANES), I32),
        ],
        compiler_params=_cparams(("parallel",)),
        name="post",
    )(o, w_o, stream, mods4, g_ffn, wr_hi, wr_lo, br)


def _local_rows(n_experts):
    return _round_up(TT * TOP_K + n_experts * (SUBLANES - 1), LANES)


def _n_blocks(n_tok, n_tiles, n_experts):
    rows = n_tok * TOP_K + n_experts * n_tiles * (SUBLANES - 1) + n_experts * (EXPERT_BLOCK - 1)
    return pl.cdiv(rows, EXPERT_BLOCK)


def _route_meta(cnt, lpos, n_experts):
    n_tiles = cnt.shape[0]
    n_blocks = _n_blocks(n_tiles * TT, n_tiles, n_experts)
    cp = _round_up(cnt, SUBLANES)
    seg = jnp.cumsum(cp, axis=1) - cp
    run_rows = cp.sum(axis=0)
    reg = _round_up(run_rows, EXPERT_BLOCK)
    reg_end = jnp.cumsum(reg)
    reg_start = reg_end - reg
    off = reg_start[None, :] + jnp.cumsum(cp, axis=0) - cp
    n_used = (reg_end[-1] // EXPERT_BLOCK).astype(I32)
    bstart = jnp.arange(n_blocks, dtype=I32) * EXPERT_BLOCK
    last_start = jnp.maximum(n_used - 1, 0) * EXPERT_BLOCK
    be = (jnp.minimum(bstart, last_start)[:, None] >= reg_end[None, :]).sum(axis=1).astype(I32)
    be = jnp.minimum(be, n_experts - 1)
    per_big = cp // BIG_CHUNK
    per_small = (cp % BIG_CHUNK) // SUBLANES
    tail = per_big * BIG_CHUNK
    return dict(
        big_list=_copy_list(per_big, seg, off, BIG_CHUNK, _max_big(n_experts)),
        small_list=_copy_list(per_small, seg + tail, off + tail, SUBLANES, _max_small(n_experts)),
        n_big=per_big.sum(axis=1).astype(I32), n_small=per_small.sum(axis=1).astype(I32),
        gap_start=(reg_start + run_rows).astype(I32), gap=(reg - run_rows).astype(I32),
        lpos=lpos.astype(I32), block_expert=be, n_used=n_used.reshape(1), n_blocks=n_blocks,
    )


def _max_big(n_experts):
    return _local_rows(n_experts) // BIG_CHUNK


def _max_small(n_experts):
    return n_experts * (BIG_CHUNK // SUBLANES - 1)


def _copy_list(per_run, local_row, sorted_row, chunk, max_copies):
    ends = jnp.cumsum(per_run, axis=1)[:, None, :]
    first = ends - per_run[:, None, :]
    k = jnp.arange(max_copies, dtype=I32)[None, :, None]
    mine = (first <= k) & (k < ends)
    pick = lambda v: jnp.sum(jnp.where(mine, v[:, None, :], 0), axis=2)
    j = k[:, :, 0] - pick(first[:, 0, :])
    src = pick(local_row) + chunk * j
    dst = pick(sorted_row) + chunk * j
    return (dst * LIST_RADIX + src).reshape(-1).astype(I32)


def _run_copies(tile, nbig_ref, nsmall_ref, big_ref, small_ref, max_big, max_small, make_big, make_small):
    def start(make, packed):
        local_row = pl.multiple_of(packed & (LIST_RADIX - 1), SUBLANES)
        sorted_row = pl.multiple_of(lax.shift_right_logical(packed, LIST_SHIFT), SUBLANES)
        make(local_row, sorted_row).start()

    def big(k, c):
        start(make_big, big_ref[tile * max_big + k])
        return c

    def small(k, c):
        start(make_small, small_ref[tile * max_small + k])
        return c

    lax.fori_loop(0, nbig_ref[tile], big, 0)
    lax.fori_loop(0, nsmall_ref[tile], small, 0)


def _wait_copies(tile, nbig_ref, nsmall_ref, make_big, make_small):
    def wb(j, c):
        make_big(0, 0).wait()
        return c

    def ws(j, c):
        make_small(0, 0).wait()
        return c

    lax.fori_loop(0, nbig_ref[tile], wb, 0)
    lax.fori_loop(0, nsmall_ref[tile], ws, 0)


def _dispatch_kernel(big_ref, small_ref, nbig_ref, nsmall_ref, gs_ref, gap_ref, nu_ref,
                     lpt_ref, h2_ref, xs_ref, buf_ref, zero_ref, sem, zsem, *, n_experts, local_rows, n_blocks):
    i = pl.program_id(0)
    n = pl.num_programs(0)
    slot = i % 2
    tps = MOE_TILES_PER_STEP

    def copies(sl, sub):
        def big(src, dst):
            return pltpu.make_async_copy(buf_ref.at[sl, sub, pl.ds(src, BIG_CHUNK), :], xs_ref.at[pl.ds(dst, BIG_CHUNK), :],
                                         sem.at[sl])

        def small(src, dst):
            return pltpu.make_async_copy(buf_ref.at[sl, sub, pl.ds(src, SUBLANES), :], xs_ref.at[pl.ds(dst, SUBLANES), :],
                                         sem.at[sl])

        return big, small

    def wait_step(step, sl):
        for sub in range(tps):
            _wait_copies(step * tps + sub, nbig_ref, nsmall_ref, *copies(sl, sub))

    @pl.when(i >= 2)
    def _():
        wait_step(i - 2, slot)

    for sub in range(tps):
        lpt = lpt_ref[sub]
        rows = lax.broadcasted_iota(I32, (local_rows, TT), 0)
        hit = rows == lpt[0:1, :]
        for k in range(1, TOP_K):
            hit = hit | (rows == lpt[k:k + 1, :])
        p = jnp.where(hit, 1.0, 0.0).astype(BF16)
        buf_ref[slot, sub] = _pack_bf16_pairs(jnp.dot(p, h2_ref[sub * TT:(sub + 1) * TT, :], preferred_element_type=F32))
        _run_copies(i * tps + sub, nbig_ref, nsmall_ref, big_ref, small_ref, _max_big(n_experts), _max_small(n_experts),
                    *copies(slot, sub))

    @pl.when(i == n - 1)
    def _():
        zero_ref[...] = jnp.zeros_like(zero_ref)
        sizes = []
        size = EXPERT_BLOCK // 2
        while size >= SUBLANES:
            sizes.append(size)
            size //= 2

        def zcopy(dst, size):
            return pltpu.make_async_copy(zero_ref.at[pl.ds(0, size), :], xs_ref.at[pl.ds(dst, size), :], zsem)

        def per_expert(e, carry):
            gap = gap_ref[e]
            pos = gs_ref[e]
            for size in sizes:
                take = (gap & size) != 0

                @pl.when(take)
                def _():
                    zcopy(pl.multiple_of(pos, SUBLANES), size).start()

                pos = pos + jnp.where(take, size, 0)
            return carry

        lax.fori_loop(0, n_experts, per_expert, 0)

        zrows = zero_ref.shape[0]
        per_block = EXPERT_BLOCK // zrows
        tail_copies = (n_blocks - nu_ref[0]) * per_block

        def tail(t, carry):
            zcopy(pl.multiple_of(nu_ref[0] * EXPERT_BLOCK + t * zrows, SUBLANES), zrows).start()
            return carry

        lax.fori_loop(0, tail_copies, tail, 0)

        def per_expert_wait(e, carry):
            gap = gap_ref[e]
            for size in sizes:
                @pl.when((gap & size) != 0)
                def _():
                    zcopy(0, size).wait()
            return carry

        lax.fori_loop(0, n_experts, per_expert_wait, 0)

        def tail_wait(t, carry):
            zcopy(0, zrows).wait()
            return carry

        lax.fori_loop(0, tail_copies, tail_wait, 0)

        @pl.when(i >= 1)
        def _():
            wait_step(i - 1, 1 - slot)

        wait_step(i, slot)


def _dispatch(h2, meta, n_tiles, n_experts):
    n, d = h2.shape
    local_rows = _local_rows(n_experts)
    n_rows = meta["n_blocks"] * EXPERT_BLOCK
    lpt = meta["lpos"].reshape(n_tiles, TT, TOP_K).transpose(0, 2, 1)
    tps = MOE_TILES_PER_STEP
    assert n_tiles % tps == 0
    grid_spec = pltpu.PrefetchScalarGridSpec(
        num_scalar_prefetch=7,
        grid=(n_tiles // tps,),
        in_specs=[
            pl.BlockSpec((tps, TOP_K, TT), lambda j, *_: (j, 0, 0)),
            pl.BlockSpec((tps * TT, d), lambda j, *_: (j, 0)),
        ],
        out_specs=pl.BlockSpec(memory_space=pl.ANY),
        scratch_shapes=[
            pltpu.VMEM((2, tps, local_rows, d // 2), U32),
            pltpu.VMEM((EXPERT_BLOCK // 2, d // 2), U32),
            pltpu.SemaphoreType.DMA((2,)),
            pltpu.SemaphoreType.DMA(()),
        ],
    )
    return pl.pallas_call(
        functools.partial(_dispatch_kernel, n_experts=n_experts, local_rows=local_rows, n_blocks=meta["n_blocks"]),
        grid_spec=grid_spec,
        out_shape=jax.ShapeDtypeStruct((n_rows, d // 2), U32),
        compiler_params=_cparams(("arbitrary",)),
        name="moe_dispatch",
    )(meta["big_list"], meta["small_list"], meta["n_big"], meta["n_small"], meta["gap_start"], meta["gap"], meta["n_used"],
      lpt, h2)


def _ffn_kernel(be_ref, nu_ref, x_ref, wgu_ref, wd_ref, sel_ref, bg_ref, bu_ref, bd_ref, y_ref, wg_s, wu_s, wd_s):
    b = pl.program_id(0)
    used = b < nu_ref[0]

    @pl.when(used & ((b == 0) | (be_ref[b] != be_ref[jnp.maximum(b - 1, 0)])))
    def _():
        sel = sel_ref[...]
        pair = 2 * LANES
        for c in range(wgu_ref.shape[1] // pair):
            chunk = wgu_ref[:, c * pair:(c + 1) * pair].astype(BF16)
            de = jnp.dot(chunk, sel, preferred_element_type=F32)
            wg_s[:, c * LANES:(c + 1) * LANES] = de[:, :LANES].astype(BF16)
            wu_s[:, c * LANES:(c + 1) * LANES] = de[:, LANES:].astype(BF16)
        wd_s[...] = wd_ref[...].astype(BF16)

    @pl.when(used)
    def _():
        x = _unpack_bf16_pairs(x_ref[...])
        gate = jnp.dot(x, wg_s[...], preferred_element_type=F32) + bg_ref[...]
        up = jnp.dot(x, wu_s[...], preferred_element_type=F32) + bu_ref[...]
        gate = jnp.minimum(gate, SWIGLU_LIMIT)
        up = jnp.clip(up, -SWIGLU_LIMIT, SWIGLU_LIMIT)
        glu = gate / (1.0 + jnp.exp(-SWIGLU_ALPHA * gate))
        act = ((up + 1.0) * glu).astype(BF16)
        y = jnp.dot(act, wd_s[...], preferred_element_type=F32) + bd_ref[...]
        y_ref[...] = _pack_bf16_pairs(y.astype(BF16).astype(F32))

    @pl.when(jnp.logical_not(used))
    def _():
        y_ref[...] = jnp.zeros_like(y_ref)


def _ffn(xs, meta, layer, w_gu, w_down, b_gate, b_up, b_down):
    n_rows = xs.shape[0]
    n_blocks = meta["n_blocks"]
    f, d = w_down.shape[2:]
    r = jnp.arange(2 * LANES)
    src = jnp.where(r < LANES, 2 * r, 2 * (r - LANES) + 1)
    sel = (r[:, None] == src[None, :]).astype(BF16)
    xmap = lambda b, be, nu: (jnp.minimum(b, jnp.maximum(nu[0] - 1, 0)), 0)
    wmap = lambda b, be, nu: (layer, be[b], 0, 0)
    grid_spec = pltpu.PrefetchScalarGridSpec(
        num_scalar_prefetch=2,
        grid=(n_blocks,),
        in_specs=[
            pl.BlockSpec((EXPERT_BLOCK, d // 2), xmap),
            pl.BlockSpec((None, None, d, 2 * f), wmap),
            pl.BlockSpec((None, None, f, d), wmap),
            pl.BlockSpec((2 * LANES, 2 * LANES), lambda b, be, nu: (0, 0)),
            pl.BlockSpec((None, None, 1, f), wmap),
            pl.BlockSpec((None, None, 1, f), wmap),
            pl.BlockSpec((None, None, 1, d), wmap),
        ],
        out_specs=pl.BlockSpec((EXPERT_BLOCK, d // 2), lambda b, be, nu: (b, 0)),
        scratch_shapes=[pltpu.VMEM((d, f), BF16), pltpu.VMEM((d, f), BF16), pltpu.VMEM((f, d), BF16)],
    )
    return pl.pallas_call(
        _ffn_kernel,
        grid_spec=grid_spec,
        out_shape=jax.ShapeDtypeStruct((n_rows, d // 2), U32),
        compiler_params=_cparams(("arbitrary",)),
        name="moe_ffn",
    )(meta["block_expert"], meta["n_used"], xs, w_gu, w_down, sel, b_gate, b_up, b_down)


def _combine_kernel(big_ref, small_ref, nbig_ref, nsmall_ref,
                    lp_ref, w_ref, s_ref, mod_ref, gf_ref, ys_ref, o_ref, buf_ref, sem, *, n_experts, local_rows, final):
    i = pl.program_id(0)
    n = pl.num_programs(0)
    slot = i % 2
    tps = MOE_TILES_PER_STEP

    def copies(sl, sub):
        def big(dst, src):
            return pltpu.make_async_copy(ys_ref.at[pl.ds(src, BIG_CHUNK), :], buf_ref.at[sl, sub, pl.ds(dst, BIG_CHUNK), :],
                                         sem.at[sl])

        def small(dst, src):
            return pltpu.make_async_copy(ys_ref.at[pl.ds(src, SUBLANES), :], buf_ref.at[sl, sub, pl.ds(dst, SUBLANES), :],
                                         sem.at[sl])

        return big, small

    def start_step(step, sl):
        for sub in range(tps):
            _run_copies(step * tps + sub, nbig_ref, nsmall_ref, big_ref, small_ref, _max_big(n_experts),
                        _max_small(n_experts), *copies(sl, sub))

    def wait_step(step, sl):
        for sub in range(tps):
            _wait_copies(step * tps + sub, nbig_ref, nsmall_ref, *copies(sl, sub))

    @pl.when(i == 0)
    def _():
        buf_ref[...] = jnp.zeros_like(buf_ref)
        start_step(0, 0)

    @pl.when(i + 1 < n)
    def _():
        start_step(i + 1, 1 - slot)

    wait_step(i, slot)

    m = mod_ref[...]
    for sub in range(tps):
        rows = slice(sub * TT, (sub + 1) * TT)
        lp = lp_ref[rows, :]
        w = w_ref[rows, :]
        lanes = lax.broadcasted_iota(I32, (TT, local_rows), 1)
        pw = jnp.zeros((TT, local_rows), F32)
        for k in range(TOP_K):
            pw = pw + jnp.where(lanes == lp[:, k:k + 1], w[:, k:k + 1], 0.0)
        y = _unpack_bf16_pairs(buf_ref[slot, sub])
        f = jnp.dot(pw.astype(BF16), y, preferred_element_type=F32)
        s = s_ref[rows, :] + m[5:6] * f
        if final:
            s = _rms(s) * gf_ref[...]
        o_ref[rows, :] = s


def _combine(ys, meta, top_w, stream, mods4, g_final, geom, layer, n_tiles, n_experts, final):
    d = stream.shape[1]
    local_rows = _local_rows(n_experts)
    tps = MOE_TILES_PER_STEP
    assert n_tiles % tps == 0
    grid_spec = pltpu.PrefetchScalarGridSpec(
        num_scalar_prefetch=4,
        grid=(n_tiles // tps,),
        in_specs=[
            pl.BlockSpec((tps * TT, TOP_K), lambda j, *_: (j, 0)),
            pl.BlockSpec((tps * TT, TOP_K), lambda j, *_: (j, 0)),
            pl.BlockSpec((tps * TT, d), lambda j, *_: (j, 0)),
            pl.BlockSpec((None, None, N_MOD, d), lambda j, *_: (layer, geom.group(j, tps * TT), 0, 0)),
            pl.BlockSpec((1, d), lambda j, *_: (0, 0)),
            pl.BlockSpec(memory_space=pl.ANY),
        ],
        out_specs=pl.BlockSpec((tps * TT, d), lambda j, *_: (j, 0)),
        scratch_shapes=[
            pltpu.VMEM((2, tps, local_rows, d // 2), U32),
            pltpu.SemaphoreType.DMA((2,)),
        ],
    )
    return pl.pallas_call(
        functools.partial(_combine_kernel, n_experts=n_experts, local_rows=local_rows, final=final),
        grid_spec=grid_spec,
        out_shape=jax.ShapeDtypeStruct((n_tiles * TT, d), F32),
        compiler_params=_cparams(("arbitrary",)),
        name="moe_combine",
    )(meta["big_list"], meta["small_list"], meta["n_big"], meta["n_small"],
      meta["lpos"], top_w, stream, mods4, g_final, ys)


def kernel(x, c, ctx, c_ctx, w_mod, b_mod, g_mix, g_ffn, g_final, f_w_in, f_w_out, mla_w_in, mla_g_qa, mla_w_qb,
           mla_g_kva, mla_w_kvb, mla_w_o, gqa_w_qkv, gqa_g_q, gqa_g_k, gqa_w_o, moe_w_router, moe_b_router,
           moe_w_gu, moe_b_gu, moe_w_down, moe_b_down):
    batch, seq, d = x.shape
    n_ctx = ctx.shape[1]
    depth = w_mod.shape[0]
    n_experts = moe_w_router.shape[2]
    geom = _Geom(batch, seq, n_ctx)

    n_groups = _round_up(batch + 1, SUBLANES)
    cc = jnp.concatenate([c, c_ctx[None, :], jnp.zeros((n_groups - batch - 1, d), F32)], axis=0)
    mods4 = _mods(cc, w_mod, b_mod).reshape(depth, n_groups, N_MOD, d)

    stream = jnp.concatenate([x.reshape(batch * seq, d), ctx.reshape(batch * n_ctx, d)], axis=0)

    gd = d // FOURIER_GROUPS
    gcos, gsin = _dft_tables(gd)
    eye = jnp.eye(FOURIER_GROUPS, dtype=F32)
    bd_cs = jnp.concatenate([jnp.kron(eye, gcos), jnp.kron(eye, gsin)], axis=1)

    f = moe_w_down.shape[2]
    b_gate = moe_b_gu[:, :, 0::2].reshape(depth, n_experts, 1, f)
    b_up = moe_b_gu[:, :, 1::2].reshape(depth, n_experts, 1, f)
    b_down = moe_b_down.reshape(depth, n_experts, 1, d)

    for i in range(depth):
        kind, j = i % N_MIXERS, i // N_MIXERS
        last = i == depth - 1
        ctx_used = (kind != 0) or (not last)
        n_tiles = geom.n_tiles if ctx_used else geom.n_lat_tiles
        gm = g_mix[i].reshape(1, d)
        gf = g_ffn[i].reshape(1, d)

        if kind == 0:
            wcs = _fold(f_w_in[j], bd_cs, BF16)
            u = _pre_fourier(stream, mods4, gm, wcs, geom, i, n_tiles * TT)
            o = _dft(u, 0, seq, batch, d)
            if ctx_used:
                o = jnp.concatenate([o, _dft(u, geom.n_lat, n_ctx, batch, d)], axis=0)
            w_o = f_w_out[j].astype(BF16)
        elif kind == 1:
            q, k, v = _pre_mla(stream, mods4, gm, mla_w_in[j], mla_g_qa[j], mla_w_qb[j], mla_g_kva[j], mla_w_kvb[j], geom, i)
            o = _attention(q, k, v, geom, groups=1, n_heads=MLA_HEADS, kv_heads=MLA_HEADS, dk=MLA_SLOT, dv=MLA_V,
                           k_head_major=False, context_queries=not last, single_buffer_kv=True)
            w_o = mla_w_o[j].astype(BF16)
        else:
            q, k, v = _pre_gqa(stream, mods4, gm, gqa_w_qkv[j], gqa_g_q[j], gqa_g_k[j], geom, i)
            o = _attention(q, k, v, geom, groups=1, n_heads=GQA_HEADS, kv_heads=GQA_KV_HEADS, dk=GQA_HEAD_DIM,
                           dv=GQA_HEAD_DIM, k_head_major=True, context_queries=not last)
            w_o = gqa_w_o[j].astype(BF16)

        n_moe_tiles = geom.n_lat_tiles if last else geom.n_tiles
        stream, h2, lp_slab, w_slab, cnt_slab = _post(o, w_o, stream, mods4, gf, moe_w_router[i], moe_b_router[i], geom, i,
                                                      n_moe_tiles)
        meta = _route_meta(cnt_slab[:, 0, :n_experts], lp_slab[:, :TOP_K], n_experts)
        xs = _dispatch(h2, meta, n_moe_tiles, n_experts)
        ys = _ffn(xs, meta, i, moe_w_gu, moe_w_down, b_gate, b_up, b_down)
        stream = _combine(ys, meta, w_slab[:, :TOP_K], stream, mods4, g_final.reshape(1, d), geom, i, n_moe_tiles, n_experts,
                          last)

    return stream[:batch * seq].reshape(batch, seq, d)
```

```python
import functools

import jax
import jax.numpy as jnp
from jax import lax
from jax.experimental import pallas as pl
from jax.experimental.pallas import tpu as pltpu

F32 = jnp.float32
BF16 = jnp.bfloat16
I32 = jnp.int32
U32 = jnp.uint32
HI = lax.Precision.HIGHEST

GRID_W = 64
N_MIXERS = 3
NORM_EPS = 1e-6
ROPE_THETA = 10000.0
FOURIER_GROUPS = 8
MLA_HEADS = 16
MLA_Q_LORA = 384
MLA_KV_LORA = 256
MLA_NOPE = 64
MLA_ROPE = 32
MLA_V = 64
GQA_HEADS = 16
GQA_KV_HEADS = 4
GQA_HEAD_DIM = 64
TOP_K = 4
SWIGLU_LIMIT = 7.0
SWIGLU_ALPHA = 1.702
N_MOD = 6

LANES = 128
SUBLANES = 8
TT = 256
PROJ_TILE = 512
KEY_CHUNK = 2048
LOG2_E = 1.4426950408889634
EXPERT_BLOCK = 512
MOE_TILES_PER_STEP = 2
BIG_CHUNK = 32
LIST_SHIFT = 11
LIST_RADIX = 1 << LIST_SHIFT
MLA_SLOT = 128
VMEM_LIMIT = 56 * 1024 * 1024


def _cparams(sem):
    return pltpu.CompilerParams(dimension_semantics=sem, vmem_limit_bytes=VMEM_LIMIT)


def _round_up(x, m):
    return (x + m - 1) // m * m


def _norm_mod(x, g, shift, scale):
    y = x * lax.rsqrt(jnp.mean(x * x, axis=-1, keepdims=True) + NORM_EPS) * g
    return y * (1.0 + scale) + shift


def _rms(x):
    return x * lax.rsqrt(jnp.mean(x * x, axis=-1, keepdims=True) + NORM_EPS)


def _pack_bf16_pairs(x):
    bits = lax.bitcast_convert_type(x, U32)
    half = x.shape[1] // 2
    return (bits[:, :half] & jnp.uint32(0xFFFF0000)) | lax.shift_right_logical(bits[:, half:], jnp.uint32(16))


def _unpack_bf16_pairs(u):
    hi = lax.bitcast_convert_type(u & jnp.uint32(0xFFFF0000), F32)
    lo = lax.bitcast_convert_type(lax.shift_left(u, jnp.uint32(16)), F32)
    return jnp.concatenate([hi, lo], axis=1).astype(BF16)


def _mod_kernel(cc_ref, w_ref, b_ref, o_ref):
    cc = cc_ref[...]
    a = cc / (1.0 + jnp.exp(-cc))
    o_ref[...] = jnp.dot(a, w_ref[...], precision=HI, preferred_element_type=F32) + b_ref[...]


def _mods(cc, w_mod, b_mod):
    depth, d, n6 = w_mod.shape
    g = cc.shape[0]
    tn = 1536 if n6 % 1536 == 0 else n6
    return pl.pallas_call(
        _mod_kernel,
        grid=(depth, n6 // tn),
        in_specs=[
            pl.BlockSpec((g, d), lambda i, n: (0, 0)),
            pl.BlockSpec((None, d, tn), lambda i, n: (i, 0, n)),
            pl.BlockSpec((None, 1, tn), lambda i, n: (i, 0, n)),
        ],
        out_specs=pl.BlockSpec((None, g, tn), lambda i, n: (i, 0, n)),
        out_shape=jax.ShapeDtypeStruct((depth, g, n6), F32),
        compiler_params=_cparams(("parallel", "parallel")),
        name="mods",
    )(cc, w_mod, b_mod.reshape(depth, 1, n6))


def _fold_kernel(a_ref, b_ref, o_ref):
    o_ref[...] = jnp.dot(a_ref[...], b_ref[...], precision=HI, preferred_element_type=F32).astype(o_ref.dtype)


def _fold(a, b, out_dtype):
    m, k = a.shape
    n = b.shape[1]
    tn = 512
    return pl.pallas_call(
        _fold_kernel,
        grid=(n // tn,),
        in_specs=[pl.BlockSpec((m, k), lambda j: (0, 0)), pl.BlockSpec((k, tn), lambda j: (0, j))],
        out_specs=pl.BlockSpec((m, tn), lambda j: (0, j)),
        out_shape=jax.ShapeDtypeStruct((m, n), out_dtype),
        compiler_params=_cparams(("parallel",)),
        name="fold",
    )(a, b)


class _Geom:
    def __init__(self, batch, seq, n_ctx):
        self.batch, self.seq, self.n_ctx = batch, seq, n_ctx
        assert seq % TT == 0 and n_ctx % TT == 0 and seq % GRID_W == 0
        self.lat_per_b = seq // TT
        self.ctx_per_b = n_ctx // TT
        self.n_lat_tiles = batch * self.lat_per_b
        self.n_ctx_tiles = batch * self.ctx_per_b
        self.n_tiles = self.n_lat_tiles + self.n_ctx_tiles
        self.n_lat = batch * seq
        self.n_tok = self.n_lat + batch * n_ctx

    def group(self, j, tile=TT):
        return jnp.where(j < self.n_lat // tile, j // (self.seq // tile), self.batch)

    def pos_tile(self, j, tile=TT):
        return jnp.where(j < self.n_lat // tile, j % (self.seq // tile), self.seq // tile)


def _mod_spec(geom, layer, d, tile=TT):
    return pl.BlockSpec((None, None, N_MOD, d), lambda j: (layer, geom.group(j, tile), 0, 0))


FOURIER_TILE = 1024


def _pre_fourier_kernel(s_ref, mod_ref, g_ref, perm_ref, w_ref, u_ref):
    m = mod_ref[...]
    h = _norm_mod(s_ref[...], g_ref[...], m[0:1], m[1:2]).astype(BF16)
    h = jnp.dot(perm_ref[...], h, preferred_element_type=F32).astype(BF16)
    u_ref[...] = jnp.dot(h, w_ref[...], preferred_element_type=F32).astype(BF16)


def _pre_fourier(stream, mods4, g_mix, wcs, geom, layer, n_tok):
    d = stream.shape[1]
    assert n_tok % FOURIER_TILE == 0 and geom.seq % FOURIER_TILE == 0 and FOURIER_TILE % geom.n_ctx == 0
    lat_tiles = geom.n_lat // FOURIER_TILE
    per_b = geom.seq // FOURIER_TILE
    group = lambda j: jnp.where(j < lat_tiles, j // per_b, geom.batch)
    r = jnp.arange(FOURIER_TILE)
    half = FOURIER_TILE // 2
    src = jnp.where(r < half, 2 * r, 2 * (r - half) + 1)
    perm = (src[:, None] == r[None, :]).astype(BF16)
    return pl.pallas_call(
        _pre_fourier_kernel,
        grid=(n_tok // FOURIER_TILE,),
        in_specs=[
            pl.BlockSpec((FOURIER_TILE, d), lambda j: (j, 0)),
            pl.BlockSpec((None, None, N_MOD, d), lambda j: (layer, group(j), 0, 0)),
            pl.BlockSpec((1, d), lambda j: (0, 0)),
            pl.BlockSpec((FOURIER_TILE, FOURIER_TILE), lambda j: (0, 0)),
            pl.BlockSpec((d, 2 * d), lambda j: (0, 0)),
        ],
        out_specs=pl.BlockSpec((FOURIER_TILE, 2 * d), lambda j: (j, 0)),
        out_shape=jax.ShapeDtypeStruct((n_tok, 2 * d), BF16),
        compiler_params=_cparams(("parallel",)),
        name="pre_fourier",
    )(stream, mods4, g_mix, perm, wcs)


def _dft_kernel(ce_ref, se_ref, co_ref, so_ref, ue_ref, uo_ref, o_ref, acce_ref, acco_ref, *, d):
    k = pl.program_id(2)

    @pl.when(k == 0)
    def _():
        acce_ref[...] = jnp.zeros_like(acce_ref)
        acco_ref[...] = jnp.zeros_like(acco_ref)

    acce_ref[...] += (jnp.dot(ce_ref[...], ue_ref[:, :d], preferred_element_type=F32)
                      - jnp.dot(se_ref[...], ue_ref[:, d:], preferred_element_type=F32))
    acco_ref[...] += (jnp.dot(co_ref[...], uo_ref[:, :d], preferred_element_type=F32)
                      - jnp.dot(so_ref[...], uo_ref[:, d:], preferred_element_type=F32))

    @pl.when(k == pl.num_programs(2) - 1)
    def _():
        o_ref[0] = (acce_ref[...] + acco_ref[...]).astype(o_ref.dtype)
        o_ref[1] = (acce_ref[...] - acco_ref[...]).astype(o_ref.dtype)


def _dft_tables(n, rows=None, col_step=1, col_off=0):
    rows = n if rows is None else rows
    cols = n // col_step
    scale = n ** -0.5

    def tables(n_rows, step):
        k = lax.broadcasted_iota(I32, (n_rows, cols), 0) * step
        c = lax.broadcasted_iota(I32, (n_rows, cols), 1) * col_step + col_off
        ang = ((k * c) % n).astype(F32) * (2.0 * jnp.pi / n)
        return jnp.cos(ang), jnp.sin(ang)

    coarse = 64
    if rows <= 4 * coarse:
        c, s = tables(rows, 1)
        return c * scale, s * scale
    ca, sa = tables(rows // coarse, coarse)
    cb, sb = tables(coarse, 1)
    ca, sa = ca[:, None, :] * scale, sa[:, None, :] * scale
    cos = (ca * cb[None] - sa * sb[None]).reshape(rows, cols)
    sin = (sa * cb[None] + ca * sb[None]).reshape(rows, cols)
    return cos, sin


def _dft(u, tok0, seq, batch, d):
    half = seq // 2
    ce, se = _dft_tables(seq, half, 2, 0)
    co, so = _dft_tables(seq, half, 2, 1)
    tabs = [t.astype(BF16) for t in (ce, se, co, so)]
    kb = min(FOURIER_TILE // 2, half)
    ksteps = half // kb
    tm = min(half, 1024)
    per_tile = FOURIER_TILE // kb

    def even_block(b, k):
        tok = tok0 + b * seq + k * FOURIER_TILE
        return (tok // FOURIER_TILE) * per_tile + (tok % FOURIER_TILE) // (2 * kb)

    tspec = pl.BlockSpec((tm, kb), lambda b, m, k: (m, k))
    out = pl.pallas_call(
        functools.partial(_dft_kernel, d=d),
        grid=(batch, half // tm, ksteps),
        in_specs=[
            tspec, tspec, tspec, tspec,
            pl.BlockSpec((kb, 2 * d), lambda b, m, k: (even_block(b, k), 0)),
            pl.BlockSpec((kb, 2 * d), lambda b, m, k: (even_block(b, k) + per_tile // 2, 0)),
        ],
        out_specs=pl.BlockSpec((None, 2, tm, d), lambda b, m, k: (b, 0, m, 0)),
        out_shape=jax.ShapeDtypeStruct((batch, 2, half, d), BF16),
        scratch_shapes=[pltpu.VMEM((tm, d), F32), pltpu.VMEM((tm, d), F32)],
        compiler_params=_cparams(("parallel", "parallel", "arbitrary")),
        name="dft",
    )(*tabs, u, u)
    return out.reshape(batch * seq, d)


def _pre_mla_kernel(s_ref, mod_ref, g_ref, win_ref, gqa_ref, gkva_ref, wqb_ref, wqbs_ref, wkvk_ref, wkvv_ref,
                    cos_ref, sin_ref, q_ref, k_ref, v_ref, *, scale):
    m = mod_ref[...]
    h = _norm_mod(s_ref[...], g_ref[...], m[0:1], m[1:2])
    a = jnp.dot(h.astype(BF16), win_ref[...], preferred_element_type=F32)
    aq = (_rms(a[:, :MLA_Q_LORA]) * gqa_ref[...]).astype(BF16)
    ckv = (_rms(a[:, MLA_Q_LORA:MLA_Q_LORA + MLA_KV_LORA]) * gkva_ref[...]).astype(BF16)
    o = MLA_Q_LORA + MLA_KV_LORA
    cos = cos_ref[...]
    sin = sin_ref[...]
    kpe = a[:, o:o + MLA_SLOT] * cos + a[:, o + MLA_SLOT:o + 2 * MLA_SLOT] * sin
    cos_h = jnp.tile(cos, (1, MLA_HEADS))
    sin_h = jnp.tile(sin, (1, MLA_HEADS))
    q = jnp.dot(aq, wqb_ref[...], preferred_element_type=F32)
    qs = jnp.dot(aq, wqbs_ref[...], preferred_element_type=F32)
    q_ref[...] = ((q * cos_h + qs * sin_h) * scale).astype(BF16)
    kk = jnp.dot(ckv, wkvk_ref[...], preferred_element_type=F32)
    k_ref[...] = (kk + jnp.tile(kpe, (1, MLA_HEADS))).astype(BF16)
    v_ref[...] = lax.dot_general(wkvv_ref[...], ckv, (((1,), (1,)), ((), ())), preferred_element_type=F32).astype(BF16)


def _rope_perm(rot):
    sec = rot // 2
    half = sec // 2
    d = jnp.arange(rot)
    first = (d % sec) < half
    partner = jnp.where(first, d + half, d - half)
    sign = jnp.where(first, -1.0, 1.0).astype(F32)
    return partner, sign


def _rope_tables(seq, rot, tile):
    rows = seq // GRID_W
    row = jnp.repeat(jnp.arange(rows, dtype=F32), GRID_W)
    col = jnp.tile(jnp.arange(GRID_W, dtype=F32), rows)
    n_freq = rot // 4
    inv_freq = ROPE_THETA ** (-jnp.arange(n_freq, dtype=F32) / n_freq)
    ang = jnp.stack([row[:, None] * inv_freq, col[:, None] * inv_freq], axis=1)
    cos = jnp.cos(ang)
    sin = jnp.sin(ang)
    cos_full = jnp.concatenate([cos, cos], axis=-1).reshape(seq, rot)
    sin_full = jnp.concatenate([sin, sin], axis=-1).reshape(seq, rot)
    _, sign = _rope_perm(rot)
    sin_full = sin_full * sign
    ident_c = jnp.ones((tile, rot), F32)
    ident_s = jnp.zeros((tile, rot), F32)
    assert seq % tile == 0
    return jnp.concatenate([cos_full, ident_c], 0), jnp.concatenate([sin_full, ident_s], 0)


def _pre_mla(stream, mods4, g_mix, w_in, g_qa, w_qb, g_kva, w_kvb, geom, layer):
    d = stream.shape[1]
    hd = MLA_NOPE + MLA_ROPE
    partner, _ = _rope_perm(MLA_ROPE)
    o = MLA_Q_LORA + MLA_KV_LORA
    kpe_w = w_in[:, o:]
    z_lo = jnp.zeros((d, MLA_NOPE), F32)
    z_hi = jnp.zeros((d, MLA_SLOT - hd), F32)
    w_in_ext = jnp.concatenate([w_in[:, :o], z_lo, kpe_w, z_hi, z_lo, kpe_w[:, partner], z_hi], axis=1).astype(BF16)
    wq = w_qb.reshape(MLA_Q_LORA, MLA_HEADS, hd)
    zq = jnp.zeros((MLA_Q_LORA, MLA_HEADS, MLA_SLOT - hd), F32)
    wq_p = jnp.concatenate([wq, zq], axis=2).reshape(MLA_Q_LORA, MLA_HEADS * MLA_SLOT).astype(BF16)
    wq_s = jnp.concatenate([jnp.zeros_like(wq[:, :, :MLA_NOPE]), wq[:, :, MLA_NOPE:][:, :, partner], zq], axis=2)
    wq_s = wq_s.reshape(MLA_Q_LORA, MLA_HEADS * MLA_SLOT).astype(BF16)
    wkv = w_kvb.reshape(MLA_KV_LORA, MLA_HEADS, MLA_NOPE + MLA_V)
    zk = jnp.zeros((MLA_KV_LORA, MLA_HEADS, MLA_SLOT - MLA_NOPE), F32)
    wkv_k = jnp.concatenate([wkv[:, :, :MLA_NOPE], zk], axis=2).reshape(MLA_KV_LORA, MLA_HEADS * MLA_SLOT).astype(BF16)
    wkv_v = wkv[:, :, MLA_NOPE:].reshape(MLA_KV_LORA, MLA_HEADS * MLA_V).T.astype(BF16)
    cos, sin = _rope_tables(geom.seq, MLA_ROPE, PROJ_TILE)
    rows = cos.shape[0]
    cos_slot = jnp.concatenate([jnp.ones((rows, MLA_NOPE), F32), cos, jnp.ones((rows, MLA_SLOT - hd), F32)], axis=1)
    sin_slot = jnp.concatenate([jnp.zeros((rows, MLA_NOPE), F32), sin, jnp.zeros((rows, MLA_SLOT - hd), F32)], axis=1)
    n_tiles = geom.n_tok // PROJ_TILE
    wq_w = MLA_HEADS * MLA_SLOT
    wv_w = MLA_HEADS * MLA_V
    full = lambda a: pl.BlockSpec(a.shape, lambda j: (0,) * a.ndim)
    g_qa2, g_kva2 = g_qa.reshape(1, -1), g_kva.reshape(1, -1)
    return pl.pallas_call(
        functools.partial(_pre_mla_kernel, scale=float(hd) ** -0.5 * LOG2_E),
        grid=(n_tiles,),
        in_specs=[
            pl.BlockSpec((PROJ_TILE, d), lambda j: (j, 0)),
            _mod_spec(geom, layer, d, PROJ_TILE),
            pl.BlockSpec((1, d), lambda j: (0, 0)),
            full(w_in_ext), full(g_qa2), full(g_kva2), full(wq_p), full(wq_s), full(wkv_k), full(wkv_v),
            pl.BlockSpec((PROJ_TILE, MLA_SLOT), lambda j: (geom.pos_tile(j, PROJ_TILE), 0)),
            pl.BlockSpec((PROJ_TILE, MLA_SLOT), lambda j: (geom.pos_tile(j, PROJ_TILE), 0)),
        ],
        out_specs=[
            pl.BlockSpec((PROJ_TILE, wq_w), lambda j: (j, 0)),
            pl.BlockSpec((PROJ_TILE, wq_w), lambda j: (j, 0)),
            pl.BlockSpec((wv_w, PROJ_TILE), lambda j: (0, j)),
        ],
        out_shape=[
            jax.ShapeDtypeStruct((n_tiles * PROJ_TILE, wq_w), BF16),
            jax.ShapeDtypeStruct((n_tiles * PROJ_TILE, wq_w), BF16),
            jax.ShapeDtypeStruct((wv_w, n_tiles * PROJ_TILE), BF16),
        ],
        compiler_params=_cparams(("parallel",)),
        name="pre_mla",
    )(stream, mods4, g_mix, w_in_ext, g_qa2, g_kva2, wq_p, wq_s, wkv_k, wkv_v, cos_slot, sin_slot)


def _split_hi_lo(x):
    hi = x.astype(BF16)
    lo = (x - hi.astype(F32)).astype(BF16)
    return hi, lo


def _pre_gqa_kernel(s_ref, mod_ref, g_ref, w_ref, wvt_ref, bd_ref, cq_ref, sq_ref, ck_ref, sk_ref, q_ref, k_ref, v_ref, *,
                    scale):
    m = mod_ref[...]
    h = _norm_mod(s_ref[...], g_ref[...], m[0:1], m[1:2]).astype(BF16)
    a = jnp.dot(h, w_ref[...], preferred_element_type=F32)
    nq = GQA_HEADS * GQA_HEAD_DIM
    nkv = GQA_KV_HEADS * GQA_HEAD_DIM
    aq, ak = a[:, :nq], a[:, nq:nq + nkv]
    aqs, aks = a[:, nq + nkv:2 * nq + nkv], a[:, 2 * nq + nkv:]
    bd = bd_ref[...]

    def head_rs(x, width):
        hi, lo = _split_hi_lo(x * x)
        b = bd[:width, :width]
        ssq = jnp.dot(hi, b, preferred_element_type=F32) + jnp.dot(lo, b, preferred_element_type=F32)
        return lax.rsqrt(ssq * (1.0 / GQA_HEAD_DIM) + NORM_EPS)

    reps_q = nq // cq_ref.shape[1]
    reps_k = nkv // ck_ref.shape[1]
    q = head_rs(aq, nq) * (aq * jnp.tile(cq_ref[...], (1, reps_q)) + aqs * jnp.tile(sq_ref[...], (1, reps_q)))
    q_ref[...] = (q * scale).astype(BF16)
    k = head_rs(ak, nkv) * (ak * jnp.tile(ck_ref[...], (1, reps_k)) + aks * jnp.tile(sk_ref[...], (1, reps_k)))
    k = k.astype(BF16)
    for g in range(GQA_KV_HEADS):
        k_ref[g] = k[:, g * GQA_HEAD_DIM:(g + 1) * GQA_HEAD_DIM]
    v_ref[...] = lax.dot_general(wvt_ref[...], h, (((1,), (1,)), ((), ())), preferred_element_type=F32).astype(BF16)


def _pre_gqa(stream, mods4, g_mix, w_qkv, g_q, g_k, geom, layer):
    d = stream.shape[1]
    nq = GQA_HEADS * GQA_HEAD_DIM
    nkv = GQA_KV_HEADS * GQA_HEAD_DIM
    partner, _ = _rope_perm(GQA_HEAD_DIM)
    wq = w_qkv[:, :nq].reshape(d, GQA_HEADS, GQA_HEAD_DIM)
    wk = w_qkv[:, nq:nq + nkv].reshape(d, GQA_KV_HEADS, GQA_HEAD_DIM)
    w_ext = jnp.concatenate([w_qkv[:, :nq + nkv], wq[:, :, partner].reshape(d, nq), wk[:, :, partner].reshape(d, nkv)],
                            axis=1).astype(BF16)
    w_vt = w_qkv[:, nq + nkv:].T.astype(BF16)
    cos, sin = _rope_tables(geom.seq, GQA_HEAD_DIM, PROJ_TILE)
    per = LANES // GQA_HEAD_DIM
    cq = jnp.tile(cos * g_q[None, :], (1, per))
    sq = jnp.tile(sin * g_q[partner][None, :], (1, per))
    ck = jnp.tile(cos * g_k[None, :], (1, per))
    sk = jnp.tile(sin * g_k[partner][None, :], (1, per))
    hid = jnp.arange(nq) // GQA_HEAD_DIM
    bd = (hid[:, None] == hid[None, :]).astype(BF16)
    n_tiles = geom.n_tok // PROJ_TILE
    full = lambda a: pl.BlockSpec(a.shape, lambda j: (0,) * a.ndim)
    tab = pl.BlockSpec((PROJ_TILE, LANES), lambda j: (geom.pos_tile(j, PROJ_TILE), 0))
    return pl.pallas_call(
        functools.partial(_pre_gqa_kernel, scale=float(GQA_HEAD_DIM) ** -0.5 * LOG2_E),
        grid=(n_tiles,),
        in_specs=[
            pl.BlockSpec((PROJ_TILE, d), lambda j: (j, 0)),
            _mod_spec(geom, layer, d, PROJ_TILE),
            pl.BlockSpec((1, d), lambda j: (0, 0)),
            full(w_ext), full(w_vt), full(bd), tab, tab, tab, tab,
        ],
        out_specs=[
            pl.BlockSpec((PROJ_TILE, nq), lambda j: (j, 0)),
            pl.BlockSpec((GQA_KV_HEADS, PROJ_TILE, GQA_HEAD_DIM), lambda j: (0, j, 0)),
            pl.BlockSpec((nkv, PROJ_TILE), lambda j: (0, j)),
        ],
        out_shape=[
            jax.ShapeDtypeStruct((n_tiles * PROJ_TILE, nq), BF16),
            jax.ShapeDtypeStruct((GQA_KV_HEADS, n_tiles * PROJ_TILE, GQA_HEAD_DIM), BF16),
            jax.ShapeDtypeStruct((nkv, n_tiles * PROJ_TILE), BF16),
        ],
        compiler_params=_cparams(("parallel",)),
        name="pre_gqa",
    )(stream, mods4, g_mix, w_ext, w_vt, bd, cq, sq, ck, sk)


def _attn_kernel(q_ref, kl_ref, vl_ref, kc_ref, vc_ref, o_ref, s_ref, *, n_lat_steps, **head_args):
    t = pl.program_id(2)

    @pl.when(t < n_lat_steps)
    def _():
        _attn_tile(q_ref, kl_ref, vl_ref, kc_ref, vc_ref, o_ref, s_ref, has_lat=True, **head_args)

    @pl.when(t >= n_lat_steps)
    def _():
        _attn_tile(q_ref, kl_ref, vl_ref, kc_ref, vc_ref, o_ref, s_ref, has_lat=False, **head_args)


def _attn_tile(q_ref, kl_ref, vl_ref, kc_ref, vc_ref, o_ref, s_ref, *, n_heads, kv_heads, dk, dv, has_lat):
    nt = (((1,), (1,)), ((), ()))
    tq = q_ref.shape[0]
    chunks, row = [], 0
    for k_ref, v_ref in ([(kc_ref, vc_ref), (kl_ref, vl_ref)] if has_lat else [(kc_ref, vc_ref)]):
        n_keys = k_ref.shape[-2]
        for c0 in range(0, n_keys, KEY_CHUNK):
            kc = min(KEY_CHUNK, n_keys - c0)
            chunks.append((k_ref, v_ref, c0, row, kc))
            row += kc
    mx = [None] * n_heads
    den = [None] * n_heads
    acc = [None] * n_heads
    for phase in range(n_heads + 1):
        for k_ref, v_ref, c0, row, kc in chunks:
            if phase < n_heads:
                g = phase
                gk = g * kv_heads // n_heads
                q = q_ref[:, g * dk:(g + 1) * dk]
                k = k_ref[gk, c0:c0 + kc, :] if k_ref.ndim == 3 else k_ref[c0:c0 + kc, gk * dk:(gk + 1) * dk]
                s = lax.dot_general(k, q, nt, preferred_element_type=F32)
                s_ref[g % 2, row:row + kc, :] = s
                cmax = jnp.max(s, axis=0, keepdims=True)
                mx[g] = cmax if mx[g] is None else jnp.maximum(mx[g], cmax)
            if phase > 0:
                g = phase - 1
                gk = g * kv_heads // n_heads
                p = jnp.exp2(s_ref[g % 2, row:row + kc, :] - mx[g])
                psum = jnp.sum(p, axis=0, keepdims=True)
                pv = jnp.dot(v_ref[gk * dv:(gk + 1) * dv, c0:c0 + kc], p.astype(BF16), preferred_element_type=F32)
                den[g] = psum if den[g] is None else den[g] + psum
                acc[g] = pv if acc[g] is None else acc[g] + pv
    o_all = jnp.concatenate([acc[g] / den[g] for g in range(n_heads)], axis=0)
    o_ref[...] = jnp.transpose(o_all).astype(o_ref.dtype)


def _attention(q, k, vt, geom, *, groups, n_heads, kv_heads, dk, dv, k_head_major, context_queries, single_buffer_kv=False):
    wq, wk, wv = n_heads * dk, kv_heads * dk, kv_heads * dv
    wo = n_heads * dv
    batch = geom.batch
    nc = geom.n_ctx
    ctx_blk0 = geom.n_lat // nc
    assert geom.n_lat % nc == 0

    mode = dict(pipeline_mode=pl.Buffered(1)) if single_buffer_kv else {}

    def kspec(rows, tok_blk):
        if k_head_major:
            return pl.BlockSpec((kv_heads, rows, dk), lambda b, h, *_: (h, tok_blk(b), 0), **mode)
        return pl.BlockSpec((rows, wk), lambda b, h, *_: (tok_blk(b), h), **mode)

    def vspec(rows, tok_blk):
        return pl.BlockSpec((wv, rows), lambda b, h, *_: (h, tok_blk(b)), **mode)

    tq = TT
    qt = geom.seq // tq
    assert nc == tq or not context_queries
    steps = qt + 1 if context_queries else qt
    qrow = lambda b, t: jnp.where(t < qt, b * qt + t, ctx_blk0 + b)
    return pl.pallas_call(
        functools.partial(_attn_kernel, n_lat_steps=qt, n_heads=n_heads, kv_heads=kv_heads, dk=dk, dv=dv),
        grid=(batch, groups, steps),
        in_specs=[
            pl.BlockSpec((tq, wq), lambda b, h, t: (qrow(b, t), h)),
            kspec(geom.seq, lambda b: b),
            vspec(geom.seq, lambda b: b),
            kspec(nc, lambda b: ctx_blk0 + b),
            vspec(nc, lambda b: ctx_blk0 + b),
        ],
        out_specs=pl.BlockSpec((tq, wo), lambda b, h, t: (qrow(b, t), h)),
        out_shape=jax.ShapeDtypeStruct((geom.n_tok if context_queries else geom.n_lat, groups * wo), BF16),
        scratch_shapes=[pltpu.VMEM((2, geom.seq + nc, tq), F32)],
        compiler_params=_cparams(("parallel", "parallel", "arbitrary")),
        name="attention",
    )(q, k, vt, k, vt)


def _post_kernel(o_ref, wo_ref, s_ref, mod_ref, g_ref, wrh_ref, wrl_ref, br_ref, s_out, h2_out, idx_out, w_out, cnt_out, *, n_experts):
    m = mod_ref[...]
    s = s_ref[...] + m[2:3] * jnp.dot(o_ref[...], wo_ref[...], preferred_element_type=F32)
    s_out[...] = s
    h2 = _norm_mod(s, g_ref[...], m[3:4], m[4:5])
    h2_out[...] = h2.astype(BF16)
    h2_hi, h2_lo = _split_hi_lo(h2)
    logits = (jnp.dot(h2_hi, wrh_ref[...], preferred_element_type=F32) + jnp.dot(h2_lo, wrh_ref[...], preferred_element_type=F32)
              + jnp.dot(h2_hi, wrl_ref[...], preferred_element_type=F32) + br_ref[...])
    lane = lax.broadcasted_iota(I32, logits.shape, 1).astype(F32)
    work = jnp.where(lane < n_experts, logits, -jnp.inf)
    vals, hits = [], []
    for k in range(TOP_K):
        mx = jnp.max(work, axis=-1, keepdims=True)
        sel = jnp.min(jnp.where(work == mx, lane, float(LANES)), axis=-1, keepdims=True)
        hit = lane == sel
        vals.append(mx)
        hits.append(hit)
        work = jnp.where(hit, -jnp.inf, work)
    es = [jnp.exp(v - vals[0]) for v in vals]
    den = es[0] + es[1] + es[2] + es[3]
    w_slab = jnp.zeros(logits.shape, F32)
    for k in range(TOP_K):
        w_slab = jnp.where(lane == k, es[k] / den, w_slab)
    w_out[...] = w_slab
    onehot = jnp.where(hits[0] | hits[1] | hits[2] | hits[3], 1.0, 0.0).astype(BF16)
    tr = lax.broadcasted_iota(I32, (TT, TT), 0)
    tc = lax.broadcasted_iota(I32, (TT, TT), 1)
    earlier = jnp.where(tc < tr, 1.0, 0.0).astype(BF16)
    er = lax.broadcasted_iota(I32, (LANES, LANES), 0)
    ec = lax.broadcasted_iota(I32, (LANES, LANES), 1)
    lower = jnp.where(er < ec, 1.0, 0.0).astype(BF16)
    bases = []
    for t in range(onehot.shape[0] // TT):
        oh = onehot[t * TT:(t + 1) * TT]
        rank = jnp.dot(earlier, oh, preferred_element_type=F32)
        cnt = jnp.sum(oh.astype(F32), axis=0, keepdims=True)
        cp = jnp.floor((cnt + (SUBLANES - 1)) * (1.0 / SUBLANES)) * SUBLANES
        seg = jnp.dot(jnp.broadcast_to(cp, (SUBLANES, LANES)).astype(BF16), lower, preferred_element_type=F32)[0:1]
        bases.append(seg + rank)
        cnt_out[t] = jnp.broadcast_to(cnt, (SUBLANES, LANES)).astype(I32)
    base = jnp.concatenate(bases, axis=0)
    lp_slab = jnp.zeros(logits.shape, F32)
    for k in range(TOP_K):
        lp_slab = jnp.where(lane == k, jnp.sum(jnp.where(hits[k], base, 0.0), axis=-1, keepdims=True), lp_slab)
    idx_out[...] = lp_slab.astype(I32)


def _post(o, w_o, stream, mods4, g_ffn, w_router, b_router, geom, layer, n_tiles):
    d = stream.shape[1]
    wo_w = o.shape[1]
    n_experts = w_router.shape[1]
    wr_hi, wr_lo = _split_hi_lo(jnp.concatenate([w_router, jnp.zeros((d, LANES - n_experts), F32)], axis=1))
    br = jnp.concatenate([b_router, jnp.zeros((LANES - n_experts,), F32)]).reshape(1, LANES)
    n = n_tiles * TT
    return pl.pallas_call(
        functools.partial(_post_kernel, n_experts=n_experts),
        grid=(n // PROJ_TILE,),
        in_specs=[
            pl.BlockSpec((PROJ_TILE, wo_w), lambda j: (j, 0)),
            pl.BlockSpec((wo_w, d), lambda j: (0, 0)),
            pl.BlockSpec((PROJ_TILE, d), lambda j: (j, 0)),
            _mod_spec(geom, layer, d, PROJ_TILE),
            pl.BlockSpec((1, d), lambda j: (0, 0)),
            pl.BlockSpec((d, LANES), lambda j: (0, 0)),
            pl.BlockSpec((d, LANES), lambda j: (0, 0)),
            pl.BlockSpec((1, LANES), lambda j: (0, 0)),
        ],
        out_specs=[
            pl.BlockSpec((PROJ_TILE, d), lambda j: (j, 0)),
            pl.BlockSpec((PROJ_TILE, d), lambda j: (j, 0)),
            pl.BlockSpec((PROJ_TILE, LANES), lambda j: (j, 0)),
            pl.BlockSpec((PROJ_TILE, LANES), lambda j: (j, 0)),
            pl.BlockSpec((PROJ_TILE // TT, SUBLANES, LANES), lambda j: (j, 0, 0)),
        ],
        out_shape=[
            jax.ShapeDtypeStruct((n, d), F32),
            jax.ShapeDtypeStruct((n, d), BF16),
            jax.ShapeDtypeStruct((n, LANES), I32),
            jax.ShapeDtypeStruct((n, LANES), F32),
            jax.ShapeDtypeStruct((n_tiles, SUBLANES, LANES), I32),
        ],
        compiler_params=_cparams(("parallel",)),
        name="post",
    )(o, w_o, stream, mods4, g_ffn, wr_hi, wr_lo, br)


def _local_rows(n_experts):
    return _round_up(TT * TOP_K + n_experts * (SUBLANES - 1), LANES)


def _n_blocks(n_tok, n_tiles, n_experts):
    rows = n_tok * TOP_K + n_experts * n_tiles * (SUBLANES - 1) + n_experts * (EXPERT_BLOCK - 1)
    return pl.cdiv(rows, EXPERT_BLOCK)


def _route_meta(cnt, lpos, n_experts):
    n_tiles = cnt.shape[0]
    n_blocks = _n_blocks(n_tiles * TT, n_tiles, n_experts)
    cp = _round_up(cnt, SUBLANES)
    seg = jnp.cumsum(cp, axis=1) - cp
    run_rows = cp.sum(axis=0)
    reg = _round_up(run_rows, EXPERT_BLOCK)
    reg_end = jnp.cumsum(reg)
    reg_start = reg_end - reg
    off = reg_start[None, :] + jnp.cumsum(cp, axis=0) - cp
    n_used = (reg_end[-1] // EXPERT_BLOCK).astype(I32)
    bstart = jnp.arange(n_blocks, dtype=I32) * EXPERT_BLOCK
    last_start = jnp.maximum(n_used - 1, 0) * EXPERT_BLOCK
    be = (jnp.minimum(bstart, last_start)[:, None] >= reg_end[None, :]).sum(axis=1).astype(I32)
    be = jnp.minimum(be, n_experts - 1)
    per_big = cp // BIG_CHUNK
    per_small = (cp % BIG_CHUNK) // SUBLANES
    tail = per_big * BIG_CHUNK
    return dict(
        big_list=_copy_list(per_big, seg, off, BIG_CHUNK, _max_big(n_experts)),
        small_list=_copy_list(per_small, seg + tail, off + tail, SUBLANES, _max_small(n_experts)),
        n_big=per_big.sum(axis=1).astype(I32), n_small=per_small.sum(axis=1).astype(I32),
        gap_start=(reg_start + run_rows).astype(I32), gap=(reg - run_rows).astype(I32),
        lpos=lpos.astype(I32), block_expert=be, n_used=n_used.reshape(1), n_blocks=n_blocks,
    )


def _max_big(n_experts):
    return _local_rows(n_experts) // BIG_CHUNK


def _max_small(n_experts):
    return n_experts * (BIG_CHUNK // SUBLANES - 1)


def _copy_list(per_run, local_row, sorted_row, chunk, max_copies):
    ends = jnp.cumsum(per_run, axis=1)[:, None, :]
    first = ends - per_run[:, None, :]
    k = jnp.arange(max_copies, dtype=I32)[None, :, None]
    mine = (first <= k) & (k < ends)
    pick = lambda v: jnp.sum(jnp.where(mine, v[:, None, :], 0), axis=2)
    j = k[:, :, 0] - pick(first[:, 0, :])
    src = pick(local_row) + chunk * j
    dst = pick(sorted_row) + chunk * j
    return (dst * LIST_RADIX + src).reshape(-1).astype(I32)


def _run_copies(tile, nbig_ref, nsmall_ref, big_ref, small_ref, max_big, max_small, make_big, make_small):
    def start(make, packed):
        local_row = pl.multiple_of(packed & (LIST_RADIX - 1), SUBLANES)
        sorted_row = pl.multiple_of(lax.shift_right_logical(packed, LIST_SHIFT), SUBLANES)
        make(local_row, sorted_row).start()

    def big(k, c):
        start(make_big, big_ref[tile * max_big + k])
        return c

    def small(k, c):
        start(make_small, small_ref[tile * max_small + k])
        return c

    lax.fori_loop(0, nbig_ref[tile], big, 0)
    lax.fori_loop(0, nsmall_ref[tile], small, 0)


def _wait_copies(tile, nbig_ref, nsmall_ref, make_big, make_small):
    def wb(j, c):
        make_big(0, 0).wait()
        return c

    def ws(j, c):
        make_small(0, 0).wait()
        return c

    lax.fori_loop(0, nbig_ref[tile], wb, 0)
    lax.fori_loop(0, nsmall_ref[tile], ws, 0)


def _dispatch_kernel(big_ref, small_ref, nbig_ref, nsmall_ref, gs_ref, gap_ref, nu_ref,
                     lpt_ref, h2_ref, xs_ref, buf_ref, zero_ref, sem, zsem, *, n_experts, local_rows, n_blocks):
    i = pl.program_id(0)
    n = pl.num_programs(0)
    slot = i % 2
    tps = MOE_TILES_PER_STEP

    def copies(sl, sub):
        def big(src, dst):
            return pltpu.make_async_copy(buf_ref.at[sl, sub, pl.ds(src, BIG_CHUNK), :], xs_ref.at[pl.ds(dst, BIG_CHUNK), :],
                                         sem.at[sl])

        def small(src, dst):
            return pltpu.make_async_copy(buf_ref.at[sl, sub, pl.ds(src, SUBLANES), :], xs_ref.at[pl.ds(dst, SUBLANES), :],
                                         sem.at[sl])

        return big, small

    def wait_step(step, sl):
        for sub in range(tps):
            _wait_copies(step * tps + sub, nbig_ref, nsmall_ref, *copies(sl, sub))

    @pl.when(i >= 2)
    def _():
        wait_step(i - 2, slot)

    for sub in range(tps):
        lpt = lpt_ref[sub]
        rows = lax.broadcasted_iota(I32, (local_rows, TT), 0)
        hit = rows == lpt[0:1, :]
        for k in range(1, TOP_K):
            hit = hit | (rows == lpt[k:k + 1, :])
        p = jnp.where(hit, 1.0, 0.0).astype(BF16)
        buf_ref[slot, sub] = _pack_bf16_pairs(jnp.dot(p, h2_ref[sub * TT:(sub + 1) * TT, :], preferred_element_type=F32))
        _run_copies(i * tps + sub, nbig_ref, nsmall_ref, big_ref, small_ref, _max_big(n_experts), _max_small(n_experts),
                    *copies(slot, sub))

    @pl.when(i == n - 1)
    def _():
        zero_ref[...] = jnp.zeros_like(zero_ref)
        sizes = []
        size = EXPERT_BLOCK // 2
        while size >= SUBLANES:
            sizes.append(size)
            size //= 2

        def zcopy(dst, size):
            return pltpu.make_async_copy(zero_ref.at[pl.ds(0, size), :], xs_ref.at[pl.ds(dst, size), :], zsem)

        def per_expert(e, carry):
            gap = gap_ref[e]
            pos = gs_ref[e]
            for size in sizes:
                take = (gap & size) != 0

                @pl.when(take)
                def _():
                    zcopy(pl.multiple_of(pos, SUBLANES), size).start()

                pos = pos + jnp.where(take, size, 0)
            return carry

        lax.fori_loop(0, n_experts, per_expert, 0)

        zrows = zero_ref.shape[0]
        per_block = EXPERT_BLOCK // zrows
        tail_copies = (n_blocks - nu_ref[0]) * per_block

        def tail(t, carry):
            zcopy(pl.multiple_of(nu_ref[0] * EXPERT_BLOCK + t * zrows, SUBLANES), zrows).start()
            return carry

        lax.fori_loop(0, tail_copies, tail, 0)

        def per_expert_wait(e, carry):
            gap = gap_ref[e]
            for size in sizes:
                @pl.when((gap & size) != 0)
                def _():
                    zcopy(0, size).wait()
            return carry

        lax.fori_loop(0, n_experts, per_expert_wait, 0)

        def tail_wait(t, carry):
            zcopy(0, zrows).wait()
            return carry

        lax.fori_loop(0, tail_copies, tail_wait, 0)

        @pl.when(i >= 1)
        def _():
            wait_step(i - 1, 1 - slot)

        wait_step(i, slot)


def _dispatch(h2, meta, n_tiles, n_experts):
    n, d = h2.shape
    local_rows = _local_rows(n_experts)
    n_rows = meta["n_blocks"] * EXPERT_BLOCK
    lpt = meta["lpos"].reshape(n_tiles, TT, TOP_K).transpose(0, 2, 1)
    tps = MOE_TILES_PER_STEP
    assert n_tiles % tps == 0
    grid_spec = pltpu.PrefetchScalarGridSpec(
        num_scalar_prefetch=7,
        grid=(n_tiles // tps,),
        in_specs=[
            pl.BlockSpec((tps, TOP_K, TT), lambda j, *_: (j, 0, 0)),
            pl.BlockSpec((tps * TT, d), lambda j, *_: (j, 0)),
        ],
        out_specs=pl.BlockSpec(memory_space=pl.ANY),
        scratch_shapes=[
            pltpu.VMEM((2, tps, local_rows, d // 2), U32),
            pltpu.VMEM((EXPERT_BLOCK // 2, d // 2), U32),
            pltpu.SemaphoreType.DMA((2,)),
            pltpu.SemaphoreType.DMA(()),
        ],
    )
    return pl.pallas_call(
        functools.partial(_dispatch_kernel, n_experts=n_experts, local_rows=local_rows, n_blocks=meta["n_blocks"]),
        grid_spec=grid_spec,
        out_shape=jax.ShapeDtypeStruct((n_rows, d // 2), U32),
        compiler_params=_cparams(("arbitrary",)),
        name="moe_dispatch",
    )(meta["big_list"], meta["small_list"], meta["n_big"], meta["n_small"], meta["gap_start"], meta["gap"], meta["n_used"],
      lpt, h2)


def _ffn_kernel(be_ref, nu_ref, x_ref, wgu_ref, wd_ref, sel_ref, bg_ref, bu_ref, bd_ref, y_ref, wg_s, wu_s, wd_s):
    b = pl.program_id(0)
    used = b < nu_ref[0]

    @pl.when(used & ((b == 0) | (be_ref[b] != be_ref[jnp.maximum(b - 1, 0)])))
    def _():
        sel = sel_ref[...]
        pair = 2 * LANES
        for c in range(wgu_ref.shape[1] // pair):
            chunk = wgu_ref[:, c * pair:(c + 1) * pair].astype(BF16)
            de = jnp.dot(chunk, sel, preferred_element_type=F32)
            wg_s[:, c * LANES:(c + 1) * LANES] = de[:, :LANES].astype(BF16)
            wu_s[:, c * LANES:(c + 1) * LANES] = de[:, LANES:].astype(BF16)
        wd_s[...] = wd_ref[...].astype(BF16)

    @pl.when(used)
    def _():
        x = _unpack_bf16_pairs(x_ref[...])
        gate = jnp.dot(x, wg_s[...], preferred_element_type=F32) + bg_ref[...]
        up = jnp.dot(x, wu_s[...], preferred_element_type=F32) + bu_ref[...]
        gate = jnp.minimum(gate, SWIGLU_LIMIT)
        up = jnp.clip(up, -SWIGLU_LIMIT, SWIGLU_LIMIT)
        glu = gate / (1.0 + jnp.exp(-SWIGLU_ALPHA * gate))
        act = ((up + 1.0) * glu).astype(BF16)
        y = jnp.dot(act, wd_s[...], preferred_element_type=F32) + bd_ref[...]
        y_ref[...] = _pack_bf16_pairs(y.astype(BF16).astype(F32))

    @pl.when(jnp.logical_not(used))
    def _():
        y_ref[...] = jnp.zeros_like(y_ref)


def _ffn(xs, meta, layer, w_gu, w_down, b_gate, b_up, b_down):
    n_rows = xs.shape[0]
    n_blocks = meta["n_blocks"]
    f, d = w_down.shape[2:]
    r = jnp.arange(2 * LANES)
    src = jnp.where(r < LANES, 2 * r, 2 * (r - LANES) + 1)
    sel = (r[:, None] == src[None, :]).astype(BF16)
    xmap = lambda b, be, nu: (jnp.minimum(b, jnp.maximum(nu[0] - 1, 0)), 0)
    wmap = lambda b, be, nu: (layer, be[b], 0, 0)
    grid_spec = pltpu.PrefetchScalarGridSpec(
        num_scalar_prefetch=2,
        grid=(n_blocks,),
        in_specs=[
            pl.BlockSpec((EXPERT_BLOCK, d // 2), xmap),
            pl.BlockSpec((None, None, d, 2 * f), wmap),
            pl.BlockSpec((None, None, f, d), wmap),
            pl.BlockSpec((2 * LANES, 2 * LANES), lambda b, be, nu: (0, 0)),
            pl.BlockSpec((None, None, 1, f), wmap),
            pl.BlockSpec((None, None, 1, f), wmap),
            pl.BlockSpec((None, None, 1, d), wmap),
        ],
        out_specs=pl.BlockSpec((EXPERT_BLOCK, d // 2), lambda b, be, nu: (b, 0)),
        scratch_shapes=[pltpu.VMEM((d, f), BF16), pltpu.VMEM((d, f), BF16), pltpu.VMEM((f, d), BF16)],
    )
    return pl.pallas_call(
        _ffn_kernel,
        grid_spec=grid_spec,
        out_shape=jax.ShapeDtypeStruct((n_rows, d // 2), U32),
        compiler_params=_cparams(("arbitrary",)),
        name="moe_ffn",
    )(meta["block_expert"], meta["n_used"], xs, w_gu, w_down, sel, b_gate, b_up, b_down)


def _combine_kernel(big_ref, small_ref, nbig_ref, nsmall_ref,
                    lp_ref, w_ref, s_ref, mod_ref, gf_ref, ys_ref, o_ref, buf_ref, sem, *, n_experts, local_rows, final):
    i = pl.program_id(0)
    n = pl.num_programs(0)
    slot = i % 2
    tps = MOE_TILES_PER_STEP

    def copies(sl, sub):
        def big(dst, src):
            return pltpu.make_async_copy(ys_ref.at[pl.ds(src, BIG_CHUNK), :], buf_ref.at[sl, sub, pl.ds(dst, BIG_CHUNK), :],
                                         sem.at[sl])

        def small(dst, src):
            return pltpu.make_async_copy(ys_ref.at[pl.ds(src, SUBLANES), :], buf_ref.at[sl, sub, pl.ds(dst, SUBLANES), :],
                                         sem.at[sl])

        return big, small

    def start_step(step, sl):
        for sub in range(tps):
            _run_copies(step * tps + sub, nbig_ref, nsmall_ref, big_ref, small_ref, _max_big(n_experts),
                        _max_small(n_experts), *copies(sl, sub))

    def wait_step(step, sl):
        for sub in range(tps):
            _wait_copies(step * tps + sub, nbig_ref, nsmall_ref, *copies(sl, sub))

    @pl.when(i == 0)
    def _():
        buf_ref[...] = jnp.zeros_like(buf_ref)
        start_step(0, 0)

    @pl.when(i + 1 < n)
    def _():
        start_step(i + 1, 1 - slot)

    wait_step(i, slot)

    m = mod_ref[...]
    for sub in range(tps):
        rows = slice(sub * TT, (sub + 1) * TT)
        lp = lp_ref[rows, :]
        w = w_ref[rows, :]
        lanes = lax.broadcasted_iota(I32, (TT, local_rows), 1)
        pw = jnp.zeros((TT, local_rows), F32)
        for k in range(TOP_K):
            pw = pw + jnp.where(lanes == lp[:, k:k + 1], w[:, k:k + 1], 0.0)
        y = _unpack_bf16_pairs(buf_ref[slot, sub])
        f = jnp.dot(pw.astype(BF16), y, preferred_element_type=F32)
        s = s_ref[rows, :] + m[5:6] * f
        if final:
            s = _rms(s) * gf_ref[...]
        o_ref[rows, :] = s


def _combine(ys, meta, top_w, stream, mods4, g_final, geom, layer, n_tiles, n_experts, final):
    d = stream.shape[1]
    local_rows = _local_rows(n_experts)
    tps = MOE_TILES_PER_STEP
    assert n_tiles % tps == 0
    grid_spec = pltpu.PrefetchScalarGridSpec(
        num_scalar_prefetch=4,
        grid=(n_tiles // tps,),
        in_specs=[
            pl.BlockSpec((tps * TT, TOP_K), lambda j, *_: (j, 0)),
            pl.BlockSpec((tps * TT, TOP_K), lambda j, *_: (j, 0)),
            pl.BlockSpec((tps * TT, d), lambda j, *_: (j, 0)),
            pl.BlockSpec((None, None, N_MOD, d), lambda j, *_: (layer, geom.group(j, tps * TT), 0, 0)),
            pl.BlockSpec((1, d), lambda j, *_: (0, 0)),
            pl.BlockSpec(memory_space=pl.ANY),
        ],
        out_specs=pl.BlockSpec((tps * TT, d), lambda j, *_: (j, 0)),
        scratch_shapes=[
            pltpu.VMEM((2, tps, local_rows, d // 2), U32),
            pltpu.SemaphoreType.DMA((2,)),
        ],
    )
    return pl.pallas_call(
        functools.partial(_combine_kernel, n_experts=n_experts, local_rows=local_rows, final=final),
        grid_spec=grid_spec,
        out_shape=jax.ShapeDtypeStruct((n_tiles * TT, d), F32),
        compiler_params=_cparams(("arbitrary",)),
        name="moe_combine",
    )(meta["big_list"], meta["small_list"], meta["n_big"], meta["n_small"],
      meta["lpos"], top_w, stream, mods4, g_final, ys)


def kernel(x, c, ctx, c_ctx, w_mod, b_mod, g_mix, g_ffn, g_final, f_w_in, f_w_out, mla_w_in, mla_g_qa, mla_w_qb,
           mla_g_kva, mla_w_kvb, mla_w_o, gqa_w_qkv, gqa_g_q, gqa_g_k, gqa_w_o, moe_w_router, moe_b_router,
           moe_w_gu, moe_b_gu, moe_w_down, moe_b_down):
    batch, seq, d = x.shape
    n_ctx = ctx.shape[1]
    depth = w_mod.shape[0]
    n_experts = moe_w_router.shape[2]
    geom = _Geom(batch, seq, n_ctx)

    n_groups = _round_up(batch + 1, SUBLANES)
    cc = jnp.concatenate([c, c_ctx[None, :], jnp.zeros((n_groups - batch - 1, d), F32)], axis=0)
    mods4 = _mods(cc, w_mod, b_mod).reshape(depth, n_groups, N_MOD, d)

    stream = jnp.concatenate([x.reshape(batch * seq, d), ctx.reshape(batch * n_ctx, d)], axis=0)

    gd = d // FOURIER_GROUPS
    gcos, gsin = _dft_tables(gd)
    eye = jnp.eye(FOURIER_GROUPS, dtype=F32)
    bd_cs = jnp.concatenate([jnp.kron(eye, gcos), jnp.kron(eye, gsin)], axis=1)

    f = moe_w_down.shape[2]
    b_gate = moe_b_gu[:, :, 0::2].reshape(depth, n_experts, 1, f)
    b_up = moe_b_gu[:, :, 1::2].reshape(depth, n_experts, 1, f)
    b_down = moe_b_down.reshape(depth, n_experts, 1, d)

    for i in range(depth):
        kind, j = i % N_MIXERS, i // N_MIXERS
        last = i == depth - 1
        ctx_used = (kind != 0) or (not last)
        n_tiles = geom.n_tiles if ctx_used else geom.n_lat_tiles
        gm = g_mix[i].reshape(1, d)
        gf = g_ffn[i].reshape(1, d)

        if kind == 0:
            wcs = _fold(f_w_in[j], bd_cs, BF16)
            u = _pre_fourier(stream, mods4, gm, wcs, geom, i, n_tiles * TT)
            o = _dft(u, 0, seq, batch, d)
            if ctx_used:
                o = jnp.concatenate([o, _dft(u, geom.n_lat, n_ctx, batch, d)], axis=0)
            w_o = f_w_out[j].astype(BF16)
        elif kind == 1:
            q, k, v = _pre_mla(stream, mods4, gm, mla_w_in[j], mla_g_qa[j], mla_w_qb[j], mla_g_kva[j], mla_w_kvb[j], geom, i)
            o = _attention(q, k, v, geom, groups=1, n_heads=MLA_HEADS, kv_heads=MLA_HEADS, dk=MLA_SLOT, dv=MLA_V,
                           k_head_major=False, context_queries=not last, single_buffer_kv=True)
            w_o = mla_w_o[j].astype(BF16)
        else:
            q, k, v = _pre_gqa(stream, mods4, gm, gqa_w_qkv[j], gqa_g_q[j], gqa_g_k[j], geom, i)
            o = _attention(q, k, v, geom, groups=1, n_heads=GQA_HEADS, kv_heads=GQA_KV_HEADS, dk=GQA_HEAD_DIM,
                           dv=GQA_HEAD_DIM, k_head_major=True, context_queries=not last)
            w_o = gqa_w_o[j].astype(BF16)

        n_moe_tiles = geom.n_lat_tiles if last else geom.n_tiles
        stream, h2, lp_slab, w_slab, cnt_slab = _post(o, w_o, stream, mods4, gf, moe_w_router[i], moe_b_router[i], geom, i,
                                                      n_moe_tiles)
        meta = _route_meta(cnt_slab[:, 0, :n_experts], lp_slab[:, :TOP_K], n_experts)
        xs = _dispatch(h2, meta, n_moe_tiles, n_experts)
        ys = _ffn(xs, meta, i, moe_w_gu, moe_w_down, b_gate, b_up, b_down)
        stream = _combine(ys, meta, w_slab[:, :TOP_K], stream, mods4, g_final.reshape(1, d), geom, i, n_moe_tiles, n_experts,
                          last)

    return stream[:batch * seq].reshape(batch, seq, d)
```

```python
import functools

import jax
import jax.numpy as jnp
from jax import lax
from jax.experimental import pallas as pl
from jax.experimental.pallas import tpu as pltpu

F32 = jnp.float32
BF16 = jnp.bfloat16
I32 = jnp.int32
U32 = jnp.uint32
HI = lax.Precision.HIGHEST

GRID_W = 64
N_MIXERS = 3
NORM_EPS = 1e-6
ROPE_THETA = 10000.0
FOURIER_GROUPS = 8
MLA_HEADS = 16
MLA_Q_LORA = 384
MLA_KV_LORA = 256
MLA_NOPE = 64
MLA_ROPE = 32
MLA_V = 64
GQA_HEADS = 16
GQA_KV_HEADS = 4
GQA_HEAD_DIM = 64
TOP_K = 4
SWIGLU_LIMIT = 7.0
SWIGLU_ALPHA = 1.702
N_MOD = 6

LANES = 128
SUBLANES = 8
TT = 256
PROJ_TILE = 512
KEY_CHUNK = 2048
LOG2_E = 1.4426950408889634
EXPERT_BLOCK = 512
MOE_TILES_PER_STEP = 2
BIG_CHUNK = 32
LIST_SHIFT = 11
LIST_RADIX = 1 << LIST_SHIFT
MLA_SLOT = 128
VMEM_LIMIT = 56 * 1024 * 1024


def _cparams(sem):
    return pltpu.CompilerParams(dimension_semantics=sem, vmem_limit_bytes=VMEM_LIMIT)


def _round_up(x, m):
    return (x + m - 1) // m * m


def _norm_mod(x, g, shift, scale):
    y = x * lax.rsqrt(jnp.mean(x * x, axis=-1, keepdims=True) + NORM_EPS) * g
    return y * (1.0 + scale) + shift


def _rms(x):
    return x * lax.rsqrt(jnp.mean(x * x, axis=-1, keepdims=True) + NORM_EPS)


def _pack_bf16_pairs(x):
    bits = lax.bitcast_convert_type(x, U32)
    half = x.shape[1] // 2
    return (bits[:, :half] & jnp.uint32(0xFFFF0000)) | lax.shift_right_logical(bits[:, half:], jnp.uint32(16))


def _unpack_bf16_pairs(u):
    hi = lax.bitcast_convert_type(u & jnp.uint32(0xFFFF0000), F32)
    lo = lax.bitcast_convert_type(lax.shift_left(u, jnp.uint32(16)), F32)
    return jnp.concatenate([hi, lo], axis=1).astype(BF16)


def _mod_kernel(cc_ref, w_ref, b_ref, o_ref):
    cc = cc_ref[...]
    a = cc / (1.0 + jnp.exp(-cc))
    o_ref[...] = jnp.dot(a, w_ref[...], precision=HI, preferred_element_type=F32) + b_ref[...]


def _mods(cc, w_mod, b_mod):
    depth, d, n6 = w_mod.shape
    g = cc.shape[0]
    tn = 1536 if n6 % 1536 == 0 else n6
    return pl.pallas_call(
        _mod_kernel,
        grid=(depth, n6 // tn),
        in_specs=[
            pl.BlockSpec((g, d), lambda i, n: (0, 0)),
            pl.BlockSpec((None, d, tn), lambda i, n: (i, 0, n)),
            pl.BlockSpec((None, 1, tn), lambda i, n: (i, 0, n)),
        ],
        out_specs=pl.BlockSpec((None, g, tn), lambda i, n: (i, 0, n)),
        out_shape=jax.ShapeDtypeStruct((depth, g, n6), F32),
        compiler_params=_cparams(("parallel", "parallel")),
        name="mods",
    )(cc, w_mod, b_mod.reshape(depth, 1, n6))


def _fold_kernel(a_ref, b_ref, o_ref):
    o_ref[...] = jnp.dot(a_ref[...], b_ref[...], precision=HI, preferred_element_type=F32).astype(o_ref.dtype)


def _fold(a, b, out_dtype):
    m, k = a.shape
    n = b.shape[1]
    tn = 512
    return pl.pallas_call(
        _fold_kernel,
        grid=(n // tn,),
        in_specs=[pl.BlockSpec((m, k), lambda j: (0, 0)), pl.BlockSpec((k, tn), lambda j: (0, j))],
        out_specs=pl.BlockSpec((m, tn), lambda j: (0, j)),
        out_shape=jax.ShapeDtypeStruct((m, n), out_dtype),
        compiler_params=_cparams(("parallel",)),
        name="fold",
    )(a, b)


class _Geom:
    def __init__(self, batch, seq, n_ctx):
        self.batch, self.seq, self.n_ctx = batch, seq, n_ctx
        assert seq % TT == 0 and n_ctx % TT == 0 and seq % GRID_W == 0
        self.lat_per_b = seq // TT
        self.ctx_per_b = n_ctx // TT
        self.n_lat_tiles = batch * self.lat_per_b
        self.n_ctx_tiles = batch * self.ctx_per_b
        self.n_tiles = self.n_lat_tiles + self.n_ctx_tiles
        self.n_lat = batch * seq
        self.n_tok = self.n_lat + batch * n_ctx

    def group(self, j, tile=TT):
        return jnp.where(j < self.n_lat // tile, j // (self.seq // tile), self.batch)

    def pos_tile(self, j, tile=TT):
        return jnp.where(j < self.n_lat // tile, j % (self.seq // tile), self.seq // tile)


def _mod_spec(geom, layer, d, tile=TT):
    return pl.BlockSpec((None, None, N_MOD, d), lambda j: (layer, geom.group(j, tile), 0, 0))


FOURIER_TILE = 1024


def _pre_fourier_kernel(s_ref, mod_ref, g_ref, perm_ref, w_ref, u_ref):
    m = mod_ref[...]
    h = _norm_mod(s_ref[...], g_ref[...], m[0:1], m[1:2]).astype(BF16)
    h = jnp.dot(perm_ref[...], h, preferred_element_type=F32).astype(BF16)
    u_ref[...] = jnp.dot(h, w_ref[...], preferred_element_type=F32).astype(BF16)


def _pre_fourier(stream, mods4, g_mix, wcs, geom, layer, n_tok):
    d = stream.shape[1]
    assert n_tok % FOURIER_TILE == 0 and geom.seq % FOURIER_TILE == 0 and FOURIER_TILE % geom.n_ctx == 0
    lat_tiles = geom.n_lat // FOURIER_TILE
    per_b = geom.seq // FOURIER_TILE
    group = lambda j: jnp.where(j < lat_tiles, j // per_b, geom.batch)
    r = jnp.arange(FOURIER_TILE)
    half = FOURIER_TILE // 2
    src = jnp.where(r < half, 2 * r, 2 * (r - half) + 1)
    perm = (src[:, None] == r[None, :]).astype(BF16)
    return pl.pallas_call(
        _pre_fourier_kernel,
        grid=(n_tok // FOURIER_TILE,),
        in_specs=[
            pl.BlockSpec((FOURIER_TILE, d), lambda j: (j, 0)),
            pl.BlockSpec((None, None, N_MOD, d), lambda j: (layer, group(j), 0, 0)),
            pl.BlockSpec((1, d), lambda j: (0, 0)),
            pl.BlockSpec((FOURIER_TILE, FOURIER_TILE), lambda j: (0, 0)),
            pl.BlockSpec((d, 2 * d), lambda j: (0, 0)),
        ],
        out_specs=pl.BlockSpec((FOURIER_TILE, 2 * d), lambda j: (j, 0)),
        out_shape=jax.ShapeDtypeStruct((n_tok, 2 * d), BF16),
        compiler_params=_cparams(("parallel",)),
        name="pre_fourier",
    )(stream, mods4, g_mix, perm, wcs)


def _dft_kernel(ce_ref, se_ref, co_ref, so_ref, ue_ref, uo_ref, o_ref, acce_ref, acco_ref, *, d):
    k = pl.program_id(2)

    @pl.when(k == 0)
    def _():
        acce_ref[...] = jnp.zeros_like(acce_ref)
        acco_ref[...] = jnp.zeros_like(acco_ref)

    acce_ref[...] += (jnp.dot(ce_ref[...], ue_ref[:, :d], preferred_element_type=F32)
                      - jnp.dot(se_ref[...], ue_ref[:, d:], preferred_element_type=F32))
    acco_ref[...] += (jnp.dot(co_ref[...], uo_ref[:, :d], preferred_element_type=F32)
                      - jnp.dot(so_ref[...], uo_ref[:, d:], preferred_element_type=F32))

    @pl.when(k == pl.num_programs(2) - 1)
    def _():
        o_ref[0] = (acce_ref[...] + acco_ref[...]).astype(o_ref.dtype)
        o_ref[1] = (acce_ref[...] - acco_ref[...]).astype(o_ref.dtype)


def _dft_tables(n, rows=None, col_step=1, col_off=0):
    rows = n if rows is None else rows
    cols = n // col_step
    scale = n ** -0.5

    def tables(n_rows, step):
        k = lax.broadcasted_iota(I32, (n_rows, cols), 0) * step
        c = lax.broadcasted_iota(I32, (n_rows, cols), 1) * col_step + col_off
        ang = ((k * c) % n).astype(F32) * (2.0 * jnp.pi / n)
        return jnp.cos(ang), jnp.sin(ang)

    coarse = 64
    if rows <= 4 * coarse:
        c, s = tables(rows, 1)
        return c * scale, s * scale
    ca, sa = tables(rows // coarse, coarse)
    cb, sb = tables(coarse, 1)
    ca, sa = ca[:, None, :] * scale, sa[:, None, :] * scale
    cos = (ca * cb[None] - sa * sb[None]).reshape(rows, cols)
    sin = (sa * cb[None] + ca * sb[None]).reshape(rows, cols)
    return cos, sin


def _dft(u, tok0, seq, batch, d):
    half = seq // 2
    ce, se = _dft_tables(seq, half, 2, 0)
    co, so = _dft_tables(seq, half, 2, 1)
    tabs = [t.astype(BF16) for t in (ce, se, co, so)]
    kb = min(FOURIER_TILE // 2, half)
    ksteps = half // kb
    tm = min(half, 1024)
    per_tile = FOURIER_TILE // kb

    def even_block(b, k):
        tok = tok0 + b * seq + k * FOURIER_TILE
        return (tok // FOURIER_TILE) * per_tile + (tok % FOURIER_TILE) // (2 * kb)

    tspec = pl.BlockSpec((tm, kb), lambda b, m, k: (m, k))
    out = pl.pallas_call(
        functools.partial(_dft_kernel, d=d),
        grid=(batch, half // tm, ksteps),
        in_specs=[
            tspec, tspec, tspec, tspec,
            pl.BlockSpec((kb, 2 * d), lambda b, m, k: (even_block(b, k), 0)),
            pl.BlockSpec((kb, 2 * d), lambda b, m, k: (even_block(b, k) + per_tile // 2, 0)),
        ],
        out_specs=pl.BlockSpec((None, 2, tm, d), lambda b, m, k: (b, 0, m, 0)),
        out_shape=jax.ShapeDtypeStruct((batch, 2, half, d), BF16),
        scratch_shapes=[pltpu.VMEM((tm, d), F32), pltpu.VMEM((tm, d), F32)],
        compiler_params=_cparams(("parallel", "parallel", "arbitrary")),
        name="dft",
    )(*tabs, u, u)
    return out.reshape(batch * seq, d)


def _pre_mla_kernel(s_ref, mod_ref, g_ref, win_ref, gqa_ref, gkva_ref, wqb_ref, wqbs_ref, wkvk_ref, wkvv_ref,
                    cos_ref, sin_ref, q_ref, k_ref, v_ref, *, scale):
    m = mod_ref[...]
    h = _norm_mod(s_ref[...], g_ref[...], m[0:1], m[1:2])
    a = jnp.dot(h.astype(BF16), win_ref[...], preferred_element_type=F32)
    aq = (_rms(a[:, :MLA_Q_LORA]) * gqa_ref[...]).astype(BF16)
    ckv = (_rms(a[:, MLA_Q_LORA:MLA_Q_LORA + MLA_KV_LORA]) * gkva_ref[...]).astype(BF16)
    o = MLA_Q_LORA + MLA_KV_LORA
    cos = cos_ref[...]
    sin = sin_ref[...]
    kpe = a[:, o:o + MLA_SLOT] * cos + a[:, o + MLA_SLOT:o + 2 * MLA_SLOT] * sin
    cos_h = jnp.tile(cos, (1, MLA_HEADS))
    sin_h = jnp.tile(sin, (1, MLA_HEADS))
    q = jnp.dot(aq, wqb_ref[...], preferred_element_type=F32)
    qs = jnp.dot(aq, wqbs_ref[...], preferred_element_type=F32)
    q_ref[...] = ((q * cos_h + qs * sin_h) * scale).astype(BF16)
    kk = jnp.dot(ckv, wkvk_ref[...], preferred_element_type=F32)
    k_ref[...] = (kk + jnp.tile(kpe, (1, MLA_HEADS))).astype(BF16)
    v_ref[...] = lax.dot_general(wkvv_ref[...], ckv, (((1,), (1,)), ((), ())), preferred_element_type=F32).astype(BF16)


def _rope_perm(rot):
    sec = rot // 2
    half = sec // 2
    d = jnp.arange(rot)
    first = (d % sec) < half
    partner = jnp.where(first, d + half, d - half)
    sign = jnp.where(first, -1.0, 1.0).astype(F32)
    return partner, sign


def _rope_tables(seq, rot, tile):
    rows = seq // GRID_W
    row = jnp.repeat(jnp.arange(rows, dtype=F32), GRID_W)
    col = jnp.tile(jnp.arange(GRID_W, dtype=F32), rows)
    n_freq = rot // 4
    inv_freq = ROPE_THETA ** (-jnp.arange(n_freq, dtype=F32) / n_freq)
    ang = jnp.stack([row[:, None] * inv_freq, col[:, None] * inv_freq], axis=1)
    cos = jnp.cos(ang)
    sin = jnp.sin(ang)
    cos_full = jnp.concatenate([cos, cos], axis=-1).reshape(seq, rot)
    sin_full = jnp.concatenate([sin, sin], axis=-1).reshape(seq, rot)
    _, sign = _rope_perm(rot)
    sin_full = sin_full * sign
    ident_c = jnp.ones((tile, rot), F32)
    ident_s = jnp.zeros((tile, rot), F32)
    assert seq % tile == 0
    return jnp.concatenate([cos_full, ident_c], 0), jnp.concatenate([sin_full, ident_s], 0)


def _pre_mla(stream, mods4, g_mix, w_in, g_qa, w_qb, g_kva, w_kvb, geom, layer):
    d = stream.shape[1]
    hd = MLA_NOPE + MLA_ROPE
    partner, _ = _rope_perm(MLA_ROPE)
    o = MLA_Q_LORA + MLA_KV_LORA
    kpe_w = w_in[:, o:]
    z_lo = jnp.zeros((d, MLA_NOPE), F32)
    z_hi = jnp.zeros((d, MLA_SLOT - hd), F32)
    w_in_ext = jnp.concatenate([w_in[:, :o], z_lo, kpe_w, z_hi, z_lo, kpe_w[:, partner], z_hi], axis=1).astype(BF16)
    wq = w_qb.reshape(MLA_Q_LORA, MLA_HEADS, hd)
    zq = jnp.zeros((MLA_Q_LORA, MLA_HEADS, MLA_SLOT - hd), F32)
    wq_p = jnp.concatenate([wq, zq], axis=2).reshape(MLA_Q_LORA, MLA_HEADS * MLA_SLOT).astype(BF16)
    wq_s = jnp.concatenate([jnp.zeros_like(wq[:, :, :MLA_NOPE]), wq[:, :, MLA_NOPE:][:, :, partner], zq], axis=2)
    wq_s = wq_s.reshape(MLA_Q_LORA, MLA_HEADS * MLA_SLOT).astype(BF16)
    wkv = w_kvb.reshape(MLA_KV_LORA, MLA_HEADS, MLA_NOPE + MLA_V)
    zk = jnp.zeros((MLA_KV_LORA, MLA_HEADS, MLA_SLOT - MLA_NOPE), F32)
    wkv_k = jnp.concatenate([wkv[:, :, :MLA_NOPE], zk], axis=2).reshape(MLA_KV_LORA, MLA_HEADS * MLA_SLOT).astype(BF16)
    wkv_v = wkv[:, :, MLA_NOPE:].reshape(MLA_KV_LORA, MLA_HEADS * MLA_V).T.astype(BF16)
    cos, sin = _rope_tables(geom.seq, MLA_ROPE, PROJ_TILE)
    rows = cos.shape[0]
    cos_slot = jnp.concatenate([jnp.ones((rows, MLA_NOPE), F32), cos, jnp.ones((rows, MLA_SLOT - hd), F32)], axis=1)
    sin_slot = jnp.concatenate([jnp.zeros((rows, MLA_NOPE), F32), sin, jnp.zeros((rows, MLA_SLOT - hd), F32)], axis=1)
    n_tiles = geom.n_tok // PROJ_TILE
    wq_w = MLA_HEADS * MLA_SLOT
    wv_w = MLA_HEADS * MLA_V
    full = lambda a: pl.BlockSpec(a.shape, lambda j: (0,) * a.ndim)
    g_qa2, g_kva2 = g_qa.reshape(1, -1), g_kva.reshape(1, -1)
    return pl.pallas_call(
        functools.partial(_pre_mla_kernel, scale=float(hd) ** -0.5 * LOG2_E),
        grid=(n_tiles,),
        in_specs=[
            pl.BlockSpec((PROJ_TILE, d), lambda j: (j, 0)),
            _mod_spec(geom, layer, d, PROJ_TILE),
            pl.BlockSpec((1, d), lambda j: (0, 0)),
            full(w_in_ext), full(g_qa2), full(g_kva2), full(wq_p), full(wq_s), full(wkv_k), full(wkv_v),
            pl.BlockSpec((PROJ_TILE, MLA_SLOT), lambda j: (geom.pos_tile(j, PROJ_TILE), 0)),
            pl.BlockSpec((PROJ_TILE, MLA_SLOT), lambda j: (geom.pos_tile(j, PROJ_TILE), 0)),
        ],
        out_specs=[
            pl.BlockSpec((PROJ_TILE, wq_w), lambda j: (j, 0)),
            pl.BlockSpec((PROJ_TILE, wq_w), lambda j: (j, 0)),
            pl.BlockSpec((wv_w, PROJ_TILE), lambda j: (0, j)),
        ],
        out_shape=[
            jax.ShapeDtypeStruct((n_tiles * PROJ_TILE, wq_w), BF16),
            jax.ShapeDtypeStruct((n_tiles * PROJ_TILE, wq_w), BF16),
            jax.ShapeDtypeStruct((wv_w, n_tiles * PROJ_TILE), BF16),
        ],
        compiler_params=_cparams(("parallel",)),
        name="pre_mla",
    )(stream, mods4, g_mix, w_in_ext, g_qa2, g_kva2, wq_p, wq_s, wkv_k, wkv_v, cos_slot, sin_slot)


def _split_hi_lo(x):
    hi = x.astype(BF16)
    lo = (x - hi.astype(F32)).astype(BF16)
    return hi, lo


def _pre_gqa_kernel(s_ref, mod_ref, g_ref, w_ref, wvt_ref, bd_ref, cq_ref, sq_ref, ck_ref, sk_ref, q_ref, k_ref, v_ref, *,
                    scale):
    m = mod_ref[...]
    h = _norm_mod(s_ref[...], g_ref[...], m[0:1], m[1:2]).astype(BF16)
    a = jnp.dot(h, w_ref[...], preferred_element_type=F32)
    nq = GQA_HEADS * GQA_HEAD_DIM
    nkv = GQA_KV_HEADS * GQA_HEAD_DIM
    aq, ak = a[:, :nq], a[:, nq:nq + nkv]
    aqs, aks = a[:, nq + nkv:2 * nq + nkv], a[:, 2 * nq + nkv:]
    bd = bd_ref[...]

    def head_rs(x, width):
        hi, lo = _split_hi_lo(x * x)
        b = bd[:width, :width]
        ssq = jnp.dot(hi, b, preferred_element_type=F32) + jnp.dot(lo, b, preferred_element_type=F32)
        return lax.rsqrt(ssq * (1.0 / GQA_HEAD_DIM) + NORM_EPS)

    reps_q = nq // cq_ref.shape[1]
    reps_k = nkv // ck_ref.shape[1]
    q = head_rs(aq, nq) * (aq * jnp.tile(cq_ref[...], (1, reps_q)) + aqs * jnp.tile(sq_ref[...], (1, reps_q)))
    q_ref[...] = (q * scale).astype(BF16)
    k = head_rs(ak, nkv) * (ak * jnp.tile(ck_ref[...], (1, reps_k)) + aks * jnp.tile(sk_ref[...], (1, reps_k)))
    k = k.astype(BF16)
    for g in range(GQA_KV_HEADS):
        k_ref[g] = k[:, g * GQA_HEAD_DIM:(g + 1) * GQA_HEAD_DIM]
    v_ref[...] = lax.dot_general(wvt_ref[...], h, (((1,), (1,)), ((), ())), preferred_element_type=F32).astype(BF16)


def _pre_gqa(stream, mods4, g_mix, w_qkv, g_q, g_k, geom, layer):
    d = stream.shape[1]
    nq = GQA_HEADS * GQA_HEAD_DIM
    nkv = GQA_KV_HEADS * GQA_HEAD_DIM
    partner, _ = _rope_perm(GQA_HEAD_DIM)
    wq = w_qkv[:, :nq].reshape(d, GQA_HEADS, GQA_HEAD_DIM)
    wk = w_qkv[:, nq:nq + nkv].reshape(d, GQA_KV_HEADS, GQA_HEAD_DIM)
    w_ext = jnp.concatenate([w_qkv[:, :nq + nkv], wq[:, :, partner].reshape(d, nq), wk[:, :, partner].reshape(d, nkv)],
                            axis=1).astype(BF16)
    w_vt = w_qkv[:, nq + nkv:].T.astype(BF16)
    cos, sin = _rope_tables(geom.seq, GQA_HEAD_DIM, PROJ_TILE)
    per = LANES // GQA_HEAD_DIM
    cq = jnp.tile(cos * g_q[None, :], (1, per))
    sq = jnp.tile(sin * g_q[partner][None, :], (1, per))
    ck = jnp.tile(cos * g_k[None, :], (1, per))
    sk = jnp.tile(sin * g_k[partner][None, :], (1, per))
    hid = jnp.arange(nq) // GQA_HEAD_DIM
    bd = (hid[:, None] == hid[None, :]).astype(BF16)
    n_tiles = geom.n_tok // PROJ_TILE
    full = lambda a: pl.BlockSpec(a.shape, lambda j: (0,) * a.ndim)
    tab = pl.BlockSpec((PROJ_TILE, LANES), lambda j: (geom.pos_tile(j, PROJ_TILE), 0))
    return pl.pallas_call(
        functools.partial(_pre_gqa_kernel, scale=float(GQA_HEAD_DIM) ** -0.5 * LOG2_E),
        grid=(n_tiles,),
        in_specs=[
            pl.BlockSpec((PROJ_TILE, d), lambda j: (j, 0)),
            _mod_spec(geom, layer, d, PROJ_TILE),
            pl.BlockSpec((1, d), lambda j: (0, 0)),
            full(w_ext), full(w_vt), full(bd), tab, tab, tab, tab,
        ],
        out_specs=[
            pl.BlockSpec((PROJ_TILE, nq), lambda j: (j, 0)),
            pl.BlockSpec((GQA_KV_HEADS, PROJ_TILE, GQA_HEAD_DIM), lambda j: (0, j, 0)),
            pl.BlockSpec((nkv, PROJ_TILE), lambda j: (0, j)),
        ],
        out_shape=[
            jax.ShapeDtypeStruct((n_tiles * PROJ_TILE, nq), BF16),
            jax.ShapeDtypeStruct((GQA_KV_HEADS, n_tiles * PROJ_TILE, GQA_HEAD_DIM), BF16),
            jax.ShapeDtypeStruct((nkv, n_tiles * PROJ_TILE), BF16),
        ],
        compiler_params=_cparams(("parallel",)),
        name="pre_gqa",
    )(stream, mods4, g_mix, w_ext, w_vt, bd, cq, sq, ck, sk)


def _attn_kernel(q_ref, kl_ref, vl_ref, kc_ref, vc_ref, o_ref, s_ref, *, n_lat_steps, **head_args):
    t = pl.program_id(2)

    @pl.when(t < n_lat_steps)
    def _():
        _attn_tile(q_ref, kl_ref, vl_ref, kc_ref, vc_ref, o_ref, s_ref, has_lat=True, **head_args)

    @pl.when(t >= n_lat_steps)
    def _():
        _attn_tile(q_ref, kl_ref, vl_ref, kc_ref, vc_ref, o_ref, s_ref, has_lat=False, **head_args)


def _attn_tile(q_ref, kl_ref, vl_ref, kc_ref, vc_ref, o_ref, s_ref, *, n_heads, kv_heads, dk, dv, has_lat):
    nt = (((1,), (1,)), ((), ()))
    tq = q_ref.shape[0]
    chunks, row = [], 0
    for k_ref, v_ref in ([(kc_ref, vc_ref), (kl_ref, vl_ref)] if has_lat else [(kc_ref, vc_ref)]):
        n_keys = k_ref.shape[-2]
        for c0 in range(0, n_keys, KEY_CHUNK):
            kc = min(KEY_CHUNK, n_keys - c0)
            chunks.append((k_ref, v_ref, c0, row, kc))
            row += kc
    mx = [None] * n_heads
    den = [None] * n_heads
    acc = [None] * n_heads
    for phase in range(n_heads + 1):
        for k_ref, v_ref, c0, row, kc in chunks:
            if phase < n_heads:
                g = phase
                gk = g * kv_heads // n_heads
                q = q_ref[:, g * dk:(g + 1) * dk]
                k = k_ref[gk, c0:c0 + kc, :] if k_ref.ndim == 3 else k_ref[c0:c0 + kc, gk * dk:(gk + 1) * dk]
                s = lax.dot_general(k, q, nt, preferred_element_type=F32)
                s_ref[g % 2, row:row + kc, :] = s
                cmax = jnp.max(s, axis=0, keepdims=True)
                mx[g] = cmax if mx[g] is None else jnp.maximum(mx[g], cmax)
            if phase > 0:
                g = phase - 1
                gk = g * kv_heads // n_heads
                p = jnp.exp2(s_ref[g % 2, row:row + kc, :] - mx[g])
                psum = jnp.sum(p, axis=0, keepdims=True)
                pv = jnp.dot(v_ref[gk * dv:(gk + 1) * dv, c0:c0 + kc], p.astype(BF16), preferred_element_type=F32)
                den[g] = psum if den[g] is None else den[g] + psum
                acc[g] = pv if acc[g] is None else acc[g] + pv
    o_all = jnp.concatenate([acc[g] / den[g] for g in range(n_heads)], axis=0)
    o_ref[...] = jnp.transpose(o_all).astype(o_ref.dtype)


def _attention(q, k, vt, geom, *, groups, n_heads, kv_heads, dk, dv, k_head_major, context_queries, single_buffer_kv=False):
    wq, wk, wv = n_heads * dk, kv_heads * dk, kv_heads * dv
    wo = n_heads * dv
    batch = geom.batch
    nc = geom.n_ctx
    ctx_blk0 = geom.n_lat // nc
    assert geom.n_lat % nc == 0

    mode = dict(pipeline_mode=pl.Buffered(1)) if single_buffer_kv else {}

    def kspec(rows, tok_blk):
        if k_head_major:
            return pl.BlockSpec((kv_heads, rows, dk), lambda b, h, *_: (h, tok_blk(b), 0), **mode)
        return pl.BlockSpec((rows, wk), lambda b, h, *_: (tok_blk(b), h), **mode)

    def vspec(rows, tok_blk):
        return pl.BlockSpec((wv, rows), lambda b, h, *_: (h, tok_blk(b)), **mode)

    tq = TT
    qt = geom.seq // tq
    assert nc == tq or not context_queries
    steps = qt + 1 if context_queries else qt
    qrow = lambda b, t: jnp.where(t < qt, b * qt + t, ctx_blk0 + b)
    return pl.pallas_call(
        functools.partial(_attn_kernel, n_lat_steps=qt, n_heads=n_heads, kv_heads=kv_heads, dk=dk, dv=dv),
        grid=(batch, groups, steps),
        in_specs=[
            pl.BlockSpec((tq, wq), lambda b, h, t: (qrow(b, t), h)),
            kspec(geom.seq, lambda b: b),
            vspec(geom.seq, lambda b: b),
            kspec(nc, lambda b: ctx_blk0 + b),
            vspec(nc, lambda b: ctx_blk0 + b),
        ],
        out_specs=pl.BlockSpec((tq, wo), lambda b, h, t: (qrow(b, t), h)),
        out_shape=jax.ShapeDtypeStruct((geom.n_tok if context_queries else geom.n_lat, groups * wo), BF16),
        scratch_shapes=[pltpu.VMEM((2, geom.seq + nc, tq), F32)],
        compiler_params=_cparams(("parallel", "parallel", "arbitrary")),
        name="attention",
    )(q, k, vt, k, vt)


def _post_kernel(o_ref, wo_ref, s_ref, mod_ref, g_ref, wrh_ref, wrl_ref, br_ref, s_out, h2_out, idx_out, w_out, cnt_out, *, n_experts):
    m = mod_ref[...]
    s = s_ref[...] + m[2:3] * jnp.dot(o_ref[...], wo_ref[...], preferred_element_type=F32)
    s_out[...] = s
    h2 = _norm_mod(s, g_ref[...], m[3:4], m[4:5])
    h2_out[...] = h2.astype(BF16)
    h2_hi, h2_lo = _split_hi_lo(h2)
    logits = (jnp.dot(h2_hi, wrh_ref[...], preferred_element_type=F32) + jnp.dot(h2_lo, wrh_ref[...], preferred_element_type=F32)
              + jnp.dot(h2_hi, wrl_ref[...], preferred_element_type=F32) + br_ref[...])
    lane = lax.broadcasted_iota(I32, logits.shape, 1).astype(F32)
    work = jnp.where(lane < n_experts, logits, -jnp.inf)
    vals, hits = [], []
    for k in range(TOP_K):
        mx = jnp.max(work, axis=-1, keepdims=True)
        sel = jnp.min(jnp.where(work == mx, lane, float(LANES)), axis=-1, keepdims=True)
        hit = lane == sel
        vals.append(mx)
        hits.append(hit)
        work = jnp.where(hit, -jnp.inf, work)
    es = [jnp.exp(v - vals[0]) for v in vals]
    den = es[0] + es[1] + es[2] + es[3]
    w_slab = jnp.zeros(logits.shape, F32)
    for k in range(TOP_K):
        w_slab = jnp.where(lane == k, es[k] / den, w_slab)
    w_out[...] = w_slab
    onehot = jnp.where(hits[0] | hits[1] | hits[2] | hits[3], 1.0, 0.0).astype(BF16)
    tr = lax.broadcasted_iota(I32, (TT, TT), 0)
    tc = lax.broadcasted_iota(I32, (TT, TT), 1)
    earlier = jnp.where(tc < tr, 1.0, 0.0).astype(BF16)
    er = lax.broadcasted_iota(I32, (LANES, LANES), 0)
    ec = lax.broadcasted_iota(I32, (LANES, LANES), 1)
    lower = jnp.where(er < ec, 1.0, 0.0).astype(BF16)
    bases = []
    for t in range(onehot.shape[0] // TT):
        oh = onehot[t * TT:(t + 1) * TT]
        rank = jnp.dot(earlier, oh, preferred_element_type=F32)
        cnt = jnp.sum(oh.astype(F32), axis=0, keepdims=True)
        cp = jnp.floor((cnt + (SUBLANES - 1)) * (1.0 / SUBLANES)) * SUBLANES
        seg = jnp.dot(jnp.broadcast_to(cp, (SUBLANES, LANES)).astype(BF16), lower, preferred_element_type=F32)[0:1]
        bases.append(seg + rank)
        cnt_out[t] = jnp.broadcast_to(cnt, (SUBLANES, LANES)).astype(I32)
    base = jnp.concatenate(bases, axis=0)
    lp_slab = jnp.zeros(logits.shape, F32)
    for k in range(TOP_K):
        lp_slab = jnp.where(lane == k, jnp.sum(jnp.where(hits[k], base, 0.0), axis=-1, keepdims=True), lp_slab)
    idx_out[...] = lp_slab.astype(I32)


def _post(o, w_o, stream, mods4, g_ffn, w_router, b_router, geom, layer, n_tiles):
    d = stream.shape[1]
    wo_w = o.shape[1]
    n_experts = w_router.shape[1]
    wr_hi, wr_lo = _split_hi_lo(jnp.concatenate([w_router, jnp.zeros((d, LANES - n_experts), F32)], axis=1))
    br = jnp.concatenate([b_router, jnp.zeros((LANES - n_experts,), F32)]).reshape(1, LANES)
    n = n_tiles * TT
    return pl.pallas_call(
        functools.partial(_post_kernel, n_experts=n_experts),
        grid=(n // PROJ_TILE,),
        in_specs=[
            pl.BlockSpec((PROJ_TILE, wo_w), lambda j: (j, 0)),
            pl.BlockSpec((wo_w, d), lambda j: (0, 0)),
            pl.BlockSpec((PROJ_TILE, d), lambda j: (j, 0)),
            _mod_spec(geom, layer, d, PROJ_TILE),
            pl.BlockSpec((1, d), lambda j: (0, 0)),
            pl.BlockSpec((d, LANES), lambda j: (0, 0)),
            pl.BlockSpec((d, LANES), lambda j: (0, 0)),
            pl.BlockSpec((1, LANES), lambda j: (0, 0)),
        ],
        out_specs=[
            pl.BlockSpec((PROJ_TILE, d), lambda j: (j, 0)),
            pl.BlockSpec((PROJ_TILE, d), lambda j: (j, 0)),
            pl.BlockSpec((PROJ_TILE, LANES), lambda j: (j, 0)),
            pl.BlockSpec((PROJ_TILE, LANES), lambda j: (j, 0)),
            pl.BlockSpec((PROJ_TILE // TT, SUBLANES, LANES), lambda j: (j, 0, 0)),
        ],
        out_shape=[
            jax.ShapeDtypeStruct((n, d), F32),
            jax.ShapeDtypeStruct((n, d), BF16),
            jax.ShapeDtypeStruct((n, LANES), I32),
            jax.ShapeDtypeStruct((n, LANES), F32),
            jax.ShapeDtypeStruct((n_tiles, SUBLANES, LANES), I32),
        ],
        compiler_params=_cparams(("parallel",)),
        name="post",
    )(o, w_o, stream, mods4, g_ffn, wr_hi, wr_lo, br)


def _local_rows(n_experts):
    return _round_up(TT * TOP_K + n_experts * (SUBLANES - 1), LANES)


def _n_blocks(n_tok, n_tiles, n_experts):
    rows = n_tok * TOP_K + n_experts * n_tiles * (SUBLANES - 1) + n_experts * (EXPERT_BLOCK - 1)
    return pl.cdiv(rows, EXPERT_BLOCK)


def _route_meta(cnt, lpos, n_experts):
    n_tiles = cnt.shape[0]
    n_blocks = _n_blocks(n_tiles * TT, n_tiles, n_experts)
    cp = _round_up(cnt, SUBLANES)
    seg = jnp.cumsum(cp, axis=1) - cp
    run_rows = cp.sum(axis=0)
    reg = _round_up(run_rows, EXPERT_BLOCK)
    reg_end = jnp.cumsum(reg)
    reg_start = reg_end - reg
    off = reg_start[None, :] + jnp.cumsum(cp, axis=0) - cp
    n_used = (reg_end[-1] // EXPERT_BLOCK).astype(I32)
    bstart = jnp.arange(n_blocks, dtype=I32) * EXPERT_BLOCK
    last_start = jnp.maximum(n_used - 1, 0) * EXPERT_BLOCK
    be = (jnp.minimum(bstart, last_start)[:, None] >= reg_end[None, :]).sum(axis=1).astype(I32)
    be = jnp.minimum(be, n_experts - 1)
    per_big = cp // BIG_CHUNK
    per_small = (cp % BIG_CHUNK) // SUBLANES
    tail = per_big * BIG_CHUNK
    return dict(
        big_list=_copy_list(per_big, seg, off, BIG_CHUNK, _max_big(n_experts)),
        small_list=_copy_list(per_small, seg + tail, off + tail, SUBLANES, _max_small(n_experts)),
        n_big=per_big.sum(axis=1).astype(I32), n_small=per_small.sum(axis=1).astype(I32),
        gap_start=(reg_start + run_rows).astype(I32), gap=(reg - run_rows).astype(I32),
        lpos=lpos.astype(I32), block_expert=be, n_used=n_used.reshape(1), n_blocks=n_blocks,
    )


def _max_big(n_experts):
    return _local_rows(n_experts) // BIG_CHUNK


def _max_small(n_experts):
    return n_experts * (BIG_CHUNK // SUBLANES - 1)


def _copy_list(per_run, local_row, sorted_row, chunk, max_copies):
    ends = jnp.cumsum(per_run, axis=1)[:, None, :]
    first = ends - per_run[:, None, :]
    k = jnp.arange(max_copies, dtype=I32)[None, :, None]
    mine = (first <= k) & (k < ends)
    pick = lambda v: jnp.sum(jnp.where(mine, v[:, None, :], 0), axis=2)
    j = k[:, :, 0] - pick(first[:, 0, :])
    src = pick(local_row) + chunk * j
    dst = pick(sorted_row) + chunk * j
    return (dst * LIST_RADIX + src).reshape(-1).astype(I32)


def _run_copies(tile, nbig_ref, nsmall_ref, big_ref, small_ref, max_big, max_small, make_big, make_small):
    def start(make, packed, priority):
        local_row = pl.multiple_of(packed & (LIST_RADIX - 1), SUBLANES)
        sorted_row = pl.multiple_of(lax.shift_right_logical(packed, LIST_SHIFT), SUBLANES)
        make(local_row, sorted_row).start(priority=priority)

    def big(k, c):
        start(make_big, big_ref[tile * max_big + k], 0)
        return c

    def small(k, c):
        start(make_small, small_ref[tile * max_small + k], 1)
        return c

    lax.fori_loop(0, nbig_ref[tile], big, 0)
    lax.fori_loop(0, nsmall_ref[tile], small, 0)


def _wait_copies(tile, nbig_ref, nsmall_ref, make_big, make_small):
    def wb(j, c):
        make_big(0, 0).wait()
        return c

    def ws(j, c):
        make_small(0, 0).wait()
        return c

    lax.fori_loop(0, nbig_ref[tile], wb, 0)
    lax.fori_loop(0, nsmall_ref[tile], ws, 0)


def _dispatch_kernel(big_ref, small_ref, nbig_ref, nsmall_ref, gs_ref, gap_ref, nu_ref,
                     lpt_ref, h2_ref, xs_ref, buf_ref, zero_ref, sem, zsem, *, n_experts, local_rows, n_blocks):
    i = pl.program_id(0)
    n = pl.num_programs(0)
    slot = i % 2
    tps = MOE_TILES_PER_STEP

    def copies(sl, sub):
        def big(src, dst):
            return pltpu.make_async_copy(buf_ref.at[sl, sub, pl.ds(src, BIG_CHUNK), :], xs_ref.at[pl.ds(dst, BIG_CHUNK), :],
                                         sem.at[sl])

        def small(src, dst):
            return pltpu.make_async_copy(buf_ref.at[sl, sub, pl.ds(src, SUBLANES), :], xs_ref.at[pl.ds(dst, SUBLANES), :],
                                         sem.at[sl])

        return big, small

    def wait_step(step, sl):
        for sub in range(tps):
            _wait_copies(step * tps + sub, nbig_ref, nsmall_ref, *copies(sl, sub))

    @pl.when(i >= 2)
    def _():
        wait_step(i - 2, slot)

    for sub in range(tps):
        lpt = lpt_ref[sub]
        rows = lax.broadcasted_iota(I32, (local_rows, TT), 0)
        hit = rows == lpt[0:1, :]
        for k in range(1, TOP_K):
            hit = hit | (rows == lpt[k:k + 1, :])
        p = jnp.where(hit, 1.0, 0.0).astype(BF16)
        buf_ref[slot, sub] = _pack_bf16_pairs(jnp.dot(p, h2_ref[sub * TT:(sub + 1) * TT, :], preferred_element_type=F32))
        _run_copies(i * tps + sub, nbig_ref, nsmall_ref, big_ref, small_ref, _max_big(n_experts), _max_small(n_experts),
                    *copies(slot, sub))

    @pl.when(i == n - 1)
    def _():
        zero_ref[...] = jnp.zeros_like(zero_ref)
        sizes = []
        size = EXPERT_BLOCK // 2
        while size >= SUBLANES:
            sizes.append(size)
            size //= 2

        def zcopy(dst, size):
            return pltpu.make_async_copy(zero_ref.at[pl.ds(0, size), :], xs_ref.at[pl.ds(dst, size), :], zsem)

        def per_expert(e, carry):
            gap = gap_ref[e]
            pos = gs_ref[e]
            for size in sizes:
                take = (gap & size) != 0

                @pl.when(take)
                def _():
                    zcopy(pl.multiple_of(pos, SUBLANES), size).start()

                pos = pos + jnp.where(take, size, 0)
            return carry

        lax.fori_loop(0, n_experts, per_expert, 0)

        zrows = zero_ref.shape[0]
        per_block = EXPERT_BLOCK // zrows
        tail_copies = (n_blocks - nu_ref[0]) * per_block

        def tail(t, carry):
            zcopy(pl.multiple_of(nu_ref[0] * EXPERT_BLOCK + t * zrows, SUBLANES), zrows).start()
            return carry

        lax.fori_loop(0, tail_copies, tail, 0)

        def per_expert_wait(e, carry):
            gap = gap_ref[e]
            for size in sizes:
                @pl.when((gap & size) != 0)
                def _():
                    zcopy(0, size).wait()
            return carry

        lax.fori_loop(0, n_experts, per_expert_wait, 0)

        def tail_wait(t, carry):
            zcopy(0, zrows).wait()
            return carry

        lax.fori_loop(0, tail_copies, tail_wait, 0)

        @pl.when(i >= 1)
        def _():
            wait_step(i - 1, 1 - slot)

        wait_step(i, slot)


def _dispatch(h2, meta, n_tiles, n_experts):
    n, d = h2.shape
    local_rows = _local_rows(n_experts)
    n_rows = meta["n_blocks"] * EXPERT_BLOCK
    lpt = meta["lpos"].reshape(n_tiles, TT, TOP_K).transpose(0, 2, 1)
    tps = MOE_TILES_PER_STEP
    assert n_tiles % tps == 0
    grid_spec = pltpu.PrefetchScalarGridSpec(
        num_scalar_prefetch=7,
        grid=(n_tiles // tps,),
        in_specs=[
            pl.BlockSpec((tps, TOP_K, TT), lambda j, *_: (j, 0, 0)),
            pl.BlockSpec((tps * TT, d), lambda j, *_: (j, 0)),
        ],
        out_specs=pl.BlockSpec(memory_space=pl.ANY),
        scratch_shapes=[
            pltpu.VMEM((2, tps, local_rows, d // 2), U32),
            pltpu.VMEM((EXPERT_BLOCK // 2, d // 2), U32),
            pltpu.SemaphoreType.DMA((2,)),
            pltpu.SemaphoreType.DMA(()),
        ],
    )
    return pl.pallas_call(
        functools.partial(_dispatch_kernel, n_experts=n_experts, local_rows=local_rows, n_blocks=meta["n_blocks"]),
        grid_spec=grid_spec,
        out_shape=jax.ShapeDtypeStruct((n_rows, d // 2), U32),
        compiler_params=_cparams(("arbitrary",)),
        name="moe_dispatch",
    )(meta["big_list"], meta["small_list"], meta["n_big"], meta["n_small"], meta["gap_start"], meta["gap"], meta["n_used"],
      lpt, h2)


def _ffn_kernel(be_ref, nu_ref, x_ref, wgu_ref, wd_ref, sel_ref, bg_ref, bu_ref, bd_ref, y_ref, wg_s, wu_s, wd_s):
    b = pl.program_id(0)
    used = b < nu_ref[0]

    @pl.when(used & ((b == 0) | (be_ref[b] != be_ref[jnp.maximum(b - 1, 0)])))
    def _():
        sel = sel_ref[...]
        pair = 2 * LANES
        for c in range(wgu_ref.shape[1] // pair):
            chunk = wgu_ref[:, c * pair:(c + 1) * pair].astype(BF16)
            de = jnp.dot(chunk, sel, preferred_element_type=F32)
            wg_s[:, c * LANES:(c + 1) * LANES] = de[:, :LANES].astype(BF16)
            wu_s[:, c * LANES:(c + 1) * LANES] = de[:, LANES:].astype(BF16)
        wd_s[...] = wd_ref[...].astype(BF16)

    @pl.when(used)
    def _():
        x = _unpack_bf16_pairs(x_ref[...])
        gate = jnp.dot(x, wg_s[...], preferred_element_type=F32) + bg_ref[...]
        up = jnp.dot(x, wu_s[...], preferred_element_type=F32) + bu_ref[...]
        gate = jnp.minimum(gate, SWIGLU_LIMIT)
        up = jnp.clip(up, -SWIGLU_LIMIT, SWIGLU_LIMIT)
        glu = gate / (1.0 + jnp.exp(-SWIGLU_ALPHA * gate))
        act = ((up + 1.0) * glu).astype(BF16)
        y = jnp.dot(act, wd_s[...], preferred_element_type=F32) + bd_ref[...]
        y_ref[...] = _pack_bf16_pairs(y.astype(BF16).astype(F32))

    @pl.when(jnp.logical_not(used))
    def _():
        y_ref[...] = jnp.zeros_like(y_ref)


def _ffn(xs, meta, layer, w_gu, w_down, b_gate, b_up, b_down):
    n_rows = xs.shape[0]
    n_blocks = meta["n_blocks"]
    f, d = w_down.shape[2:]
    r = jnp.arange(2 * LANES)
    src = jnp.where(r < LANES, 2 * r, 2 * (r - LANES) + 1)
    sel = (r[:, None] == src[None, :]).astype(BF16)
    xmap = lambda b, be, nu: (jnp.minimum(b, jnp.maximum(nu[0] - 1, 0)), 0)
    wmap = lambda b, be, nu: (layer, be[b], 0, 0)
    grid_spec = pltpu.PrefetchScalarGridSpec(
        num_scalar_prefetch=2,
        grid=(n_blocks,),
        in_specs=[
            pl.BlockSpec((EXPERT_BLOCK, d // 2), xmap),
            pl.BlockSpec((None, None, d, 2 * f), wmap),
            pl.BlockSpec((None, None, f, d), wmap),
            pl.BlockSpec((2 * LANES, 2 * LANES), lambda b, be, nu: (0, 0)),
            pl.BlockSpec((None, None, 1, f), wmap),
            pl.BlockSpec((None, None, 1, f), wmap),
            pl.BlockSpec((None, None, 1, d), wmap),
        ],
        out_specs=pl.BlockSpec((EXPERT_BLOCK, d // 2), lambda b, be, nu: (b, 0)),
        scratch_shapes=[pltpu.VMEM((d, f), BF16), pltpu.VMEM((d, f), BF16), pltpu.VMEM((f, d), BF16)],
    )
    return pl.pallas_call(
        _ffn_kernel,
        grid_spec=grid_spec,
        out_shape=jax.ShapeDtypeStruct((n_rows, d // 2), U32),
        compiler_params=_cparams(("arbitrary",)),
        name="moe_ffn",
    )(meta["block_expert"], meta["n_used"], xs, w_gu, w_down, sel, b_gate, b_up, b_down)


def _combine_kernel(big_ref, small_ref, nbig_ref, nsmall_ref,
                    lp_ref, w_ref, s_ref, mod_ref, gf_ref, ys_ref, o_ref, buf_ref, sem, *, n_experts, local_rows, final):
    i = pl.program_id(0)
    n = pl.num_programs(0)
    slot = i % 2
    tps = MOE_TILES_PER_STEP

    def copies(sl, sub):
        def big(dst, src):
            return pltpu.make_async_copy(ys_ref.at[pl.ds(src, BIG_CHUNK), :], buf_ref.at[sl, sub, pl.ds(dst, BIG_CHUNK), :],
                                         sem.at[sl])

        def small(dst, src):
            return pltpu.make_async_copy(ys_ref.at[pl.ds(src, SUBLANES), :], buf_ref.at[sl, sub, pl.ds(dst, SUBLANES), :],
                                         sem.at[sl])

        return big, small

    def start_step(step, sl):
        for sub in range(tps):
            _run_copies(step * tps + sub, nbig_ref, nsmall_ref, big_ref, small_ref, _max_big(n_experts),
                        _max_small(n_experts), *copies(sl, sub))

    def wait_step(step, sl):
        for sub in range(tps):
            _wait_copies(step * tps + sub, nbig_ref, nsmall_ref, *copies(sl, sub))

    @pl.when(i == 0)
    def _():
        buf_ref[...] = jnp.zeros_like(buf_ref)
        start_step(0, 0)

    @pl.when(i + 1 < n)
    def _():
        start_step(i + 1, 1 - slot)

    wait_step(i, slot)

    m = mod_ref[...]
    for sub in range(tps):
        rows = slice(sub * TT, (sub + 1) * TT)
        lp = lp_ref[rows, :]
        w = w_ref[rows, :]
        lanes = lax.broadcasted_iota(I32, (TT, local_rows), 1)
        pw = jnp.zeros((TT, local_rows), F32)
        for k in range(TOP_K):
            pw = pw + jnp.where(lanes == lp[:, k:k + 1], w[:, k:k + 1], 0.0)
        y = _unpack_bf16_pairs(buf_ref[slot, sub])
        f = jnp.dot(pw.astype(BF16), y, preferred_element_type=F32)
        s = s_ref[rows, :] + m[5:6] * f
        if final:
            s = _rms(s) * gf_ref[...]
        o_ref[rows, :] = s


def _combine(ys, meta, top_w, stream, mods4, g_final, geom, layer, n_tiles, n_experts, final):
    d = stream.shape[1]
    local_rows = _local_rows(n_experts)
    tps = MOE_TILES_PER_STEP
    assert n_tiles % tps == 0
    grid_spec = pltpu.PrefetchScalarGridSpec(
        num_scalar_prefetch=4,
        grid=(n_tiles // tps,),
        in_specs=[
            pl.BlockSpec((tps * TT, TOP_K), lambda j, *_: (j, 0)),
            pl.BlockSpec((tps * TT, TOP_K), lambda j, *_: (j, 0)),
            pl.BlockSpec((tps * TT, d), lambda j, *_: (j, 0)),
            pl.BlockSpec((None, None, N_MOD, d), lambda j, *_: (layer, geom.group(j, tps * TT), 0, 0)),
            pl.BlockSpec((1, d), lambda j, *_: (0, 0)),
            pl.BlockSpec(memory_space=pl.ANY),
        ],
        out_specs=pl.BlockSpec((tps * TT, d), lambda j, *_: (j, 0)),
        scratch_shapes=[
            pltpu.VMEM((2, tps, local_rows, d // 2), U32),
            pltpu.SemaphoreType.DMA((2,)),
        ],
    )
    return pl.pallas_call(
        functools.partial(_combine_kernel, n_experts=n_experts, local_rows=local_rows, final=final),
        grid_spec=grid_spec,
        out_shape=jax.ShapeDtypeStruct((n_tiles * TT, d), F32),
        compiler_params=_cparams(("arbitrary",)),
        name="moe_combine",
    )(meta["big_list"], meta["small_list"], meta["n_big"], meta["n_small"],
      meta["lpos"], top_w, stream, mods4, g_final, ys)


def kernel(x, c, ctx, c_ctx, w_mod, b_mod, g_mix, g_ffn, g_final, f_w_in, f_w_out, mla_w_in, mla_g_qa, mla_w_qb,
           mla_g_kva, mla_w_kvb, mla_w_o, gqa_w_qkv, gqa_g_q, gqa_g_k, gqa_w_o, moe_w_router, moe_b_router,
           moe_w_gu, moe_b_gu, moe_w_down, moe_b_down):
    batch, seq, d = x.shape
    n_ctx = ctx.shape[1]
    depth = w_mod.shape[0]
    n_experts = moe_w_router.shape[2]
    geom = _Geom(batch, seq, n_ctx)

    n_groups = _round_up(batch + 1, SUBLANES)
    cc = jnp.concatenate([c, c_ctx[None, :], jnp.zeros((n_groups - batch - 1, d), F32)], axis=0)
    mods4 = _mods(cc, w_mod, b_mod).reshape(depth, n_groups, N_MOD, d)

    stream = jnp.concatenate([x.reshape(batch * seq, d), ctx.reshape(batch * n_ctx, d)], axis=0)

    gd = d // FOURIER_GROUPS
    gcos, gsin = _dft_tables(gd)
    eye = jnp.eye(FOURIER_GROUPS, dtype=F32)
    bd_cs = jnp.concatenate([jnp.kron(eye, gcos), jnp.kron(eye, gsin)], axis=1)

    f = moe_w_down.shape[2]
    b_gate = moe_b_gu[:, :, 0::2].reshape(depth, n_experts, 1, f)
    b_up = moe_b_gu[:, :, 1::2].reshape(depth, n_experts, 1, f)
    b_down = moe_b_down.reshape(depth, n_experts, 1, d)

    for i in range(depth):
        kind, j = i % N_MIXERS, i // N_MIXERS
        last = i == depth - 1
        ctx_used = (kind != 0) or (not last)
        n_tiles = geom.n_tiles if ctx_used else geom.n_lat_tiles
        gm = g_mix[i].reshape(1, d)
        gf = g_ffn[i].reshape(1, d)

        if kind == 0:
            wcs = _fold(f_w_in[j], bd_cs, BF16)
            u = _pre_fourier(stream, mods4, gm, wcs, geom, i, n_tiles * TT)
            o = _dft(u, 0, seq, batch, d)
            if ctx_used:
                o = jnp.concatenate([o, _dft(u, geom.n_lat, n_ctx, batch, d)], axis=0)
            w_o = f_w_out[j].astype(BF16)
        elif kind == 1:
            q, k, v = _pre_mla(stream, mods4, gm, mla_w_in[j], mla_g_qa[j], mla_w_qb[j], mla_g_kva[j], mla_w_kvb[j], geom, i)
            o = _attention(q, k, v, geom, groups=1, n_heads=MLA_HEADS, kv_heads=MLA_HEADS, dk=MLA_SLOT, dv=MLA_V,
                           k_head_major=False, context_queries=not last, single_buffer_kv=True)
            w_o = mla_w_o[j].astype(BF16)
        else:
            q, k, v = _pre_gqa(stream, mods4, gm, gqa_w_qkv[j], gqa_g_q[j], gqa_g_k[j], geom, i)
            o = _attention(q, k, v, geom, groups=1, n_heads=GQA_HEADS, kv_heads=GQA_KV_HEADS, dk=GQA_HEAD_DIM,
                           dv=GQA_HEAD_DIM, k_head_major=True, context_queries=not last)
            w_o = gqa_w_o[j].astype(BF16)

        n_moe_tiles = geom.n_lat_tiles if last else geom.n_tiles
        stream, h2, lp_slab, w_slab, cnt_slab = _post(o, w_o, stream, mods4, gf, moe_w_router[i], moe_b_router[i], geom, i,
                                                      n_moe_tiles)
        meta = _route_meta(cnt_slab[:, 0, :n_experts], lp_slab[:, :TOP_K], n_experts)
        xs = _dispatch(h2, meta, n_moe_tiles, n_experts)
        ys = _ffn(xs, meta, i, moe_w_gu, moe_w_down, b_gate, b_up, b_down)
        stream = _combine(ys, meta, w_slab[:, :TOP_K], stream, mods4, g_final.reshape(1, d), geom, i, n_moe_tiles, n_experts,
                          last)

    return stream[:batch * seq].reshape(batch, seq, d)
```
